```python
import math
import jax
import jax.numpy as jnp
from jax import lax
import numpy as np

D_MODEL = 4096
BATCH = 4
SEQ = 2048
DEPTH = 2
DEC_BATCH = 8
DEC_SEQ = 8
PAST_LEN = 16384
PAGE_SIZE = 128

HEAD_DIM = 128
MIX_WIDTH = D_MODEL
GROUP_WIDTH = MIX_WIDTH // 4
N_HEADS_A = GROUP_WIDTH // HEAD_DIM
DIFF_DK = HEAD_DIM // 2
N_HEADS_B = GROUP_WIDTH // HEAD_DIM
GLA_DK = HEAD_DIM // 2
GLA_DV = HEAD_DIM
GLA_RANK = 16
GLA_TAU = 16.0
GLA_CHUNK = 64
CONV_WIDTH = 3
N_HEADS_D = GROUP_WIDTH // HEAD_DIM
CHUNK_D = 128
Q_BLOCK = 128
REL_BUCKETS = 32
REL_MAX_DIST = 128
N_GROUPS = 4
EXPERTS_PER_GROUP = 8
N_EXPERTS = N_GROUPS * EXPERTS_PER_GROUP
TOP_K = 2
EXPERT_HIDDEN = D_MODEL // 4
MOE_BLOCK = 128
DEEPNORM_ALPHA = (2.0 * DEPTH) ** 0.25
DEEPNORM_BETA = (8.0 * DEPTH) ** -0.25
NORM_EPS = 1e-5
MASK_VALUE = -1e30
PROJ_SPLITS = (
    N_HEADS_A * HEAD_DIM,
    N_HEADS_A * HEAD_DIM,
    N_HEADS_A * HEAD_DIM,
    N_HEADS_B * GLA_DK,
    N_HEADS_B * GLA_DK,
    N_HEADS_B * GLA_DV,
    GLA_RANK,
    N_HEADS_B * GLA_DV,
    GROUP_WIDTH,
    GROUP_WIDTH,
    GROUP_WIDTH,
    GROUP_WIDTH,
    GROUP_WIDTH,
)
PROJ_WIDTH = sum(PROJ_SPLITS)

kernel_name = 'hybrid_diff_gla_conv_gmlp_hmoe_step'


def _layer_norm(x, g, b):
    xf = x.astype(jnp.float32)
    mu = jnp.mean(xf, axis=-1, keepdims=True)
    var = jnp.mean(jnp.square(xf - mu), axis=-1, keepdims=True)
    return ((xf - mu) * lax.rsqrt(var + NORM_EPS) * g + b).astype(x.dtype)


def _rms_norm(x, g):
    xf = x.astype(jnp.float32)
    return (xf * lax.rsqrt(jnp.mean(xf * xf, axis=-1, keepdims=True) + NORM_EPS) * g).astype(x.dtype)


def _split_proj(p):
    idx, acc = [], 0
    for w in PROJ_SPLITS[:-1]:
        acc += w
        idx.append(acc)
    return jnp.split(p, idx, axis=-1)


def _rel_bias(qpos, kpos, rel_table):
    n = jnp.maximum(qpos[:, None] - kpos[None, :], 0)
    max_exact = REL_BUCKETS // 2
    nf = jnp.maximum(n, 1).astype(jnp.float32)
    large = max_exact + (jnp.log(nf / max_exact) / math.log(REL_MAX_DIST / max_exact)
                         * (REL_BUCKETS - max_exact)).astype(jnp.int32)
    bucket = jnp.where(n < max_exact, n, jnp.minimum(large, REL_BUCKETS - 1))
    return jnp.transpose(rel_table[bucket], (2, 0, 1)).astype(jnp.float32)


def _diff_logits(q, k, qpos, kpos, rel_table):
    scale = DIFF_DK ** -0.5
    bias = _rel_bias(qpos, kpos, rel_table)[None]
    causal = (kpos[None, :] <= qpos[:, None])[None, None]
    l1 = jnp.einsum('bqhd,bkhd->bhqk', q[..., :DIFF_DK], k[..., :DIFF_DK]).astype(jnp.float32) * scale + bias
    l2 = jnp.einsum('bqhd,bkhd->bhqk', q[..., DIFF_DK:], k[..., DIFF_DK:]).astype(jnp.float32) * scale + bias
    return jnp.where(causal, l1, MASK_VALUE), jnp.where(causal, l2, MASK_VALUE)


def _diff_weights(l1, l2, lam):
    return jax.nn.softmax(l1, axis=-1) - lam * jax.nn.softmax(l2, axis=-1)


def _diff_attn_prompt(q, k, v, rel_table, lam):
    B, S, H, _ = q.shape
    nb = S // Q_BLOCK
    qb = q.reshape(B, nb, Q_BLOCK, H, HEAD_DIM).transpose(1, 0, 2, 3, 4)
    kpos = jnp.arange(S)

    def one_block(args):
        qi, i0 = args
        qpos = i0 + jnp.arange(Q_BLOCK)
        l1, l2 = _diff_logits(qi, k, qpos, kpos, rel_table)
        w = _diff_weights(l1, l2, lam).astype(v.dtype)
        return jnp.einsum('bhqk,bkhd->bqhd', w, v)

    out = lax.map(one_block, (qb, jnp.arange(nb) * Q_BLOCK))
    return out.transpose(1, 0, 2, 3, 4).reshape(B, S, H, HEAD_DIM)


def _diff_attn_sample(q, k, v, k_past, v_past, rel_table, lam):
    T = q.shape[1]
    P = k_past.shape[1]
    qpos = P + jnp.arange(T)
    l1p, l2p = _diff_logits(q, k_past, qpos, jnp.arange(P), rel_table)
    l1n, l2n = _diff_logits(q, k, qpos, qpos, rel_table)
    w = _diff_weights(jnp.concatenate([l1p, l1n], -1), jnp.concatenate([l2p, l2n], -1), lam).astype(v.dtype)
    return (jnp.einsum('bhqk,bkhd->bqhd', w[..., :P], v_past)
            + jnp.einsum('bhqk,bkhd->bqhd', w[..., P:], v))


def _gla_chunked(q, k, v, log_a, s0):
    B, T, H, _ = q.shape
    DV = v.shape[-1]
    pad = (-T) % GLA_CHUNK

    def prep(a):
        a = jnp.pad(a.astype(jnp.float32), ((0, 0), (0, pad), (0, 0), (0, 0)))
        n = a.shape[1] // GLA_CHUNK
        return a.reshape(B, n, GLA_CHUNK, H, a.shape[-1]).transpose(1, 0, 3, 2, 4)

    qc, kc, vc, gc = prep(q), prep(k), prep(v), prep(log_a)
    tri = jnp.tril(jnp.ones((GLA_CHUNK, GLA_CHUNK), bool))[:, :, None]

    def step(S, inp):
        qi, ki, vi, gi = inp
        b = jnp.cumsum(gi, axis=2)
        b_last = b[:, :, -1:, :]
        o_inter = jnp.einsum('bhcd,bhde->bhce', qi * jnp.exp(b), S)
        diff = b[:, :, :, None, :] - b[:, :, None, :, :]
        dec = jnp.where(tri, jnp.exp(jnp.minimum(diff, 0.0)), 0.0)
        att = jnp.einsum('bhid,bhjd,bhijd->bhij', qi, ki, dec)
        o = o_inter + jnp.einsum('bhij,bhje->bhie', att, vi)
        S_new = (jnp.exp(b_last)[:, :, 0, :, None] * S
                 + jnp.einsum('bhjd,bhje->bhde', ki * jnp.exp(b_last - b), vi))
        return S_new, o

    s_fin, o = lax.scan(step, s0.astype(jnp.float32), (qc, kc, vc, gc))
    o = o.transpose(1, 0, 3, 2, 4).reshape(B, -1, H, DV)[:, :T]
    return o, s_fin


def _short_conv(gate_b, gate_c, h, buf, w):
    z = gate_c * h
    T = z.shape[1]
    zp = jnp.concatenate([buf.astype(z.dtype), z], axis=1)
    y = zp[:, 0:T] * w[0]
    for j in range(1, CONV_WIDTH):
        y = y + zp[:, j:j + T] * w[j]
    return gate_b * y, zp[:, -(CONV_WIDTH - 1):]


def _chunk_mlp(u, v, ws, bs):
    B, T, _ = v.shape
    pad = (-T) % CHUNK_D
    vp = jnp.pad(v, ((0, 0), (0, pad), (0, 0)))
    n = vp.shape[1] // CHUNK_D
    vc = vp.reshape(B, n, CHUNK_D, N_HEADS_D, HEAD_DIM)
    w_causal = ws * jnp.tril(jnp.ones((CHUNK_D, CHUNK_D), ws.dtype))
    z = jnp.einsum('gij,bcjgd->bcigd', w_causal, vc) + bs.T[None, None, :, :, None]
    z = z.reshape(B, n * CHUNK_D, GROUP_WIDTH)[:, :T]
    return u * z


def _grouped_experts(x, expert_idx, gates, l, w_gate, w_up, w_down):
    T, D = x.shape
    TK = T * TOP_K
    blk = int(min(MOE_BLOCK, max(8, TK // N_EXPERTS)))
    n_blocks = -(-TK // blk) + N_EXPERTS
    flat_e = expert_idx.reshape(TK).astype(jnp.int32)
    order = jnp.argsort(flat_e)
    sorted_e = flat_e[order]
    counts = jnp.bincount(flat_e, length=N_EXPERTS)
    padded = (counts + blk - 1) // blk * blk
    pad_end = jnp.cumsum(padded)
    pad_start = pad_end - padded
    start = jnp.cumsum(counts) - counts
    dest = pad_start[sorted_e] + jnp.arange(TK) - start[sorted_e]
    tok_sorted = (order // TOP_K).astype(jnp.int32)
    row_tok = jnp.full((n_blocks * blk,), T, jnp.int32).at[dest].set(tok_sorted)
    x_pad = jnp.concatenate([x, jnp.zeros((1, D), x.dtype)], axis=0)
    xb = x_pad[row_tok].reshape(n_blocks, blk, D)
    block_e = jnp.minimum(jnp.searchsorted(pad_end, jnp.arange(n_blocks) * blk, side='right'), N_EXPERTS - 1)

    def run(args):
        xi, e = args
        hdn = jax.nn.silu(xi @ w_gate[l, e]) * (xi @ w_up[l, e])
        return hdn @ w_down[l, e]

    yb = lax.map(run, (xb, block_e)).reshape(n_blocks * blk, D)
    y_sorted = yb[dest] * gates.reshape(TK)[order][:, None]
    return jax.ops.segment_sum(y_sorted, tok_sorted, num_segments=T)


def _hier_moe(x, l, W):
    T = x.shape[0]
    rows = jnp.arange(T)
    g_logits = (x @ W['router_group'][l]).astype(jnp.float32)
    p_group = jax.nn.softmax(g_logits, axis=-1)
    g_star = jnp.argmax(g_logits, axis=-1)
    p_top = p_group[rows, g_star][:, None]
    e_logits = (x @ W['router_expert'][l]).astype(jnp.float32).reshape(T, N_GROUPS, EXPERTS_PER_GROUP)
    within = e_logits[rows, g_star]
    top_v, top_i = lax.top_k(within, TOP_K)
    gates = (jax.nn.softmax(top_v, axis=-1) * p_top).astype(x.dtype)
    expert_idx = g_star[:, None] * EXPERTS_PER_GROUP + top_i
    return _grouped_experts(x, expert_idx, gates, l, W['w_gate'], W['w_up'], W['w_down'])


def _hybrid_layer(x, l, lam, lam_init, W, rel_table, kv_past, gla_s0, conv_buf):
    B, T, _ = x.shape
    (a_q, a_k, a_v, b_q, b_k, b_v, b_g, b_r,
     c_b, c_c, c_h, d_u, d_v) = _split_proj(x @ W['w_in'][l])
    q_a = a_q.reshape(B, T, N_HEADS_A, HEAD_DIM)
    k_a = a_k.reshape(B, T, N_HEADS_A, HEAD_DIM)
    v_a = a_v.reshape(B, T, N_HEADS_A, HEAD_DIM)
    if kv_past is None:
        o_a = _diff_attn_prompt(q_a, k_a, v_a, rel_table, lam)
    else:
        o_a = _diff_attn_sample(q_a, k_a, v_a, kv_past[0], kv_past[1], rel_table, lam)
    o_a = _rms_norm(o_a, W['diff_norm_g'][l]) * (1.0 - lam_init)
    q_b = b_q.reshape(B, T, N_HEADS_B, GLA_DK) * (GLA_DK ** -0.5)
    k_b = b_k.reshape(B, T, N_HEADS_B, GLA_DK)
    v_b = b_v.reshape(B, T, N_HEADS_B, GLA_DV)
    log_a = jax.nn.log_sigmoid((b_g @ W['gla_w_up'][l] + W['gla_b'][l]).astype(jnp.float32)) / GLA_TAU
    o_b, s_fin = _gla_chunked(q_b, k_b, v_b, log_a.reshape(B, T, N_HEADS_B, GLA_DK), gla_s0)
    o_b = _rms_norm(o_b.astype(x.dtype), W['gla_norm_g'][l]) * jax.nn.silu(b_r.reshape(B, T, N_HEADS_B, GLA_DV))
    o_c, conv_state = _short_conv(c_b, c_c, c_h, conv_buf, W['conv_w'][l])
    v_d = _layer_norm(d_v, W['cm_ln_g'][l], W['cm_ln_b'][l])
    o_d = _chunk_mlp(d_u, v_d, W['cm_ws'][l], W['cm_bs'][l])
    mix = jnp.concatenate([o_a.reshape(B, T, -1), o_b.reshape(B, T, -1), o_c, o_d], axis=-1) @ W['w_out'][l]
    x = _layer_norm(DEEPNORM_ALPHA * x + mix, W['ln1_g'][l], W['ln1_b'][l])
    ffn = _hier_moe(x.reshape(B * T, -1), l, W).reshape(B, T, -1)
    x = _layer_norm(DEEPNORM_ALPHA * x + ffn, W['ln2_g'][l], W['ln2_b'][l])
    return x, (k_a, v_a, s_fin, conv_state, v_d)


def setup_inputs(seed: int = 0) -> dict:
    key = jax.random.key(seed)
    ks = jax.random.split(key, 32)
    f32 = jnp.float32
    n_pages = PAST_LEN // PAGE_SIZE
    n_phys = (5 * DEC_BATCH * n_pages) // 4

    def nrm(k, shape, s):
        return jax.random.normal(k, shape, f32) * s

    def gain(k, shape):
        return 1.0 + nrm(k, shape, 0.02)

    page_table = jax.random.permutation(ks[4], n_phys)[:DEC_BATCH * n_pages]
    page_table = page_table.reshape(DEC_BATCH, n_pages).astype(jnp.int32)
    return {
        'x_prompt': nrm(ks[0], (BATCH, SEQ, D_MODEL), 1.0),
        'x_sample': nrm(ks[1], (DEC_BATCH, DEC_SEQ, D_MODEL), 1.0),
        'cache_k': nrm(ks[2], (DEPTH, n_phys, PAGE_SIZE, N_HEADS_A, HEAD_DIM), 1.0),
        'cache_v': nrm(ks[3], (DEPTH, n_phys, PAGE_SIZE, N_HEADS_A, HEAD_DIM), 1.0),
        'page_table': page_table,
        'state_gla': nrm(ks[5], (DEPTH, DEC_BATCH, N_HEADS_B, GLA_DK, GLA_DV), 1.0),
        'state_conv': nrm(ks[6], (DEPTH, DEC_BATCH, CONV_WIDTH - 1, GROUP_WIDTH), 1.0),
        'rel_table': nrm(ks[7], (REL_BUCKETS, N_HEADS_A), 0.5),
        'w_in': nrm(ks[8], (DEPTH, D_MODEL, PROJ_WIDTH), D_MODEL ** -0.5),
        'w_out': nrm(ks[9], (DEPTH, MIX_WIDTH, D_MODEL), MIX_WIDTH ** -0.5 * DEEPNORM_BETA),
        'lam_q1': nrm(ks[10], (DEPTH, DIFF_DK), 0.1),
        'lam_k1': nrm(ks[11], (DEPTH, DIFF_DK), 0.1),
        'lam_q2': nrm(ks[12], (DEPTH, DIFF_DK), 0.1),
        'lam_k2': nrm(ks[13], (DEPTH, DIFF_DK), 0.1),
        'diff_norm_g': gain(ks[14], (DEPTH, HEAD_DIM)),
        'gla_w_up': nrm(ks[15], (DEPTH, GLA_RANK, N_HEADS_B * GLA_DK), GLA_RANK ** -0.5),
        'gla_b': nrm(ks[16], (DEPTH, N_HEADS_B * GLA_DK), 0.1),
        'gla_norm_g': gain(ks[17], (DEPTH, GLA_DV)),
        'conv_w': nrm(ks[18], (DEPTH, CONV_WIDTH, GROUP_WIDTH), CONV_WIDTH ** -0.5),
        'cm_ln_g': gain(ks[19], (DEPTH, GROUP_WIDTH)),
        'cm_ln_b': nrm(ks[20], (DEPTH, GROUP_WIDTH), 0.02),
        'cm_ws': nrm(ks[21], (DEPTH, N_HEADS_D, CHUNK_D, CHUNK_D), CHUNK_D ** -0.5),
        'cm_bs': gain(ks[22], (DEPTH, N_HEADS_D, CHUNK_D)),
        'ln1_g': gain(ks[23], (DEPTH, D_MODEL)),
        'ln1_b': nrm(ks[24], (DEPTH, D_MODEL), 0.02),
        'ln2_g': gain(ks[25], (DEPTH, D_MODEL)),
        'ln2_b': nrm(ks[26], (DEPTH, D_MODEL), 0.02),
        'router_group': nrm(ks[27], (DEPTH, D_MODEL, N_GROUPS), D_MODEL ** -0.5),
        'router_expert': nrm(ks[28], (DEPTH, D_MODEL, N_EXPERTS), D_MODEL ** -0.5),
        'w_gate': nrm(ks[29], (DEPTH, N_EXPERTS, D_MODEL, EXPERT_HIDDEN), D_MODEL ** -0.5),
        'w_up': nrm(ks[30], (DEPTH, N_EXPERTS, D_MODEL, EXPERT_HIDDEN), D_MODEL ** -0.5),
        'w_down': nrm(ks[31], (DEPTH, N_EXPERTS, EXPERT_HIDDEN, D_MODEL), EXPERT_HIDDEN ** -0.5 * DEEPNORM_BETA),
    }


def reference(x_prompt, x_sample, cache_k, cache_v, page_table, state_gla, state_conv, rel_table,
              w_in, w_out, lam_q1, lam_k1, lam_q2, lam_k2, diff_norm_g, gla_w_up, gla_b, gla_norm_g,
              conv_w, cm_ln_g, cm_ln_b, cm_ws, cm_bs, ln1_g, ln1_b, ln2_g, ln2_b,
              router_group, router_expert, w_gate, w_up, w_down):
    f32 = jnp.float32
    W = dict(w_in=w_in, w_out=w_out, diff_norm_g=diff_norm_g, gla_w_up=gla_w_up, gla_b=gla_b,
             gla_norm_g=gla_norm_g, conv_w=conv_w, cm_ln_g=cm_ln_g, cm_ln_b=cm_ln_b, cm_ws=cm_ws,
             cm_bs=cm_bs, ln1_g=ln1_g, ln1_b=ln1_b, ln2_g=ln2_g, ln2_b=ln2_b,
             router_group=router_group, router_expert=router_expert,
             w_gate=w_gate, w_up=w_up, w_down=w_down)
    bp = x_prompt.shape[0]
    db = page_table.shape[0]
    hp, hs = x_prompt, x_sample
    kp_l, vp_l, ks_l, vs_l, gp_l, gs_l, cp_l, cs_l, ds_l = [], [], [], [], [], [], [], [], []
    for l in range(DEPTH):
        lam_init = 0.8 - 0.6 * math.exp(-0.3 * l)
        lam = (jnp.exp(jnp.sum(lam_q1[l].astype(f32) * lam_k1[l].astype(f32)))
               - jnp.exp(jnp.sum(lam_q2[l].astype(f32) * lam_k2[l].astype(f32))) + lam_init)
        hp, st_p = _hybrid_layer(hp, l, lam, lam_init, W, rel_table, None,
                                 jnp.zeros((bp, N_HEADS_B, GLA_DK, GLA_DV), f32),
                                 jnp.zeros((bp, CONV_WIDTH - 1, GROUP_WIDTH), hp.dtype))
        kv_past = (cache_k[l, page_table].reshape(db, -1, N_HEADS_A, HEAD_DIM),
                   cache_v[l, page_table].reshape(db, -1, N_HEADS_A, HEAD_DIM))
        hs, st_s = _hybrid_layer(hs, l, lam, lam_init, W, rel_table, kv_past, state_gla[l], state_conv[l])
        kp_l.append(st_p[0]); vp_l.append(st_p[1]); gp_l.append(st_p[2]); cp_l.append(st_p[3])
        ks_l.append(st_s[0]); vs_l.append(st_s[1]); gs_l.append(st_s[2]); cs_l.append(st_s[3]); ds_l.append(st_s[4])
    k_prompt = jnp.stack(kp_l)
    v_prompt = jnp.stack(vp_l)
    k_sample = jnp.stack(ks_l)
    v_sample = jnp.stack(vs_l)
    gla_prompt = jnp.stack(gp_l)
    gla_sample = jnp.stack(gs_l)
    conv_prompt = jnp.stack(cp_l)
    conv_sample = jnp.stack(cs_l)
    chunk_v_sample = jnp.stack(ds_l)
    return (hp, hs, k_prompt, v_prompt, k_sample, v_sample, gla_prompt, gla_sample,
            conv_prompt, conv_sample, chunk_v_sample)
```

```python
import functools
import math

import jax
import jax.numpy as jnp
from jax import lax
from jax.experimental import pallas as pl
from jax.experimental.pallas import tpu as pltpu

F32 = jnp.float32
BF16 = jnp.bfloat16

HEAD_DIM = 128
HALF_DIM = HEAD_DIM // 2
GLA_RANK = 16
GLA_TAU = 16.0
GLA_CHUNK = 64
CONV_WIDTH = 3
CHUNK_D = 128
REL_BUCKETS = 32
REL_MAX_DIST = 128
N_GROUPS = 4
EXPERTS_PER_GROUP = 8
N_EXPERTS = N_GROUPS * EXPERTS_PER_GROUP
TOP_K = 2
NORM_EPS = 1e-5
MASK_VALUE = -1e30
LOG2E = 1.4426950408889634

LANES = 128
SAMPLE_ROWS = 64
ATTN_TILE = 256
MOE_ITEM_ROWS = 1024
MOE_SUB_ROWS = 256
MOE_K_STEPS = 8
VMEM_LIMIT = 52 * 1024 * 1024


def _pick(n, target, mult):
    if n <= target:
        return n
    best = None
    for d in range(mult, target + 1, mult):
        if n % d == 0:
            best = d
    assert best is not None, (n, target, mult)
    return best


def _params(*sem):
    return pltpu.CompilerParams(dimension_semantics=sem, vmem_limit_bytes=VMEM_LIMIT)


def _dot(a, b):
    return jnp.dot(a, b, preferred_element_type=F32)


def _dot_nt(a, b):
    return lax.dot_general(a, b, (((1,), (1,)), ((), ())), preferred_element_type=F32)


def _dot_tn(a, b):
    return lax.dot_general(a, b, (((0,), (0,)), ((), ())), preferred_element_type=F32)


def _split_bf16(x):
    hi = x.astype(BF16)
    lo = (x - hi.astype(F32)).astype(BF16)
    return hi, lo


def _sigmoid(x):
    return 1.0 / (1.0 + jnp.exp(-x))


def _mm_kernel(*refs, n_in):
    o_ref = refs[2 * n_in]
    acc = None
    for x_ref, w_ref in zip(refs[:n_in], refs[n_in:2 * n_in]):
        d = _dot(x_ref[...], w_ref[...])
        acc = d if acc is None else acc + d
    o_ref[...] = acc.astype(o_ref.dtype)


def _matmul(xs, w, n_out, tm, tn, name):
    n_in = len(xs)
    m, kg = xs[0].shape
    grid = (m // tm, n_out // tn)
    in_specs = [pl.BlockSpec((tm, kg), lambda i, j: (i, 0)) for _ in xs]
    in_specs += [pl.BlockSpec((kg, tn), functools.partial(lambda i, j, g: (g, j), g=g)) for g in range(n_in)]
    return pl.pallas_call(
        functools.partial(_mm_kernel, n_in=n_in),
        grid=grid,
        in_specs=in_specs,
        out_specs=pl.BlockSpec((tm, tn), lambda i, j: (i, j)),
        out_shape=jax.ShapeDtypeStruct((m, n_out), F32),
        compiler_params=_params("parallel", "arbitrary"),
        name=name,
    )(*xs, *([w] * n_in))


def _rel_bias(dist, rel_table):
    n = jnp.maximum(dist, 0)
    max_exact = REL_BUCKETS // 2
    nf = jnp.maximum(n, 1).astype(F32)
    large = max_exact + (jnp.log(nf / max_exact) / math.log(REL_MAX_DIST / max_exact)
                         * (REL_BUCKETS - max_exact)).astype(jnp.int32)
    bucket = jnp.where(n < max_exact, n, jnp.minimum(large, REL_BUCKETS - 1))
    return rel_table[bucket].astype(F32)


def _prompt_bias_tiles(rel_table, t):
    r = jnp.arange(t)[:, None]
    c = jnp.arange(t)[None, :]
    tiles = []
    for delta in range(3):
        dist = delta * t + r - c
        b = _rel_bias(dist, rel_table) * LOG2E
        tiles.append(jnp.where((dist >= 0)[..., None], b, MASK_VALUE))
    return jnp.transpose(jnp.stack(tiles), (3, 0, 1, 2))


def _lam_value(lq1, lk1, lq2, lk2, lam_init):
    return (jnp.exp(jnp.sum(lq1[...] * lk1[...], axis=1, keepdims=True))
            - jnp.exp(jnp.sum(lq2[...] * lk2[...], axis=1, keepdims=True)) + lam_init)


def _rms_head(o, gain, post):
    return o * lax.rsqrt(jnp.mean(o * o, axis=1, keepdims=True) + NORM_EPS) * gain * post


def _attn_prompt_kernel(lq1, lk1, lq2, lk2, q_ref, k_ref, v_ref, bias_ref, gn_ref, o_ref, *, t, lam_init):
    qi = pl.program_id(2)
    lam = _lam_value(lq1, lk1, lq2, lk2, lam_init)
    q = q_ref[...] * (HALF_DIM ** -0.5 * LOG2E)
    lane = lax.broadcasted_iota(jnp.int32, q.shape, 1)
    qq = jnp.concatenate([jnp.where(lane < HALF_DIM, q, 0.0), jnp.where(lane >= HALF_DIM, q, 0.0)],
                         axis=0).astype(BF16)

    def body(j, carry):
        m, s, acc = carry
        start = pl.multiple_of(j * t, t)
        kt = k_ref[pl.ds(start, t), :].astype(BF16)
        vt = v_ref[pl.ds(start, t), :].astype(BF16)
        bt = bias_ref[0, jnp.minimum(qi - j, 2)]
        l = _dot_nt(qq, kt) + jnp.concatenate([bt, bt], axis=0)
        m_new = jnp.maximum(m, jnp.max(l, axis=1, keepdims=True))
        a = jnp.exp2(m - m_new)
        p = jnp.exp2(l - m_new)
        s = a * s + jnp.sum(p, axis=1, keepdims=True)
        acc = a * acc + _dot(p.astype(BF16), vt)
        return m_new, s, acc

    init = (jnp.full((2 * t, 1), MASK_VALUE, F32), jnp.zeros((2 * t, 1), F32), jnp.zeros((2 * t, HEAD_DIM), F32))
    _, s, acc = lax.fori_loop(0, qi + 1, body, init)
    o = acc / s
    o = o[:t] - lam * o[t:]
    o_ref[...] = _rms_head(o, gn_ref[...], 1.0 - lam_init).astype(o_ref.dtype)


def _attn_prompt(proj, lams, bias_tiles, gn, n_seq, seq, n_heads, lam_init):
    t = ATTN_TILE
    assert seq % t == 0 and t >= REL_MAX_DIST
    nq = seq // t
    lam_spec = pl.BlockSpec((1, HALF_DIM), lambda b, h, i: (0, 0))
    return pl.pallas_call(
        functools.partial(_attn_prompt_kernel, t=t, lam_init=lam_init),
        grid=(n_seq, n_heads, nq),
        in_specs=[lam_spec] * 4 + [
            pl.BlockSpec((t, HEAD_DIM), lambda b, h, i: (b * nq + i, h)),
            pl.BlockSpec((seq, HEAD_DIM), lambda b, h, i: (b, n_heads + h)),
            pl.BlockSpec((seq, HEAD_DIM), lambda b, h, i: (b, 2 * n_heads + h)),
            pl.BlockSpec((1, 3, t, t), lambda b, h, i: (h, 0, 0, 0)),
            pl.BlockSpec((1, HEAD_DIM), lambda b, h, i: (0, 0)),
        ],
        out_specs=pl.BlockSpec((t, HEAD_DIM), lambda b, h, i: (b * nq + i, h)),
        out_shape=jax.ShapeDtypeStruct((n_seq * seq, n_heads * HEAD_DIM), BF16),
        compiler_params=_params("parallel", "parallel", "arbitrary"),
        name="attn_prompt",
    )(*lams, proj, proj, proj, bias_tiles, gn)


def _attn_sample_kernel(pt_ref, lq1, lk1, lq2, lk2, q_ref, kc_ref, vc_ref, kn_ref, vn_ref, bl_ref, bn_ref, cr_ref,
                        gn_ref, o_ref, m_ref, s_ref, acc_ref, *, n_heads, t_new, lam_init):
    p = pl.program_id(1)
    n_pages = pl.num_programs(1)

    @pl.when(p == 0)
    def _():
        m_ref[...] = jnp.full(m_ref.shape, MASK_VALUE, F32)
        s_ref[...] = jnp.zeros(s_ref.shape, F32)
        acc_ref[...] = jnp.zeros(acc_ref.shape, F32)

    q = q_ref[0]

    def step(k, v, bias):
        l = _dot_nt(q, k.astype(BF16)) + bias
        m = m_ref[...]
        m_new = jnp.maximum(m, jnp.max(l, axis=1, keepdims=True))
        a = jnp.exp2(m - m_new)
        pr = jnp.exp2(l - m_new)
        s_ref[...] = a * s_ref[...] + jnp.sum(pr, axis=1, keepdims=True)
        acc_ref[...] = a * acc_ref[...] + _dot(pr.astype(BF16), v.astype(BF16))
        m_ref[...] = m_new

    last = p == n_pages - 1
    step(kc_ref[0, 0], vc_ref[0, 0], jnp.where(last, bl_ref[...], cr_ref[...]))

    @pl.when(last)
    def _():
        step(kn_ref[0], vn_ref[0], bn_ref[...])
        lam = _lam_value(lq1, lk1, lq2, lk2, lam_init)
        half = n_heads * t_new
        for h in range(n_heads):
            rows = slice(h * t_new, (h + 1) * t_new)
            rows2 = slice(half + h * t_new, half + (h + 1) * t_new)
            cols = slice(h * HEAD_DIM, (h + 1) * HEAD_DIM)
            o = acc_ref[rows, cols] / s_ref[rows, :] - lam * (acc_ref[rows2, cols] / s_ref[rows2, :])
            o_ref[0, :, cols] = _rms_head(o, gn_ref[...], 1.0 - lam_init)


def _attn_sample(layer, q_bd, cache_k, cache_v, k_new, v_new, page_table, bias_last, bias_new, c_row, lams, gn,
                 n_heads, t_new, lam_init):
    n_seq, n_pages = page_table.shape
    page = cache_k.shape[2]
    gw = n_heads * HEAD_DIM
    rows = q_bd.shape[1]
    lam_spec = pl.BlockSpec((1, HALF_DIM), lambda b, p, pt: (0, 0))
    cache_spec = pl.BlockSpec((1, 1, page, gw), lambda b, p, pt: (layer, pt[b, p], 0, 0))
    new_spec = pl.BlockSpec((1, page, gw), lambda b, p, pt: (b, 0, 0))
    full2 = lambda shape: pl.BlockSpec(shape, lambda b, p, pt: (0, 0))
    grid_spec = pltpu.PrefetchScalarGridSpec(
        num_scalar_prefetch=1,
        grid=(n_seq, n_pages),
        in_specs=[lam_spec] * 4 + [
            pl.BlockSpec((1, rows, gw), lambda b, p, pt: (b, 0, 0)),
            cache_spec, cache_spec, new_spec, new_spec,
            full2((rows, page)), full2((rows, page)), full2((rows, 1)), full2((1, HEAD_DIM)),
        ],
        out_specs=pl.BlockSpec((1, t_new, gw), lambda b, p, pt: (b, 0, 0)),
        scratch_shapes=[pltpu.VMEM((rows, 1), F32), pltpu.VMEM((rows, 1), F32), pltpu.VMEM((rows, gw), F32)],
    )
    return pl.pallas_call(
        functools.partial(_attn_sample_kernel, n_heads=n_heads, t_new=t_new, lam_init=lam_init),
        grid_spec=grid_spec,
        out_shape=jax.ShapeDtypeStruct((n_seq, t_new, gw), F32),
        compiler_params=_params("parallel", "arbitrary"),
        name="attn_sample",
    )(page_table, *lams, q_bd, cache_k, cache_v, k_new, v_new, bias_last, bias_new, c_row, gn)


def _gla_kernel(q_ref, k_ref, v_ref, r_ref, gp_ref, wh_ref, wl_ref, gb_ref, gn_ref, s0_ref, o_ref, sfin_ref, st_ref,
                *, chunk, n_chunks, n_heads, valid):
    t = pl.program_id(1)

    @pl.when(t == 0)
    def _():
        st_ref[...] = s0_ref[0]

    row = lax.broadcasted_iota(jnp.int32, (chunk, chunk), 0)
    col = lax.broadcasted_iota(jnp.int32, (chunk, chunk), 1)
    tril = row >= col
    tri_bf = jnp.where(tril, 1.0, 0.0).astype(BF16)
    lane = lax.broadcasted_iota(jnp.int32, (chunk, LANES), 1)
    lane_sq = lax.broadcasted_iota(jnp.int32, (HEAD_DIM, LANES), 1)
    mid = chunk // 2 - 1

    for c in range(n_chunks):
        rs = slice(c * chunk, (c + 1) * chunk)
        gh, gl = _split_bf16(gp_ref[rs, :])
        z = _dot(gh, wh_ref[...]) + _dot(gl, wh_ref[...]) + _dot(gh, wl_ref[...]) + gb_ref[...]
        g = -(jnp.maximum(-z, 0.0) + jnp.log(1.0 + jnp.exp(-jnp.abs(z)))) * (1.0 / GLA_TAU)
        k = k_ref[rs, :]
        if valid < chunk:
            ok = lax.broadcasted_iota(jnp.int32, g.shape, 0) < valid
            g = jnp.where(ok, g, 0.0)
            k = jnp.where(ok, k, 0.0)
        gh, gl = _split_bf16(g)
        b = _dot(tri_bf, gh) + _dot(tri_bf, gl)
        b_last = b[chunk - 1:chunk, :]
        b_mid = b[mid:mid + 1, :]
        q = q_ref[rs, :] * (HALF_DIM ** -0.5)
        q_state = q * jnp.exp(b)
        q_mid = q * jnp.exp(b - b_mid)
        k_mid = (k * jnp.exp(b_mid - b)).astype(BF16)
        k_end = (k * jnp.exp(b_last - b)).astype(BF16)
        decay = jnp.exp(b_last)

        for hp in range(n_heads // 2):
            ls = slice(hp * LANES, (hp + 1) * LANES)
            st = st_ref[hp]
            st_bf = st.astype(BF16)
            upd = []
            for par in range(2):
                h = 2 * hp + par
                mine = (lane < HALF_DIM) if par == 0 else (lane >= HALF_DIM)
                hs = slice(h * HEAD_DIM, (h + 1) * HEAD_DIM)
                v_h = v_ref[rs, hs].astype(BF16)
                att = _dot_nt(jnp.where(mine, q_mid[:, ls], 0.0).astype(BF16), k_mid[:, ls])
                att = jnp.where(tril, att, 0.0).astype(BF16)
                o = _dot(att, v_h) + _dot_nt(jnp.where(mine, q_state[:, ls], 0.0).astype(BF16), st_bf)
                r_h = r_ref[rs, hs]
                o_ref[rs, hs] = _rms_head(o, gn_ref[...], r_h * _sigmoid(r_h)).astype(o_ref.dtype)
                upd.append(_dot_tn(v_h, k_end[:, ls]))
            st_ref[hp] = st * decay[:, ls] + jnp.where(lane_sq < HALF_DIM, upd[0], upd[1])

    @pl.when(t == pl.num_programs(1) - 1)
    def _():
        sfin_ref[0] = st_ref[...]


def _gla(proj, gproj, wup_hi, wup_lo, gla_b, gn, s0, row0, n_seq, seq, n_heads, valid):
    gw = n_heads * HEAD_DIM
    hk = n_heads * HALF_DIM
    chunk = min(GLA_CHUNK, seq)
    tt = _pick(seq, 256, chunk)
    nt = seq // tt
    r0 = row0 // tt
    assert row0 % tt == 0
    qcol = 3 * gw // hk
    row_spec = lambda width, colblk: pl.BlockSpec((tt, width), lambda b, t: (r0 + b * nt + t, colblk))
    const = lambda shape: pl.BlockSpec(shape, lambda b, t: tuple(0 for _ in shape))
    st_spec = pl.BlockSpec((1, n_heads // 2, HEAD_DIM, LANES), lambda b, t: (b, 0, 0, 0))
    return pl.pallas_call(
        functools.partial(_gla_kernel, chunk=chunk, n_chunks=tt // chunk, n_heads=n_heads, valid=valid),
        grid=(n_seq, nt),
        in_specs=[row_spec(hk, qcol), row_spec(hk, qcol + 1), row_spec(gw, 4), row_spec(gw, 5),
                  pl.BlockSpec((tt, LANES), lambda b, t: (r0 + b * nt + t, 0)),
                  const((LANES, hk)), const((LANES, hk)), const((1, hk)), const((1, HEAD_DIM)), st_spec],
        out_specs=[pl.BlockSpec((tt, gw), lambda b, t: (b * nt + t, 0)), st_spec],
        out_shape=[jax.ShapeDtypeStruct((n_seq * seq, gw), BF16),
                   jax.ShapeDtypeStruct((n_seq, n_heads // 2, HEAD_DIM, LANES), F32)],
        scratch_shapes=[pltpu.VMEM((n_heads // 2, HEAD_DIM, LANES), F32)],
        compiler_params=_params("parallel", "arbitrary"),
        name="gla",
    )(proj, proj, proj, proj, gproj, wup_hi, wup_lo, gla_b, gn, s0)


def _conv_mlp_kernel(cb_ref, cc_ref, ch_ref, du_ref, dv_ref, cw_ref, buf_ref, lg_ref, lb_ref, ws_ref, bs_ref,
                     oc_ref, od_ref, cs_ref, vd_ref, carry_ref, *, tt, chunk, n_heads, valid):
    t = pl.program_id(1)

    @pl.when(t == 0)
    def _():
        carry_ref[...] = buf_ref[0]

    z = cc_ref[...] * ch_ref[...]
    row = lax.broadcasted_iota(jnp.int32, z.shape, 0)
    c0 = carry_ref[0:1, :]
    c1 = carry_ref[1:2, :]
    z1 = jnp.where(row == 0, c1, pltpu.roll(z, 1, 0))
    z2 = jnp.where(row == 0, c0, jnp.where(row == 1, c1, pltpu.roll(z, 2, 0)))
    w = cw_ref[...]
    oc_ref[...] = (cb_ref[...] * (z2 * w[0:1, :] + z1 * w[1:2, :] + z * w[2:3, :])).astype(oc_ref.dtype)
    if valid >= 2:
        carry_ref[...] = z[valid - 2:valid, :]
    else:
        carry_ref[...] = jnp.concatenate([c1, z[0:1, :]], axis=0)

    @pl.when(t == pl.num_programs(1) - 1)
    def _():
        cs_ref[0] = carry_ref[...]

    x = dv_ref[...]
    mu = jnp.mean(x, axis=1, keepdims=True)
    xc = x - mu
    var = jnp.mean(xc * xc, axis=1, keepdims=True)
    vd = xc * lax.rsqrt(var + NORM_EPS) * lg_ref[...] + lb_ref[...]
    vd_ref[...] = vd
    vd_bf = vd.astype(BF16)
    r2 = lax.broadcasted_iota(jnp.int32, (chunk, chunk), 0)
    c2 = lax.broadcasted_iota(jnp.int32, (chunk, chunk), 1)
    for g in range(n_heads):
        wc = jnp.where(r2 >= c2, ws_ref[g, :chunk, :chunk], 0.0).astype(BF16)
        bias = bs_ref[:chunk, g:g + 1]
        cols = slice(g * HEAD_DIM, (g + 1) * HEAD_DIM)
        for c in range(tt // chunk):
            rs = slice(c * chunk, (c + 1) * chunk)
            zc = _dot(wc, vd_bf[rs, cols]) + bias
            od_ref[rs, cols] = (du_ref[rs, cols] * zc).astype(od_ref.dtype)


def _conv_mlp(proj, conv_w, buf, ln_g, ln_b, ws, bs_t, row0, n_seq, seq, n_heads, valid):
    gw = n_heads * HEAD_DIM
    chunk = min(CHUNK_D, seq)
    tt = _pick(seq, 256, chunk)
    nt = seq // tt
    r0 = row0 // tt
    assert row0 % tt == 0 and (valid == seq or nt == 1)
    row_spec = lambda colblk: pl.BlockSpec((tt, gw), lambda b, t: (r0 + b * nt + t, colblk))
    const = lambda shape: pl.BlockSpec(shape, lambda b, t: tuple(0 for _ in shape))
    out_rows = pl.BlockSpec((tt, gw), lambda b, t: (b * nt + t, 0))
    state_spec = pl.BlockSpec((1, CONV_WIDTH - 1, gw), lambda b, t: (b, 0, 0))
    return pl.pallas_call(
        functools.partial(_conv_mlp_kernel, tt=tt, chunk=chunk, n_heads=n_heads, valid=min(valid, tt)),
        grid=(n_seq, nt),
        in_specs=[row_spec(6), row_spec(7), row_spec(8), row_spec(9), row_spec(10),
                  const((CONV_WIDTH, gw)), state_spec, const((1, gw)), const((1, gw)),
                  const((n_heads, CHUNK_D, CHUNK_D)), const((CHUNK_D, n_heads))],
        out_specs=[out_rows, out_rows, state_spec, out_rows],
        out_shape=[jax.ShapeDtypeStruct((n_seq * seq, gw), BF16), jax.ShapeDtypeStruct((n_seq * seq, gw), BF16),
                   jax.ShapeDtypeStruct((n_seq, CONV_WIDTH - 1, gw), F32),
                   jax.ShapeDtypeStruct((n_seq * seq, gw), F32)],
        scratch_shapes=[pltpu.VMEM((CONV_WIDTH - 1, gw), F32)],
        compiler_params=_params("parallel", "arbitrary"),
        name="conv_mlp",
    )(proj, proj, proj, proj, proj, conv_w, buf, ln_g, ln_b, ws, bs_t)


def _layer_norm_rows(x, g, b):
    mu = jnp.mean(x, axis=1, keepdims=True)
    xc = x - mu
    var = jnp.mean(xc * xc, axis=1, keepdims=True)
    return xc * lax.rsqrt(var + NORM_EPS) * g + b


def _ln_router_kernel(x_ref, mix_ref, g_ref, b_ref, wh_ref, wl_ref, y_ref, idx_ref, gate_ref, *, alpha):
    y = _layer_norm_rows(alpha * x_ref[...] + mix_ref[...], g_ref[...], b_ref[...])
    y_ref[...] = y[:, None, :]
    yh, yl = _split_bf16(y)
    lg = _dot(yh, wh_ref[...]) + _dot(yl, wh_ref[...]) + _dot(yh, wl_ref[...])
    lane = lax.broadcasted_iota(jnp.int32, lg.shape, 1)
    neg = -jnp.inf
    glog = jnp.where(lane < N_GROUPS, lg, neg)
    gmax = jnp.max(glog, axis=1, keepdims=True)
    g_star = jnp.min(jnp.where(glog == gmax, lane, LANES), axis=1, keepdims=True)
    p_top = 1.0 / jnp.sum(jnp.exp(glog - gmax), axis=1, keepdims=True)
    lo = N_GROUPS + EXPERTS_PER_GROUP * g_star
    w1 = jnp.where((lane >= lo) & (lane < lo + EXPERTS_PER_GROUP), lg, neg)
    v1 = jnp.max(w1, axis=1, keepdims=True)
    i1 = jnp.min(jnp.where(w1 == v1, lane, LANES), axis=1, keepdims=True)
    w2 = jnp.where(lane == i1, neg, w1)
    v2 = jnp.max(w2, axis=1, keepdims=True)
    i2 = jnp.min(jnp.where(w2 == v2, lane, LANES), axis=1, keepdims=True)
    e21 = jnp.exp(v2 - v1)
    gate1 = p_top / (1.0 + e21)
    gate2 = p_top * e21 / (1.0 + e21)
    idx_ref[...] = jnp.where(lane == 0, i1 - N_GROUPS, jnp.where(lane == 1, i2 - N_GROUPS, 0))
    gate_ref[...] = jnp.where(lane == 0, gate1, jnp.where(lane == 1, gate2, 0.0))


def _ln_router(x, mix, g, b, wr_hi, wr_lo, alpha):
    m, d = x.shape
    tm = _pick(m, 256, 8)
    rows = pl.BlockSpec((tm, d), lambda i: (i, 0))
    const = lambda shape: pl.BlockSpec(shape, lambda i: (0, 0))
    small = pl.BlockSpec((tm, LANES), lambda i: (i, 0))
    return pl.pallas_call(
        functools.partial(_ln_router_kernel, alpha=alpha),
        grid=(m // tm,),
        in_specs=[rows, rows, const((1, d)), const((1, d)), const((d, LANES)), const((d, LANES))],
        out_specs=[pl.BlockSpec((tm, 1, d), lambda i: (i, 0, 0)), small, small],
        out_shape=[jax.ShapeDtypeStruct((m, 1, d), F32),
                   jax.ShapeDtypeStruct((m, LANES), jnp.int32), jax.ShapeDtypeStruct((m, LANES), F32)],
        compiler_params=_params("parallel"),
        name="ln_router",
    )(x, mix, g, b, wr_hi, wr_lo)


def _dispatch_kernel(dest_ref, x_hbm, xs_hbm, sem, *, tm):
    base = pl.program_id(0) * tm

    def copy(i, k):
        return pltpu.make_async_copy(x_hbm.at[pl.ds(base + i, 1)], xs_hbm.at[pl.ds(dest_ref[0, 0, 2 * i + k], 1)], sem)

    def start(i, _):
        copy(i, 0).start()
        copy(i, 1).start()
        return 0

    def wait(i, _):
        copy(i, 0).wait()
        copy(i, 1).wait()
        return 0

    lax.fori_loop(0, tm, start, 0)
    lax.fori_loop(0, tm, wait, 0)


def _dispatch(xb, dest, n_rows):
    m, _, d = xb.shape
    tm = _pick(m, 256, 8)
    dest3 = dest.reshape(m // tm, 1, TOP_K * tm)
    return pl.pallas_call(
        functools.partial(_dispatch_kernel, tm=tm),
        grid=(m // tm,),
        in_specs=[pl.BlockSpec((1, 1, TOP_K * tm), lambda i: (i, 0, 0), memory_space=pltpu.SMEM),
                  pl.BlockSpec(memory_space=pl.ANY)],
        out_specs=pl.BlockSpec(memory_space=pl.ANY),
        out_shape=jax.ShapeDtypeStruct((n_rows, 1, d), xb.dtype),
        scratch_shapes=[pltpu.SemaphoreType.DMA(())],
        compiler_params=pltpu.CompilerParams(dimension_semantics=("arbitrary",), has_side_effects=True),
        name="moe_dispatch",
    )(dest3, xb)


def _expert_kernel(ie_ref, in_ref, nu_ref, x_ref, wg_ref, wu_ref, wd_ref, y_ref, g_acc, u_acc, h_buf, *, ks, sub):
    i = pl.program_id(0)
    s = pl.program_id(1)
    n_sub = (in_ref[i] + sub - 1) // sub
    live = i < nu_ref[0]

    @pl.when(live & (s < ks))
    def _():
        wg = wg_ref[0].astype(BF16)
        wu = wu_ref[0].astype(BF16)

        def body(r, _):
            rows = pl.ds(pl.multiple_of(r * sub, sub), sub)
            x = x_ref[rows, 0, :].astype(BF16)
            dg = _dot(x, wg)
            du = _dot(x, wu)

            @pl.when(s == 0)
            def _():
                g_acc[rows, :] = dg
                u_acc[rows, :] = du

            @pl.when(s > 0)
            def _():
                g_acc[rows, :] += dg
                u_acc[rows, :] += du
            return 0

        lax.fori_loop(0, n_sub, body, 0)

    @pl.when(live & (s == ks))
    def _():
        def body(r, _):
            rows = pl.ds(pl.multiple_of(r * sub, sub), sub)
            g = g_acc[rows, :]
            h_buf[rows, :] = (g * _sigmoid(g) * u_acc[rows, :]).astype(BF16)
            return 0

        lax.fori_loop(0, n_sub, body, 0)

    @pl.when(live & (s >= ks))
    def _():
        wd = wd_ref[0].astype(BF16)
        y_ref[...] = jnp.zeros(y_ref.shape, y_ref.dtype)

        def body(r, _):
            rows = pl.ds(pl.multiple_of(r * sub, sub), sub)
            y_ref[rows, 0, :] = _dot(h_buf[rows, :], wd)
            return 0

        lax.fori_loop(0, n_sub, body, 0)


def _experts(xs, item_e, item_n, n_used, w_gate, w_up, w_down, layer, n_items):
    d = xs.shape[-1]
    hidden = w_gate.shape[-1]
    n_exp = w_gate.shape[1]
    ks = MOE_K_STEPS
    kt = d // ks
    wg3 = w_gate.reshape(-1, d, hidden)
    wu3 = w_up.reshape(-1, d, hidden)
    wd3 = w_down.reshape(-1, hidden, d)
    e0 = layer * n_exp
    rows = MOE_ITEM_ROWS

    def item(i, nu):
        return jnp.minimum(i, nu[0] - 1)

    def step(i, s, nu):
        return jnp.where(i < nu[0], s, 2 * ks - 1)

    grid_spec = pltpu.PrefetchScalarGridSpec(
        num_scalar_prefetch=3,
        grid=(n_items, 2 * ks),
        in_specs=[
            pl.BlockSpec((rows, 1, kt),
                         lambda i, s, ie, inn, nu: (item(i, nu), 0, jnp.minimum(step(i, s, nu), ks - 1))),
            pl.BlockSpec((1, kt, hidden),
                         lambda i, s, ie, inn, nu: (e0 + ie[item(i, nu)], jnp.minimum(step(i, s, nu), ks - 1), 0)),
            pl.BlockSpec((1, kt, hidden),
                         lambda i, s, ie, inn, nu: (e0 + ie[item(i, nu)], jnp.minimum(step(i, s, nu), ks - 1), 0)),
            pl.BlockSpec((1, hidden, kt),
                         lambda i, s, ie, inn, nu: (e0 + ie[item(i, nu)], 0, jnp.maximum(step(i, s, nu) - ks, 0))),
        ],
        out_specs=pl.BlockSpec((rows, 1, kt),
                               lambda i, s, ie, inn, nu: (item(i, nu), 0, jnp.maximum(step(i, s, nu) - ks, 0))),
        scratch_shapes=[pltpu.VMEM((rows, hidden), F32), pltpu.VMEM((rows, hidden), F32),
                        pltpu.VMEM((rows, hidden), BF16)],
    )
    return pl.pallas_call(
        functools.partial(_expert_kernel, ks=ks, sub=MOE_SUB_ROWS),
        grid_spec=grid_spec,
        out_shape=jax.ShapeDtypeStruct((n_items * rows, 1, d), F32),
        compiler_params=_params("arbitrary", "arbitrary"),
        name="moe_experts",
    )(item_e, item_n, n_used, xs, wg3, wu3, wd3)


def _combine_kernel(dest_ref, x_ref, gate_ref, g_ref, b_ref, ys_hbm, y_ref, yb_ref, buf, sem, *, tm, alpha):
    def copy(i, k):
        return pltpu.make_async_copy(ys_hbm.at[pl.ds(dest_ref[0, 0, 2 * i + k], 1)], buf.at[k, pl.ds(i, 1)], sem)

    def start(i, _):
        copy(i, 0).start()
        copy(i, 1).start()
        return 0

    def wait(i, _):
        copy(i, 0).wait()
        copy(i, 1).wait()
        return 0

    lax.fori_loop(0, tm, start, 0)
    lax.fori_loop(0, tm, wait, 0)
    gates = gate_ref[...]
    ffn = gates[:, 0:1] * buf[0, :, 0, :] + gates[:, 1:2] * buf[1, :, 0, :]
    y = _layer_norm_rows(alpha * x_ref[:, 0, :] + ffn, g_ref[...], b_ref[...])
    y_ref[...] = y
    yb_ref[...] = y.astype(BF16)


def _combine(x1, gates, dest, ys, g, b, alpha):
    m, _, d = x1.shape
    tm = _pick(m, 256, 8)
    dest3 = dest.reshape(m // tm, 1, TOP_K * tm)
    rows = pl.BlockSpec((tm, d), lambda i: (i, 0))
    const = pl.BlockSpec((1, d), lambda i: (0, 0))
    return pl.pallas_call(
        functools.partial(_combine_kernel, tm=tm, alpha=alpha),
        grid=(m // tm,),
        in_specs=[pl.BlockSpec((1, 1, TOP_K * tm), lambda i: (i, 0, 0), memory_space=pltpu.SMEM),
                  pl.BlockSpec((tm, 1, d), lambda i: (i, 0, 0)), pl.BlockSpec((tm, LANES), lambda i: (i, 0)),
                  const, const, pl.BlockSpec(memory_space=pl.ANY)],
        out_specs=[rows, rows],
        out_shape=[jax.ShapeDtypeStruct((m, d), F32), jax.ShapeDtypeStruct((m, d), BF16)],
        scratch_shapes=[pltpu.VMEM((TOP_K, tm, 1, d), F32), pltpu.SemaphoreType.DMA(())],
        compiler_params=_params("arbitrary"),
        name="moe_combine",
    )(dest3, x1, gates, g, b, ys)


def _route_plan(idx, n_items):
    e_flat = idx[:, :TOP_K].reshape(-1)
    onehot = (e_flat[:, None] == jnp.arange(N_EXPERTS, dtype=jnp.int32)[None, :]).astype(jnp.int32)
    csum = jnp.cumsum(onehot, axis=0)
    rank = jnp.sum(onehot * csum, axis=1) - 1
    counts = csum[-1]
    items_e = (counts + MOE_ITEM_ROWS - 1) // MOE_ITEM_ROWS
    item_end = jnp.cumsum(items_e)
    item_start = item_end - items_e
    dest = (item_start[e_flat] * MOE_ITEM_ROWS + rank).astype(jnp.int32)
    ids = jnp.arange(n_items, dtype=jnp.int32)
    item_e = jnp.minimum(jnp.searchsorted(item_end, ids, side="right"), N_EXPERTS - 1).astype(jnp.int32)
    n_used = item_end[-1].astype(jnp.int32)
    item_n = jnp.clip(counts[item_e] - (ids - item_start[item_e]) * MOE_ITEM_ROWS, 0, MOE_ITEM_ROWS)
    item_n = jnp.where(ids < n_used, item_n, 0).astype(jnp.int32)
    return dest, item_e, item_n, n_used.reshape(1)


def kernel(x_prompt, x_sample, cache_k, cache_v, page_table, state_gla, state_conv, rel_table,
           w_in, w_out, lam_q1, lam_k1, lam_q2, lam_k2, diff_norm_g, gla_w_up, gla_b, gla_norm_g,
           conv_w, cm_ln_g, cm_ln_b, cm_ws, cm_bs, ln1_g, ln1_b, ln2_g, ln2_b,
           router_group, router_expert, w_gate, w_up, w_down):
    bp, seq, d = x_prompt.shape
    bs, ts, _ = x_sample.shape
    depth = w_in.shape[0]
    gw = w_out.shape[1] // 4
    nh = gw // HEAD_DIM
    hk = nh * HALF_DIM
    page = cache_k.shape[2]
    n_phys = cache_k.shape[1]
    tp = SAMPLE_ROWS
    n_prompt = bp * seq
    n_all = n_prompt + bs * tp
    alpha = (2.0 * depth) ** 0.25
    assert ts <= tp and nh % 2 == 0 and n_prompt % tp == 0

    x = jnp.concatenate([x_prompt.reshape(n_prompt, d),
                         jnp.pad(x_sample, ((0, 0), (0, tp - ts), (0, 0))).reshape(bs * tp, d)], axis=0)
    xb = x.astype(BF16)

    bias_tiles = _prompt_bias_tiles(rel_table, ATTN_TILE)
    t_idx = jnp.arange(ts)
    dist_last = page + t_idx[:, None] - jnp.arange(page)[None, :]
    b_last = jnp.transpose(_rel_bias(dist_last, rel_table), (2, 0, 1)) * LOG2E
    dist_new = t_idx[:, None] - jnp.arange(page)[None, :]
    b_new = jnp.transpose(_rel_bias(dist_new, rel_table), (2, 0, 1)) * LOG2E
    b_new = jnp.where(((dist_new >= 0) & (jnp.arange(page)[None, :] < ts))[None], b_new, MASK_VALUE)
    b_far = _rel_bias(jnp.full((1,), 2 * page, jnp.int32), rel_table)[0] * LOG2E
    rows_s = 2 * nh * ts
    bias_last = jnp.tile(b_last.reshape(nh * ts, page), (2, 1))
    bias_new = jnp.tile(b_new.reshape(nh * ts, page), (2, 1))
    c_row = jnp.tile(jnp.repeat(b_far, ts), 2).reshape(rows_s, 1)
    head_eye = jnp.eye(nh, dtype=F32)
    half_mask = (jnp.arange(HEAD_DIM)[None, :] < HALF_DIM) == (jnp.arange(2)[:, None] == 0)

    cache_k4 = cache_k.reshape(depth, n_phys, page, gw)
    cache_v4 = cache_v.reshape(depth, n_phys, page, gw)
    n_items = (TOP_K * n_all) // MOE_ITEM_ROWS + N_EXPERTS
    tm_proj = _pick(n_all, 1024, 16)

    outs = {k: [] for k in ("kp", "vp", "ks", "vs", "gp", "gs", "cp", "cs", "ds")}
    for l in range(depth):
        lam_init = 0.8 - 0.6 * math.exp(-0.3 * l)
        lams = (lam_q1[l][None], lam_k1[l][None], lam_q2[l][None], lam_k2[l][None])
        gn_a = diff_norm_g[l][None]

        w_main = jnp.concatenate([w_in[l, :, :5 * gw], w_in[l, :, 5 * gw + GLA_RANK:]], axis=1).astype(BF16)
        w_gate_cols = jnp.pad(w_in[l, :, 5 * gw:5 * gw + GLA_RANK], ((0, 0), (0, LANES - GLA_RANK))).astype(BF16)
        proj = _matmul([xb], w_main, 11 * gw, tm_proj, _pick(11 * gw, 1024, LANES), "in_proj")
        gproj = _matmul([xb], w_gate_cols, LANES, tm_proj, LANES, "gate_proj")
        proj_s = proj[n_prompt:].reshape(bs, tp, 11 * gw)[:, :ts]

        oa_p = _attn_prompt(proj, lams, bias_tiles, gn_a, bp, seq, nh, lam_init)
        q_s = proj_s[..., :gw].reshape(bs, ts, nh, HEAD_DIM) * (HALF_DIM ** -0.5 * LOG2E)
        q_bd = jnp.einsum("bthd,md,hg->bmhtgd", q_s, half_mask.astype(F32), head_eye).reshape(bs, rows_s, gw)
        k_s = proj_s[..., gw:2 * gw]
        v_s = proj_s[..., 2 * gw:3 * gw]
        pad_new = ((0, 0), (0, page - ts), (0, 0))
        oa_s = _attn_sample(l, q_bd.astype(BF16), cache_k4, cache_v4, jnp.pad(k_s, pad_new), jnp.pad(v_s, pad_new),
                            page_table, bias_last, bias_new, c_row, lams, gn_a, nh, ts, lam_init)
        oa_s = jnp.pad(oa_s, ((0, 0), (0, tp - ts), (0, 0))).reshape(bs * tp, gw).astype(BF16)

        wup = jnp.pad(gla_w_up[l], ((0, LANES - GLA_RANK), (0, 0)))
        wup_hi, wup_lo = _split_bf16(wup)
        gb = gla_b[l][None]
        gn_b = gla_norm_g[l][None]
        s0_s = jnp.transpose(state_gla[l].reshape(bs, nh // 2, 2 * HALF_DIM, HEAD_DIM), (0, 1, 3, 2))
        s0_p = jnp.zeros((bp, nh // 2, HEAD_DIM, LANES), F32)
        ob_p, sf_p = _gla(proj, gproj, wup_hi, wup_lo, gb, gn_b, s0_p, 0, bp, seq, nh, seq)
        ob_s, sf_s = _gla(proj, gproj, wup_hi, wup_lo, gb, gn_b, s0_s, n_prompt, bs, tp, nh, ts)
        unpair = lambda s: jnp.transpose(s, (0, 1, 3, 2)).reshape(s.shape[0], nh, HALF_DIM, HEAD_DIM)

        bs_t = jnp.transpose(cm_bs[l])
        cw, lg, lb = conv_w[l], cm_ln_g[l][None], cm_ln_b[l][None]
        oc_p, od_p, cs_p, _ = _conv_mlp(proj, cw, jnp.zeros((bp, CONV_WIDTH - 1, gw), F32), lg, lb, cm_ws[l], bs_t,
                                        0, bp, seq, nh, seq)
        oc_s, od_s, cs_s, vd_s = _conv_mlp(proj, cw, state_conv[l], lg, lb, cm_ws[l], bs_t, n_prompt, bs, tp, nh, ts)

        cat = lambda a, b: jnp.concatenate([a, b], axis=0)
        mix = _matmul([cat(oa_p, oa_s), cat(ob_p, ob_s), cat(oc_p, oc_s), cat(od_p, od_s)], w_out[l].astype(BF16),
                      d, tm_proj, _pick(d, 1024, LANES), "out_proj")
        wr = jnp.pad(jnp.concatenate([router_group[l], router_expert[l]], axis=1),
                     ((0, 0), (0, LANES - N_GROUPS - N_EXPERTS)))
        wr_hi, wr_lo = _split_bf16(wr)
        x1, ridx, rgate = _ln_router(x, mix, ln1_g[l][None], ln1_b[l][None], wr_hi, wr_lo, alpha)

        dest, item_e, item_n, n_used = _route_plan(ridx, n_items)
        xs = _dispatch(x1, dest, n_items * MOE_ITEM_ROWS)
        ys = _experts(xs, item_e, item_n, n_used, w_gate, w_up, w_down, l, n_items)
        x, xb = _combine(x1, rgate, dest, ys, ln2_g[l][None], ln2_b[l][None], alpha)

        heads = lambda a, n, t: a.reshape(n, t, nh, HEAD_DIM)
        outs["kp"].append(heads(proj[:n_prompt, gw:2 * gw], bp, seq))
        outs["vp"].append(heads(proj[:n_prompt, 2 * gw:3 * gw], bp, seq))
        outs["ks"].append(heads(k_s, bs, ts))
        outs["vs"].append(heads(v_s, bs, ts))
        outs["gp"].append(unpair(sf_p))
        outs["gs"].append(unpair(sf_s))
        outs["cp"].append(cs_p)
        outs["cs"].append(cs_s)
        outs["ds"].append(vd_s.reshape(bs, tp, gw)[:, :ts])

    y_prompt = x[:n_prompt].reshape(bp, seq, d)
    y_sample = x[n_prompt:].reshape(bs, tp, d)[:, :ts]
    st = lambda k: jnp.stack(outs[k])
    return (y_prompt, y_sample, st("kp"), st("vp"), st("ks"), st("vs"), st("gp"), st("gs"),
            st("cp"), st("cs"), st("ds"))
```

```python
import functools
import math

import jax
import jax.numpy as jnp
from jax import lax
from jax.experimental import pallas as pl
from jax.experimental.pallas import tpu as pltpu

F32 = jnp.float32
BF16 = jnp.bfloat16

HEAD_DIM = 128
HALF_DIM = HEAD_DIM // 2
GLA_RANK = 16
GLA_TAU = 16.0
GLA_CHUNK = 64
CONV_WIDTH = 3
CHUNK_D = 128
REL_BUCKETS = 32
REL_MAX_DIST = 128
N_GROUPS = 4
EXPERTS_PER_GROUP = 8
N_EXPERTS = N_GROUPS * EXPERTS_PER_GROUP
TOP_K = 2
NORM_EPS = 1e-5
MASK_VALUE = -1e30
LOG2E = 1.4426950408889634

LANES = 128
SAMPLE_ROWS = 64
ATTN_TILE = 256
MOE_ITEM_ROWS = 1024
MOE_SUB_ROWS = 256
MOE_K_STEPS = 8
VMEM_LIMIT = 52 * 1024 * 1024


def _pick(n, target, mult):
    if n <= target:
        return n
    best = None
    for d in range(mult, target + 1, mult):
        if n % d == 0:
            best = d
    assert best is not None, (n, target, mult)
    return best


def _params(*sem):
    return pltpu.CompilerParams(dimension_semantics=sem, vmem_limit_bytes=VMEM_LIMIT)


def _dot(a, b):
    return jnp.dot(a, b, preferred_element_type=F32)


def _dot_nt(a, b):
    return lax.dot_general(a, b, (((1,), (1,)), ((), ())), preferred_element_type=F32)


def _dot_tn(a, b):
    return lax.dot_general(a, b, (((0,), (0,)), ((), ())), preferred_element_type=F32)


def _split_bf16(x):
    hi = x.astype(BF16)
    lo = (x - hi.astype(F32)).astype(BF16)
    return hi, lo


def _sigmoid(x):
    return 1.0 / (1.0 + jnp.exp(-x))


def _mm_kernel(*refs, n_in):
    o_ref = refs[2 * n_in]
    acc = None
    for x_ref, w_ref in zip(refs[:n_in], refs[n_in:2 * n_in]):
        d = _dot(x_ref[...], w_ref[...])
        acc = d if acc is None else acc + d
    o_ref[...] = acc.astype(o_ref.dtype)


def _matmul(xs, w, n_out, tm, tn, name):
    n_in = len(xs)
    m, kg = xs[0].shape
    grid = (m // tm, n_out // tn)
    in_specs = [pl.BlockSpec((tm, kg), lambda i, j: (i, 0)) for _ in xs]
    in_specs += [pl.BlockSpec((kg, tn), functools.partial(lambda i, j, g: (g, j), g=g)) for g in range(n_in)]
    return pl.pallas_call(
        functools.partial(_mm_kernel, n_in=n_in),
        grid=grid,
        in_specs=in_specs,
        out_specs=pl.BlockSpec((tm, tn), lambda i, j: (i, j)),
        out_shape=jax.ShapeDtypeStruct((m, n_out), F32),
        compiler_params=_params("parallel", "arbitrary"),
        name=name,
    )(*xs, *([w] * n_in))


def _rel_bias(dist, rel_table):
    n = jnp.maximum(dist, 0)
    max_exact = REL_BUCKETS // 2
    nf = jnp.maximum(n, 1).astype(F32)
    large = max_exact + (jnp.log(nf / max_exact) / math.log(REL_MAX_DIST / max_exact)
                         * (REL_BUCKETS - max_exact)).astype(jnp.int32)
    bucket = jnp.where(n < max_exact, n, jnp.minimum(large, REL_BUCKETS - 1))
    onehot = (bucket[..., None] == jnp.arange(REL_BUCKETS, dtype=bucket.dtype)).astype(F32)
    return jnp.einsum("...k,kh->...h", onehot, rel_table.astype(F32), precision=lax.Precision.HIGHEST)


def _prompt_bias_tiles(rel_table, t):
    r = jnp.arange(t)[:, None]
    c = jnp.arange(t)[None, :]
    tiles = []
    for delta in range(3):
        dist = delta * t + r - c
        b = _rel_bias(dist, rel_table) * LOG2E
        tiles.append(jnp.where((dist >= 0)[..., None], b, MASK_VALUE))
    return jnp.transpose(jnp.stack(tiles), (3, 0, 1, 2))


def _lam_value(lq1, lk1, lq2, lk2, lam_init):
    return (jnp.exp(jnp.sum(lq1[...] * lk1[...], axis=1, keepdims=True))
            - jnp.exp(jnp.sum(lq2[...] * lk2[...], axis=1, keepdims=True)) + lam_init)


def _rms_head(o, gain, post):
    return o * lax.rsqrt(jnp.mean(o * o, axis=1, keepdims=True) + NORM_EPS) * gain * post


def _attn_prompt_kernel(lq1, lk1, lq2, lk2, q_ref, k_ref, v_ref, bias_ref, gn_ref, o_ref, *, t, lam_init):
    qi = pl.program_id(2)
    lam = _lam_value(lq1, lk1, lq2, lk2, lam_init)
    q = q_ref[...] * (HALF_DIM ** -0.5 * LOG2E)
    lane = lax.broadcasted_iota(jnp.int32, q.shape, 1)
    qq = jnp.concatenate([jnp.where(lane < HALF_DIM, q, 0.0), jnp.where(lane >= HALF_DIM, q, 0.0)],
                         axis=0).astype(BF16)

    def body(j, carry):
        m, s, acc = carry
        start = pl.multiple_of(j * t, t)
        kt = k_ref[pl.ds(start, t), :].astype(BF16)
        vt = v_ref[pl.ds(start, t), :].astype(BF16)
        bt = bias_ref[0, jnp.minimum(qi - j, 2)]
        l = _dot_nt(qq, kt) + jnp.concatenate([bt, bt], axis=0)
        m_new = jnp.maximum(m, jnp.max(l, axis=1, keepdims=True))
        a = jnp.exp2(m - m_new)
        p = jnp.exp2(l - m_new)
        s = a * s + jnp.sum(p, axis=1, keepdims=True)
        acc = a * acc + _dot(p.astype(BF16), vt)
        return m_new, s, acc

    init = (jnp.full((2 * t, 1), MASK_VALUE, F32), jnp.zeros((2 * t, 1), F32), jnp.zeros((2 * t, HEAD_DIM), F32))
    _, s, acc = lax.fori_loop(0, qi + 1, body, init)
    o = acc / s
    o = o[:t] - lam * o[t:]
    o_ref[...] = _rms_head(o, gn_ref[...], 1.0 - lam_init).astype(o_ref.dtype)


def _attn_prompt(proj, lams, bias_tiles, gn, n_seq, seq, n_heads, lam_init):
    t = ATTN_TILE
    assert seq % t == 0 and t >= REL_MAX_DIST
    nq = seq // t
    lam_spec = pl.BlockSpec((1, HALF_DIM), lambda b, h, i: (0, 0))
    return pl.pallas_call(
        functools.partial(_attn_prompt_kernel, t=t, lam_init=lam_init),
        grid=(n_seq, n_heads, nq),
        in_specs=[lam_spec] * 4 + [
            pl.BlockSpec((t, HEAD_DIM), lambda b, h, i: (b * nq + i, h)),
            pl.BlockSpec((seq, HEAD_DIM), lambda b, h, i: (b, n_heads + h)),
            pl.BlockSpec((seq, HEAD_DIM), lambda b, h, i: (b, 2 * n_heads + h)),
            pl.BlockSpec((1, 3, t, t), lambda b, h, i: (h, 0, 0, 0)),
            pl.BlockSpec((1, HEAD_DIM), lambda b, h, i: (0, 0)),
        ],
        out_specs=pl.BlockSpec((t, HEAD_DIM), lambda b, h, i: (b * nq + i, h)),
        out_shape=jax.ShapeDtypeStruct((n_seq * seq, n_heads * HEAD_DIM), BF16),
        compiler_params=_params("parallel", "parallel", "arbitrary"),
        name="attn_prompt",
    )(*lams, proj, proj, proj, bias_tiles, gn)


def _attn_sample_kernel(pt_ref, lq1, lk1, lq2, lk2, q_ref, *refs, n_pp, lam_init):
    kc_refs, vc_refs = refs[:n_pp], refs[n_pp:2 * n_pp]
    kn_ref, vn_ref, bias_ref, gn_ref, o_ref, m_ref, s_ref, acc_ref = refs[2 * n_pp:]
    p = pl.program_id(1)
    n_steps = pl.num_programs(1)

    @pl.when(p == 0)
    def _():
        m_ref[...] = jnp.full(m_ref.shape, MASK_VALUE, F32)
        s_ref[...] = jnp.zeros(s_ref.shape, F32)
        acc_ref[...] = jnp.zeros(acc_ref.shape, F32)

    q = q_ref[0]

    def step(k, v, bias):
        l = _dot_nt(q, k.astype(BF16)) + bias
        m = m_ref[...]
        m_new = jnp.maximum(m, jnp.max(l, axis=1, keepdims=True))
        a = jnp.exp2(m - m_new)
        pr = jnp.exp2(l - m_new)
        s_ref[...] = a * s_ref[...] + jnp.sum(pr, axis=1, keepdims=True)
        acc_ref[...] = a * acc_ref[...] + _dot(pr.astype(BF16), v.astype(BF16))
        m_ref[...] = m_new

    last = p == n_steps - 1
    for i in range(n_pp):
        table = jnp.where(last, 1, 0) if i == n_pp - 1 else 0
        step(kc_refs[i][0, 0], vc_refs[i][0, 0], bias_ref[table])

    @pl.when(last)
    def _():
        step(kn_ref[0], vn_ref[0], bias_ref[2])
        lam = _lam_value(lq1, lk1, lq2, lk2, lam_init)
        o = acc_ref[...] / s_ref[...]
        half = o.shape[0] // 2
        o_ref[0] = _rms_head(o[:half] - lam * o[half:], gn_ref[...], 1.0 - lam_init)


def _attn_sample(layer, q2, cache_k, cache_v, k_new, v_new, page_table, bias, lams, gn, lam_init):
    n_seq, n_pages = page_table.shape
    prow = cache_k.shape[2]
    rows = q2.shape[1]
    n_pp = 4 if n_pages % 4 == 0 else 1
    lam_spec = pl.BlockSpec((1, HALF_DIM), lambda b, p, pt: (0, 0))
    cache_specs = [pl.BlockSpec((1, 1, prow, HEAD_DIM),
                                functools.partial(lambda b, p, pt, i: (layer, pt[b, p * n_pp + i], 0, 0), i=i))
                   for i in range(n_pp)]
    new_spec = pl.BlockSpec((1, prow, HEAD_DIM), lambda b, p, pt: (b, 0, 0))
    grid_spec = pltpu.PrefetchScalarGridSpec(
        num_scalar_prefetch=1,
        grid=(n_seq, n_pages // n_pp),
        in_specs=[lam_spec] * 4 + [pl.BlockSpec((1, rows, HEAD_DIM), lambda b, p, pt: (b, 0, 0))]
        + cache_specs + cache_specs + [
            new_spec, new_spec,
            pl.BlockSpec((3, rows, prow), lambda b, p, pt: (0, 0, 0)),
            pl.BlockSpec((1, HEAD_DIM), lambda b, p, pt: (0, 0)),
        ],
        out_specs=pl.BlockSpec((1, rows // 2, HEAD_DIM), lambda b, p, pt: (b, 0, 0)),
        scratch_shapes=[pltpu.VMEM((rows, 1), F32), pltpu.VMEM((rows, 1), F32), pltpu.VMEM((rows, HEAD_DIM), F32)],
    )
    return pl.pallas_call(
        functools.partial(_attn_sample_kernel, n_pp=n_pp, lam_init=lam_init),
        grid_spec=grid_spec,
        out_shape=jax.ShapeDtypeStruct((n_seq, rows // 2, HEAD_DIM), F32),
        compiler_params=_params("parallel", "arbitrary"),
        name="attn_sample",
    )(page_table, *lams, q2, *([cache_k] * n_pp), *([cache_v] * n_pp), k_new, v_new, bias, gn)


def _gla_kernel(q_ref, k_ref, v_ref, r_ref, gp_ref, wh_ref, wl_ref, gb_ref, gn_ref, s0_ref, o_ref, sfin_ref, st_ref,
                *, chunk, n_chunks, n_heads, valid):
    t = pl.program_id(1)

    @pl.when(t == 0)
    def _():
        st_ref[...] = s0_ref[0]

    row = lax.broadcasted_iota(jnp.int32, (chunk, chunk), 0)
    col = lax.broadcasted_iota(jnp.int32, (chunk, chunk), 1)
    tril = row >= col
    tri_bf = jnp.where(tril, 1.0, 0.0).astype(BF16)
    lane = lax.broadcasted_iota(jnp.int32, (chunk, LANES), 1)
    lane_sq = lax.broadcasted_iota(jnp.int32, (HEAD_DIM, LANES), 1)
    mid = chunk // 2 - 1

    for c in range(n_chunks):
        rs = slice(c * chunk, (c + 1) * chunk)
        gh, gl = _split_bf16(gp_ref[rs, :])
        z = _dot(gh, wh_ref[...]) + _dot(gl, wh_ref[...]) + _dot(gh, wl_ref[...]) + gb_ref[...]
        g = -(jnp.maximum(-z, 0.0) + jnp.log(1.0 + jnp.exp(-jnp.abs(z)))) * (1.0 / GLA_TAU)
        k = k_ref[rs, :]
        if valid < chunk:
            ok = lax.broadcasted_iota(jnp.int32, g.shape, 0) < valid
            g = jnp.where(ok, g, 0.0)
            k = jnp.where(ok, k, 0.0)
        gh, gl = _split_bf16(g)
        b = _dot(tri_bf, gh) + _dot(tri_bf, gl)
        b_last = b[chunk - 1:chunk, :]
        b_mid = b[mid:mid + 1, :]
        q = q_ref[rs, :] * (HALF_DIM ** -0.5)
        q_state = q * jnp.exp(b)
        q_mid = q * jnp.exp(b - b_mid)
        k_mid = (k * jnp.exp(b_mid - b)).astype(BF16)
        k_end = (k * jnp.exp(b_last - b)).astype(BF16)
        decay = jnp.exp(b_last)

        for hp in range(n_heads // 2):
            ls = slice(hp * LANES, (hp + 1) * LANES)
            st = st_ref[hp]
            st_bf = st.astype(BF16)
            upd = []
            for par in range(2):
                h = 2 * hp + par
                mine = (lane < HALF_DIM) if par == 0 else (lane >= HALF_DIM)
                hs = slice(h * HEAD_DIM, (h + 1) * HEAD_DIM)
                v_h = v_ref[rs, hs].astype(BF16)
                att = _dot_nt(jnp.where(mine, q_mid[:, ls], 0.0).astype(BF16), k_mid[:, ls])
                att = jnp.where(tril, att, 0.0).astype(BF16)
                o = _dot(att, v_h) + _dot_nt(jnp.where(mine, q_state[:, ls], 0.0).astype(BF16), st_bf)
                r_h = r_ref[rs, hs]
                o_ref[rs, hs] = _rms_head(o, gn_ref[...], r_h * _sigmoid(r_h)).astype(o_ref.dtype)
                upd.append(_dot_tn(v_h, k_end[:, ls]))
            st_ref[hp] = st * decay[:, ls] + jnp.where(lane_sq < HALF_DIM, upd[0], upd[1])

    @pl.when(t == pl.num_programs(1) - 1)
    def _():
        sfin_ref[0] = st_ref[...]


def _gla(proj, gproj, wup_hi, wup_lo, gla_b, gn, s0, row0, n_seq, seq, n_heads, valid):
    gw = n_heads * HEAD_DIM
    hk = n_heads * HALF_DIM
    chunk = min(GLA_CHUNK, seq)
    tt = _pick(seq, 256, chunk)
    nt = seq // tt
    r0 = row0 // tt
    assert row0 % tt == 0
    qcol = 3 * gw // hk
    row_spec = lambda width, colblk: pl.BlockSpec((tt, width), lambda b, t: (r0 + b * nt + t, colblk))
    const = lambda shape: pl.BlockSpec(shape, lambda b, t: tuple(0 for _ in shape))
    st_spec = pl.BlockSpec((1, n_heads // 2, HEAD_DIM, LANES), lambda b, t: (b, 0, 0, 0))
    return pl.pallas_call(
        functools.partial(_gla_kernel, chunk=chunk, n_chunks=tt // chunk, n_heads=n_heads, valid=valid),
        grid=(n_seq, nt),
        in_specs=[row_spec(hk, qcol), row_spec(hk, qcol + 1), row_spec(gw, 4), row_spec(gw, 5),
                  pl.BlockSpec((tt, LANES), lambda b, t: (r0 + b * nt + t, 0)),
                  const((LANES, hk)), const((LANES, hk)), const((1, hk)), const((1, HEAD_DIM)), st_spec],
        out_specs=[pl.BlockSpec((tt, gw), lambda b, t: (b * nt + t, 0)), st_spec],
        out_shape=[jax.ShapeDtypeStruct((n_seq * seq, gw), BF16),
                   jax.ShapeDtypeStruct((n_seq, n_heads // 2, HEAD_DIM, LANES), F32)],
        scratch_shapes=[pltpu.VMEM((n_heads // 2, HEAD_DIM, LANES), F32)],
        compiler_params=_params("parallel", "arbitrary"),
        name="gla",
    )(proj, proj, proj, proj, gproj, wup_hi, wup_lo, gla_b, gn, s0)


def _conv_mlp_kernel(cb_ref, cc_ref, ch_ref, du_ref, dv_ref, cw_ref, buf_ref, lg_ref, lb_ref, ws_ref, bs_ref,
                     oc_ref, od_ref, cs_ref, vd_ref, carry_ref, *, tt, chunk, n_heads, valid):
    t = pl.program_id(1)

    @pl.when(t == 0)
    def _():
        carry_ref[...] = buf_ref[0]

    z = cc_ref[...] * ch_ref[...]
    row = lax.broadcasted_iota(jnp.int32, z.shape, 0)
    c0 = carry_ref[0:1, :]
    c1 = carry_ref[1:2, :]
    z1 = jnp.where(row == 0, c1, pltpu.roll(z, 1, 0))
    z2 = jnp.where(row == 0, c0, jnp.where(row == 1, c1, pltpu.roll(z, 2, 0)))
    w = cw_ref[...]
    oc_ref[...] = (cb_ref[...] * (z2 * w[0:1, :] + z1 * w[1:2, :] + z * w[2:3, :])).astype(oc_ref.dtype)
    if valid >= 2:
        carry_ref[...] = z[valid - 2:valid, :]
    else:
        carry_ref[...] = jnp.concatenate([c1, z[0:1, :]], axis=0)

    @pl.when(t == pl.num_programs(1) - 1)
    def _():
        cs_ref[0] = carry_ref[...]

    x = dv_ref[...]
    mu = jnp.mean(x, axis=1, keepdims=True)
    xc = x - mu
    var = jnp.mean(xc * xc, axis=1, keepdims=True)
    vd = xc * lax.rsqrt(var + NORM_EPS) * lg_ref[...] + lb_ref[...]
    vd_ref[...] = vd
    vd_bf = vd.astype(BF16)
    r2 = lax.broadcasted_iota(jnp.int32, (chunk, chunk), 0)
    c2 = lax.broadcasted_iota(jnp.int32, (chunk, chunk), 1)
    for g in range(n_heads):
        wc = jnp.where(r2 >= c2, ws_ref[g, :chunk, :chunk], 0.0).astype(BF16)
        bias = bs_ref[:chunk, g:g + 1]
        cols = slice(g * HEAD_DIM, (g + 1) * HEAD_DIM)
        for c in range(tt // chunk):
            rs = slice(c * chunk, (c + 1) * chunk)
            zc = _dot(wc, vd_bf[rs, cols]) + bias
            od_ref[rs, cols] = (du_ref[rs, cols] * zc).astype(od_ref.dtype)


def _conv_mlp(proj, conv_w, buf, ln_g, ln_b, ws, bs_t, row0, n_seq, seq, n_heads, valid):
    gw = n_heads * HEAD_DIM
    chunk = min(CHUNK_D, seq)
    tt = _pick(seq, 256, chunk)
    nt = seq // tt
    r0 = row0 // tt
    assert row0 % tt == 0 and (valid == seq or nt == 1)
    row_spec = lambda colblk: pl.BlockSpec((tt, gw), lambda b, t: (r0 + b * nt + t, colblk))
    const = lambda shape: pl.BlockSpec(shape, lambda b, t: tuple(0 for _ in shape))
    out_rows = pl.BlockSpec((tt, gw), lambda b, t: (b * nt + t, 0))
    state_spec = pl.BlockSpec((1, CONV_WIDTH - 1, gw), lambda b, t: (b, 0, 0))
    return pl.pallas_call(
        functools.partial(_conv_mlp_kernel, tt=tt, chunk=chunk, n_heads=n_heads, valid=min(valid, tt)),
        grid=(n_seq, nt),
        in_specs=[row_spec(6), row_spec(7), row_spec(8), row_spec(9), row_spec(10),
                  const((CONV_WIDTH, gw)), state_spec, const((1, gw)), const((1, gw)),
                  const((n_heads, CHUNK_D, CHUNK_D)), const((CHUNK_D, n_heads))],
        out_specs=[out_rows, out_rows, state_spec, out_rows],
        out_shape=[jax.ShapeDtypeStruct((n_seq * seq, gw), BF16), jax.ShapeDtypeStruct((n_seq * seq, gw), BF16),
                   jax.ShapeDtypeStruct((n_seq, CONV_WIDTH - 1, gw), F32),
                   jax.ShapeDtypeStruct((n_seq * seq, gw), F32)],
        scratch_shapes=[pltpu.VMEM((CONV_WIDTH - 1, gw), F32)],
        compiler_params=_params("parallel", "arbitrary"),
        name="conv_mlp",
    )(proj, proj, proj, proj, proj, conv_w, buf, ln_g, ln_b, ws, bs_t)


def _layer_norm_rows(x, g, b):
    mu = jnp.mean(x, axis=1, keepdims=True)
    xc = x - mu
    var = jnp.mean(xc * xc, axis=1, keepdims=True)
    return xc * lax.rsqrt(var + NORM_EPS) * g + b


def _ln_router_kernel(x_ref, mix_ref, g_ref, b_ref, wh_ref, wl_ref, y_ref, idx_ref, gate_ref, *, alpha):
    y = _layer_norm_rows(alpha * x_ref[...] + mix_ref[...], g_ref[...], b_ref[...])
    y_ref[...] = y
    yh, yl = _split_bf16(y)
    lg = _dot(yh, wh_ref[...]) + _dot(yl, wh_ref[...]) + _dot(yh, wl_ref[...])
    lane = lax.broadcasted_iota(jnp.int32, lg.shape, 1)
    neg = -jnp.inf
    glog = jnp.where(lane < N_GROUPS, lg, neg)
    gmax = jnp.max(glog, axis=1, keepdims=True)
    g_star = jnp.min(jnp.where(glog == gmax, lane, LANES), axis=1, keepdims=True)
    p_top = 1.0 / jnp.sum(jnp.exp(glog - gmax), axis=1, keepdims=True)
    lo = N_GROUPS + EXPERTS_PER_GROUP * g_star
    w1 = jnp.where((lane >= lo) & (lane < lo + EXPERTS_PER_GROUP), lg, neg)
    v1 = jnp.max(w1, axis=1, keepdims=True)
    i1 = jnp.min(jnp.where(w1 == v1, lane, LANES), axis=1, keepdims=True)
    w2 = jnp.where(lane == i1, neg, w1)
    v2 = jnp.max(w2, axis=1, keepdims=True)
    i2 = jnp.min(jnp.where(w2 == v2, lane, LANES), axis=1, keepdims=True)
    e21 = jnp.exp(v2 - v1)
    gate1 = p_top / (1.0 + e21)
    gate2 = p_top * e21 / (1.0 + e21)
    idx_ref[...] = jnp.where(lane == 0, i1 - N_GROUPS, jnp.where(lane == 1, i2 - N_GROUPS, 0))
    gate_ref[...] = jnp.where(lane == 0, gate1, jnp.where(lane == 1, gate2, 0.0))


def _ln_router(x, mix, g, b, wr_hi, wr_lo, alpha):
    m, d = x.shape
    tm = _pick(m, 256, 8)
    rows = pl.BlockSpec((tm, d), lambda i: (i, 0))
    const = lambda shape: pl.BlockSpec(shape, lambda i: (0, 0))
    small = pl.BlockSpec((tm, LANES), lambda i: (i, 0))
    return pl.pallas_call(
        functools.partial(_ln_router_kernel, alpha=alpha),
        grid=(m // tm,),
        in_specs=[rows, rows, const((1, d)), const((1, d)), const((d, LANES)), const((d, LANES))],
        out_specs=[rows, small, small],
        out_shape=[jax.ShapeDtypeStruct((m, d), F32),
                   jax.ShapeDtypeStruct((m, LANES), jnp.int32), jax.ShapeDtypeStruct((m, LANES), F32)],
        compiler_params=_params("parallel"),
        name="ln_router",
    )(x, mix, g, b, wr_hi, wr_lo)


SLAB_TILE = 8


def _token_copy(src, src_row, dst, dst_row, sem):
    rows = lambda r: pl.ds(pl.multiple_of(r * SLAB_TILE, SLAB_TILE), SLAB_TILE)
    return pltpu.make_async_copy(src.at[:, rows(src_row), :], dst.at[:, rows(dst_row), :], sem)


def _dispatch_kernel(dest_ref, x_ref, xs_hbm, stage, sem, *, tm):
    for p in range(stage.shape[0]):
        for c in range(SLAB_TILE):
            col = (p * SLAB_TILE + c) * LANES
            stage[p, pl.ds(c, tm, stride=SLAB_TILE), :] = x_ref[:, col:col + LANES]

    def copy(i, k):
        return _token_copy(stage, i, xs_hbm, dest_ref[0, 0, 2 * i + k], sem)

    def start(i, _):
        copy(i, 0).start()
        copy(i, 1).start()
        return 0

    def wait(i, _):
        copy(i, 0).wait()
        copy(i, 1).wait()
        return 0

    lax.fori_loop(0, tm, start, 0)
    lax.fori_loop(0, tm, wait, 0)


def _dispatch(x1, dest, n_rows):
    m, d = x1.shape
    planes = d // (SLAB_TILE * LANES)
    tm = _pick(m, 256, 8)
    dest3 = dest.reshape(m // tm, 1, TOP_K * tm)
    return pl.pallas_call(
        functools.partial(_dispatch_kernel, tm=tm),
        grid=(m // tm,),
        in_specs=[pl.BlockSpec((1, 1, TOP_K * tm), lambda i: (i, 0, 0), memory_space=pltpu.SMEM),
                  pl.BlockSpec((tm, d), lambda i: (i, 0))],
        out_specs=pl.BlockSpec(memory_space=pl.ANY),
        out_shape=jax.ShapeDtypeStruct((planes, n_rows * SLAB_TILE, LANES), F32),
        scratch_shapes=[pltpu.VMEM((planes, tm * SLAB_TILE, LANES), F32), pltpu.SemaphoreType.DMA(())],
        compiler_params=pltpu.CompilerParams(dimension_semantics=("arbitrary",), has_side_effects=True,
                                             vmem_limit_bytes=VMEM_LIMIT),
        name="moe_dispatch",
    )(dest3, x1)


def _expert_kernel(ie_ref, in_ref, nu_ref, x_ref, wg_ref, wu_ref, wd_ref, y_ref, g_acc, u_acc, h_buf,
                   *, ks, sub, tiles):
    i = pl.program_id(0)
    s = pl.program_id(1)
    n_sub = (in_ref[i] + sub - 1) // sub
    live = i < nu_ref[0]
    per_blk = SLAB_TILE // tiles

    @pl.when(live & (s < ks))
    def _():
        wg = wg_ref[0].astype(BF16)
        wu = wu_ref[0].astype(BF16)
        c0 = (s % per_blk) * tiles

        def body(r, _):
            rows = pl.ds(pl.multiple_of(r * sub, sub), sub)
            first = r * (sub * SLAB_TILE) + c0
            x = jnp.concatenate([x_ref[pl.ds(first + c, sub, stride=SLAB_TILE), :] for c in range(tiles)],
                                axis=1).astype(BF16)
            dg = _dot(x, wg)
            du = _dot(x, wu)

            @pl.when(s == 0)
            def _():
                g_acc[rows, :] = dg
                u_acc[rows, :] = du

            @pl.when(s > 0)
            def _():
                g_acc[rows, :] += dg
                u_acc[rows, :] += du
            return 0

        lax.fori_loop(0, n_sub, body, 0)

    @pl.when(live & (s == ks))
    def _():
        def body(r, _):
            rows = pl.ds(pl.multiple_of(r * sub, sub), sub)
            g = g_acc[rows, :]
            h_buf[rows, :] = (g * _sigmoid(g) * u_acc[rows, :]).astype(BF16)
            return 0

        lax.fori_loop(0, n_sub, body, 0)

    @pl.when(live & (s >= ks))
    def _():
        wd = wd_ref[0].astype(BF16)
        c0 = ((s - ks) % per_blk) * tiles

        def body(r, _):
            rows = pl.ds(pl.multiple_of(r * sub, sub), sub)
            first = r * (sub * SLAB_TILE) + c0
            y = _dot(h_buf[rows, :], wd)
            for c in range(tiles):
                y_ref[pl.ds(first + c, sub, stride=SLAB_TILE), :] = y[:, c * LANES:(c + 1) * LANES]
            return 0

        lax.fori_loop(0, n_sub, body, 0)


def _experts(xs, item_e, item_n, n_used, w_gate, w_up, w_down, layer, n_items):
    d, hidden = w_gate.shape[-2:]
    n_exp = w_gate.shape[1]
    ks = MOE_K_STEPS
    kt = d // ks
    tiles = kt // LANES
    assert SLAB_TILE % tiles == 0
    per_blk = SLAB_TILE // tiles
    wg3 = w_gate.reshape(-1, d, hidden)
    wu3 = w_up.reshape(-1, d, hidden)
    wd3 = w_down.reshape(-1, hidden, d)
    e0 = layer * n_exp
    rows = MOE_ITEM_ROWS

    def item(i, nu):
        return jnp.minimum(i, nu[0] - 1)

    def step(i, s, nu):
        return jnp.where(i < nu[0], s, 2 * ks - 1)

    def up_step(i, s, nu):
        return jnp.minimum(step(i, s, nu), ks - 1)

    def down_step(i, s, nu):
        return jnp.maximum(step(i, s, nu) - ks, 0)

    grid_spec = pltpu.PrefetchScalarGridSpec(
        num_scalar_prefetch=3,
        grid=(n_items, 2 * ks),
        in_specs=[
            pl.BlockSpec((None, rows * SLAB_TILE, LANES),
                         lambda i, s, ie, inn, nu: (up_step(i, s, nu) // per_blk, item(i, nu), 0)),
            pl.BlockSpec((1, kt, hidden), lambda i, s, ie, inn, nu: (e0 + ie[item(i, nu)], up_step(i, s, nu), 0)),
            pl.BlockSpec((1, kt, hidden), lambda i, s, ie, inn, nu: (e0 + ie[item(i, nu)], up_step(i, s, nu), 0)),
            pl.BlockSpec((1, hidden, kt), lambda i, s, ie, inn, nu: (e0 + ie[item(i, nu)], 0, down_step(i, s, nu))),
        ],
        out_specs=pl.BlockSpec((None, rows * SLAB_TILE, LANES),
                               lambda i, s, ie, inn, nu: (down_step(i, s, nu) // per_blk, item(i, nu), 0)),
        scratch_shapes=[pltpu.VMEM((rows, hidden), F32), pltpu.VMEM((rows, hidden), F32),
                        pltpu.VMEM((rows, hidden), BF16)],
    )
    return pl.pallas_call(
        functools.partial(_expert_kernel, ks=ks, sub=MOE_SUB_ROWS, tiles=tiles),
        grid_spec=grid_spec,
        out_shape=jax.ShapeDtypeStruct(xs.shape, F32),
        compiler_params=_params("arbitrary", "arbitrary"),
        name="moe_experts",
    )(item_e, item_n, n_used, xs, wg3, wu3, wd3)


def _combine_kernel(dest_ref, x_ref, gate_ref, g_ref, b_ref, ys_hbm, y_ref, yb_ref, stage0, stage1, sem,
                    *, tm, alpha):
    stages = (stage0, stage1)

    def copy(i, k):
        return _token_copy(ys_hbm, dest_ref[0, 0, 2 * i + k], stages[k], i, sem)

    def start(i, _):
        copy(i, 0).start()
        copy(i, 1).start()
        return 0

    def wait(i, _):
        copy(i, 0).wait()
        copy(i, 1).wait()
        return 0

    lax.fori_loop(0, tm, start, 0)
    lax.fori_loop(0, tm, wait, 0)
    gates = gate_ref[...]
    g0 = gates[:, 0:1]
    g1 = gates[:, 1:2]
    tile = lambda stage, p, c: stage[p, pl.ds(c, tm, stride=SLAB_TILE), :]
    ffn = jnp.concatenate([g0 * tile(stage0, p, c) + g1 * tile(stage1, p, c)
                           for p in range(stage0.shape[0]) for c in range(SLAB_TILE)], axis=1)
    y = _layer_norm_rows(alpha * x_ref[...] + ffn, g_ref[...], b_ref[...])
    y_ref[...] = y
    yb_ref[...] = y.astype(BF16)


def _combine(x1, gates, dest, ys, g, b, alpha):
    m, d = x1.shape
    tm = _pick(m, 256, 8)
    stage = pltpu.VMEM((ys.shape[0], tm * SLAB_TILE, LANES), F32)
    dest3 = dest.reshape(m // tm, 1, TOP_K * tm)
    rows = pl.BlockSpec((tm, d), lambda i: (i, 0))
    const = pl.BlockSpec((1, d), lambda i: (0, 0))
    return pl.pallas_call(
        functools.partial(_combine_kernel, tm=tm, alpha=alpha),
        grid=(m // tm,),
        in_specs=[pl.BlockSpec((1, 1, TOP_K * tm), lambda i: (i, 0, 0), memory_space=pltpu.SMEM),
                  rows, pl.BlockSpec((tm, LANES), lambda i: (i, 0)), const, const, pl.BlockSpec(memory_space=pl.ANY)],
        out_specs=[rows, rows],
        out_shape=[jax.ShapeDtypeStruct((m, d), F32), jax.ShapeDtypeStruct((m, d), BF16)],
        scratch_shapes=[stage, stage, pltpu.SemaphoreType.DMA(())],
        compiler_params=_params("arbitrary"),
        name="moe_combine",
    )(dest3, x1, gates, g, b, ys)


def _route_plan(idx, n_items):
    e_flat = idx[:, :TOP_K].reshape(-1)
    onehot = (e_flat[:, None] == jnp.arange(N_EXPERTS, dtype=jnp.int32)[None, :]).astype(jnp.int32)
    csum = jnp.cumsum(onehot, axis=0)
    rank = jnp.sum(onehot * csum, axis=1) - 1
    counts = csum[-1]
    items_e = (counts + MOE_ITEM_ROWS - 1) // MOE_ITEM_ROWS
    item_end = jnp.cumsum(items_e)
    item_start = item_end - items_e
    dest = (item_start[e_flat] * MOE_ITEM_ROWS + rank).astype(jnp.int32)
    ids = jnp.arange(n_items, dtype=jnp.int32)
    item_e = jnp.minimum(jnp.searchsorted(item_end, ids, side="right"), N_EXPERTS - 1).astype(jnp.int32)
    n_used = item_end[-1].astype(jnp.int32)
    item_n = jnp.clip(counts[item_e] - (ids - item_start[item_e]) * MOE_ITEM_ROWS, 0, MOE_ITEM_ROWS)
    item_n = jnp.where(ids < n_used, item_n, 0).astype(jnp.int32)
    return dest, item_e, item_n, n_used.reshape(1)


def kernel(x_prompt, x_sample, cache_k, cache_v, page_table, state_gla, state_conv, rel_table,
           w_in, w_out, lam_q1, lam_k1, lam_q2, lam_k2, diff_norm_g, gla_w_up, gla_b, gla_norm_g,
           conv_w, cm_ln_g, cm_ln_b, cm_ws, cm_bs, ln1_g, ln1_b, ln2_g, ln2_b,
           router_group, router_expert, w_gate, w_up, w_down):
    bp, seq, d = x_prompt.shape
    bs, ts, _ = x_sample.shape
    depth = w_in.shape[0]
    gw = w_out.shape[1] // 4
    nh = gw // HEAD_DIM
    hk = nh * HALF_DIM
    page = cache_k.shape[2]
    n_phys = cache_k.shape[1]
    tp = SAMPLE_ROWS
    n_prompt = bp * seq
    n_all = n_prompt + bs * tp
    alpha = (2.0 * depth) ** 0.25
    assert ts <= tp and nh % 2 == 0 and n_prompt % tp == 0

    x = jnp.concatenate([x_prompt.reshape(n_prompt, d),
                         jnp.pad(x_sample, ((0, 0), (0, tp - ts), (0, 0))).reshape(bs * tp, d)], axis=0)
    xb = x.astype(BF16)

    bias_tiles = _prompt_bias_tiles(rel_table, ATTN_TILE)
    t_idx = jnp.arange(ts)
    k_idx = jnp.arange(page)
    dist_far = jnp.full((ts, page), 2 * page, jnp.int32)
    dist_last = page + t_idx[:, None] - k_idx[None, :]
    dist_new = t_idx[:, None] - k_idx[None, :]
    ok = jnp.stack([dist_far > 0, dist_last > 0, (dist_new >= 0) & (k_idx[None, :] < ts)])
    b3 = _rel_bias(jnp.stack([dist_far, dist_last, dist_new]), rel_table) * LOG2E
    same_head = jnp.eye(nh, dtype=bool)
    b3 = jnp.where(ok[:, None, :, :, None] & same_head[None, :, None, None, :],
                   jnp.transpose(b3, (0, 3, 1, 2))[..., None], MASK_VALUE)
    bias_s = jnp.tile(b3.reshape(3, nh * ts, page * nh), (1, 2, 1))
    half_mask = (jnp.arange(HEAD_DIM)[None, :] < HALF_DIM) == (jnp.arange(2)[:, None] == 0)

    cache_k4 = cache_k.reshape(depth, n_phys, page * nh, HEAD_DIM)
    cache_v4 = cache_v.reshape(depth, n_phys, page * nh, HEAD_DIM)
    n_items = (TOP_K * n_all) // MOE_ITEM_ROWS + N_EXPERTS
    tm_proj = _pick(n_all, 1024, 16)

    outs = {k: [] for k in ("kp", "vp", "ks", "vs", "gp", "gs", "cp", "cs", "ds")}
    for l in range(depth):
        lam_init = 0.8 - 0.6 * math.exp(-0.3 * l)
        lams = (lam_q1[l][None], lam_k1[l][None], lam_q2[l][None], lam_k2[l][None])
        gn_a = diff_norm_g[l][None]

        w_main = jnp.concatenate([w_in[l, :, :5 * gw], w_in[l, :, 5 * gw + GLA_RANK:]], axis=1).astype(BF16)
        w_gate_cols = jnp.pad(w_in[l, :, 5 * gw:5 * gw + GLA_RANK], ((0, 0), (0, LANES - GLA_RANK))).astype(BF16)
        proj = _matmul([xb], w_main, 11 * gw, tm_proj, _pick(11 * gw, 1024, LANES), "in_proj")
        gproj = _matmul([xb], w_gate_cols, LANES, tm_proj, LANES, "gate_proj")
        proj_s = proj[n_prompt:].reshape(bs, tp, 11 * gw)[:, :ts]

        oa_p = _attn_prompt(proj, lams, bias_tiles, gn_a, bp, seq, nh, lam_init)
        q_s = proj_s[..., :gw].reshape(bs, ts, nh, HEAD_DIM) * (HALF_DIM ** -0.5 * LOG2E)
        q2 = (jnp.transpose(q_s, (0, 2, 1, 3))[:, None] * half_mask.astype(F32)[None, :, None, None, :])
        q2 = q2.reshape(bs, 2 * nh * ts, HEAD_DIM).astype(BF16)
        k_s = proj_s[..., gw:2 * gw]
        v_s = proj_s[..., 2 * gw:3 * gw]
        new_rows = lambda a: jnp.pad(a.reshape(bs, ts * nh, HEAD_DIM), ((0, 0), (0, (page - ts) * nh), (0, 0)))
        oa_s = _attn_sample(l, q2, cache_k4, cache_v4, new_rows(k_s), new_rows(v_s), page_table, bias_s, lams, gn_a,
                            lam_init)
        oa_s = jnp.transpose(oa_s.reshape(bs, nh, ts, HEAD_DIM), (0, 2, 1, 3)).reshape(bs, ts, gw)
        oa_s = jnp.pad(oa_s, ((0, 0), (0, tp - ts), (0, 0))).reshape(bs * tp, gw).astype(BF16)

        wup = jnp.pad(gla_w_up[l], ((0, LANES - GLA_RANK), (0, 0)))
        wup_hi, wup_lo = _split_bf16(wup)
        gb = gla_b[l][None]
        gn_b = gla_norm_g[l][None]
        s0_s = jnp.transpose(state_gla[l].reshape(bs, nh // 2, 2 * HALF_DIM, HEAD_DIM), (0, 1, 3, 2))
        s0_p = jnp.zeros((bp, nh // 2, HEAD_DIM, LANES), F32)
        ob_p, sf_p = _gla(proj, gproj, wup_hi, wup_lo, gb, gn_b, s0_p, 0, bp, seq, nh, seq)
        ob_s, sf_s = _gla(proj, gproj, wup_hi, wup_lo, gb, gn_b, s0_s, n_prompt, bs, tp, nh, ts)
        unpair = lambda s: jnp.transpose(s, (0, 1, 3, 2)).reshape(s.shape[0], nh, HALF_DIM, HEAD_DIM)

        bs_t = jnp.transpose(cm_bs[l])
        cw, lg, lb = conv_w[l], cm_ln_g[l][None], cm_ln_b[l][None]
        oc_p, od_p, cs_p, _ = _conv_mlp(proj, cw, jnp.zeros((bp, CONV_WIDTH - 1, gw), F32), lg, lb, cm_ws[l], bs_t,
                                        0, bp, seq, nh, seq)
        oc_s, od_s, cs_s, vd_s = _conv_mlp(proj, cw, state_conv[l], lg, lb, cm_ws[l], bs_t, n_prompt, bs, tp, nh, ts)

        cat = lambda a, b: jnp.concatenate([a, b], axis=0)
        mix = _matmul([cat(oa_p, oa_s), cat(ob_p, ob_s), cat(oc_p, oc_s), cat(od_p, od_s)], w_out[l].astype(BF16),
                      d, tm_proj, _pick(d, 1024, LANES), "out_proj")
        wr = jnp.pad(jnp.concatenate([router_group[l], router_expert[l]], axis=1),
                     ((0, 0), (0, LANES - N_GROUPS - N_EXPERTS)))
        wr_hi, wr_lo = _split_bf16(wr)
        x1, ridx, rgate = _ln_router(x, mix, ln1_g[l][None], ln1_b[l][None], wr_hi, wr_lo, alpha)

        dest, item_e, item_n, n_used = _route_plan(ridx, n_items)
        xs = _dispatch(x1, dest, n_items * MOE_ITEM_ROWS)
        ys = _experts(xs, item_e, item_n, n_used, w_gate, w_up, w_down, l, n_items)
        x, xb = _combine(x1, rgate, dest, ys, ln2_g[l][None], ln2_b[l][None], alpha)

        heads = lambda a, n, t: a.reshape(n, t, nh, HEAD_DIM)
        outs["kp"].append(heads(proj[:n_prompt, gw:2 * gw], bp, seq))
        outs["vp"].append(heads(proj[:n_prompt, 2 * gw:3 * gw], bp, seq))
        outs["ks"].append(heads(k_s, bs, ts))
        outs["vs"].append(heads(v_s, bs, ts))
        outs["gp"].append(unpair(sf_p))
        outs["gs"].append(unpair(sf_s))
        outs["cp"].append(cs_p)
        outs["cs"].append(cs_s)
        outs["ds"].append(vd_s.reshape(bs, tp, gw)[:, :ts])

    y_prompt = x[:n_prompt].reshape(bp, seq, d)
    y_sample = x[n_prompt:].reshape(bs, tp, d)[:, :ts]
    st = lambda k: jnp.stack(outs[k])
    return (y_prompt, y_sample, st("kp"), st("vp"), st("ks"), st("vs"), st("gp"), st("gs"),
            st("cp"), st("cs"), st("ds"))
```

```python
import functools
import math

import jax
import jax.numpy as jnp
from jax import lax
from jax.experimental import pallas as pl
from jax.experimental.pallas import tpu as pltpu

F32 = jnp.float32
BF16 = jnp.bfloat16

HEAD_DIM = 128
HALF_DIM = HEAD_DIM // 2
GLA_RANK = 16
GLA_TAU = 16.0
GLA_CHUNK = 64
CONV_WIDTH = 3
CHUNK_D = 128
REL_BUCKETS = 32
REL_MAX_DIST = 128
N_GROUPS = 4
EXPERTS_PER_GROUP = 8
N_EXPERTS = N_GROUPS * EXPERTS_PER_GROUP
TOP_K = 2
NORM_EPS = 1e-5
MASK_VALUE = -1e30
LOG2E = 1.4426950408889634

LANES = 128
SAMPLE_ROWS = 64
ATTN_TILE = 256
MOE_ITEM_ROWS = 768
MOE_SUB_ROWS = 256
MOE_HIDDEN_TILE = 256
MOE_DOWN_STEPS = 8
VMEM_LIMIT = 52 * 1024 * 1024


def _pick(n, target, mult):
    if n <= target:
        return n
    best = None
    for d in range(mult, target + 1, mult):
        if n % d == 0:
            best = d
    assert best is not None, (n, target, mult)
    return best


def _params(*sem):
    return pltpu.CompilerParams(dimension_semantics=sem, vmem_limit_bytes=VMEM_LIMIT)


def _dot(a, b):
    return jnp.dot(a, b, preferred_element_type=F32)


def _dot_nt(a, b):
    return lax.dot_general(a, b, (((1,), (1,)), ((), ())), preferred_element_type=F32)


def _dot_tn(a, b):
    return lax.dot_general(a, b, (((0,), (0,)), ((), ())), preferred_element_type=F32)


def _split_bf16(x):
    hi = x.astype(BF16)
    lo = (x - hi.astype(F32)).astype(BF16)
    return hi, lo


def _sigmoid(x):
    return 1.0 / (1.0 + jnp.exp(-x))


def _mm_kernel(*refs, n_in):
    o_ref = refs[2 * n_in]
    acc = None
    for x_ref, w_ref in zip(refs[:n_in], refs[n_in:2 * n_in]):
        d = _dot(x_ref[...], w_ref[...])
        acc = d if acc is None else acc + d
    o_ref[...] = acc.astype(o_ref.dtype)


def _matmul(xs, w, n_out, tm, tn, name):
    n_in = len(xs)
    m, kg = xs[0].shape
    grid = (m // tm, n_out // tn)
    in_specs = [pl.BlockSpec((tm, kg), lambda i, j: (i, 0)) for _ in xs]
    in_specs += [pl.BlockSpec((kg, tn), functools.partial(lambda i, j, g: (g, j), g=g)) for g in range(n_in)]
    return pl.pallas_call(
        functools.partial(_mm_kernel, n_in=n_in),
        grid=grid,
        in_specs=in_specs,
        out_specs=pl.BlockSpec((tm, tn), lambda i, j: (i, j)),
        out_shape=jax.ShapeDtypeStruct((m, n_out), F32),
        compiler_params=_params("parallel", "arbitrary"),
        name=name,
    )(*xs, *([w] * n_in))


def _rel_bias(dist, rel_table):
    n = jnp.maximum(dist, 0)
    max_exact = REL_BUCKETS // 2
    nf = jnp.maximum(n, 1).astype(F32)
    large = max_exact + (jnp.log(nf / max_exact) / math.log(REL_MAX_DIST / max_exact)
                         * (REL_BUCKETS - max_exact)).astype(jnp.int32)
    bucket = jnp.where(n < max_exact, n, jnp.minimum(large, REL_BUCKETS - 1))
    onehot = (bucket[..., None] == jnp.arange(REL_BUCKETS, dtype=bucket.dtype)).astype(F32)
    return jnp.einsum("...k,kh->...h", onehot, rel_table.astype(F32), precision=lax.Precision.HIGHEST)


def _prompt_bias_tiles(rel_table, t):
    r = jnp.arange(t)[:, None]
    c = jnp.arange(t)[None, :]
    tiles = []
    for delta in range(3):
        dist = delta * t + r - c
        b = _rel_bias(dist, rel_table) * LOG2E
        tiles.append(jnp.where((dist >= 0)[..., None], b, MASK_VALUE))
    return jnp.transpose(jnp.stack(tiles), (3, 0, 1, 2))


def _lam_value(lq1, lk1, lq2, lk2, lam_init):
    return (jnp.exp(jnp.sum(lq1[...] * lk1[...], axis=1, keepdims=True))
            - jnp.exp(jnp.sum(lq2[...] * lk2[...], axis=1, keepdims=True)) + lam_init)


def _rms_head(o, gain, post):
    return o * lax.rsqrt(jnp.mean(o * o, axis=1, keepdims=True) + NORM_EPS) * gain * post


def _attn_prompt_kernel(lq1, lk1, lq2, lk2, q_ref, k_ref, v_ref, bias_ref, gn_ref, o_ref, *, t, lam_init):
    qi = pl.program_id(2)
    lam = _lam_value(lq1, lk1, lq2, lk2, lam_init)
    lane = lax.broadcasted_iota(jnp.int32, (t, HEAD_DIM), 1)
    heads = []
    for h in range(2):
        q = q_ref[:, h * HEAD_DIM:(h + 1) * HEAD_DIM] * (HALF_DIM ** -0.5 * LOG2E)
        heads.append(jnp.concatenate([jnp.where(lane < HALF_DIM, q, 0.0), jnp.where(lane >= HALF_DIM, q, 0.0)],
                                     axis=0).astype(BF16))

    def body(j, carry):
        start = pl.multiple_of(j * t, t)
        tile = jnp.minimum(qi - j, 2)
        out = []
        for h in range(2):
            m, s, acc = carry[h]
            cols = slice(h * HEAD_DIM, (h + 1) * HEAD_DIM)
            kt = k_ref[pl.ds(start, t), cols].astype(BF16)
            vt = v_ref[pl.ds(start, t), cols].astype(BF16)
            bt = bias_ref[h, tile]
            l = _dot_nt(heads[h], kt) + jnp.concatenate([bt, bt], axis=0)
            m_new = jnp.maximum(m, jnp.max(l, axis=1, keepdims=True))
            a = jnp.exp2(m - m_new)
            p = jnp.exp2(l - m_new)
            out.append((m_new, a * s + jnp.sum(p, axis=1, keepdims=True), a * acc + _dot(p.astype(BF16), vt)))
        return tuple(out)

    init = (jnp.full((2 * t, 1), MASK_VALUE, F32), jnp.zeros((2 * t, 1), F32), jnp.zeros((2 * t, HEAD_DIM), F32))
    res = lax.fori_loop(0, qi + 1, body, (init, init))
    for h in range(2):
        _, s, acc = res[h]
        o = acc / s
        o = o[:t] - lam * o[t:]
        o_ref[:, h * HEAD_DIM:(h + 1) * HEAD_DIM] = _rms_head(o, gn_ref[...], 1.0 - lam_init).astype(o_ref.dtype)


def _attn_prompt(proj, lams, bias_tiles, gn, n_seq, seq, n_heads, lam_init):
    t = ATTN_TILE
    assert seq % t == 0 and t >= REL_MAX_DIST and n_heads % 2 == 0
    nq = seq // t
    pair = 2 * HEAD_DIM
    lam_spec = pl.BlockSpec((1, HALF_DIM), lambda b, h, i: (0, 0))
    return pl.pallas_call(
        functools.partial(_attn_prompt_kernel, t=t, lam_init=lam_init),
        grid=(n_seq, n_heads // 2, nq),
        in_specs=[lam_spec] * 4 + [
            pl.BlockSpec((t, pair), lambda b, h, i: (b * nq + i, h)),
            pl.BlockSpec((seq, pair), lambda b, h, i: (b, n_heads // 2 + h)),
            pl.BlockSpec((seq, pair), lambda b, h, i: (b, n_heads + h)),
            pl.BlockSpec((2, 3, t, t), lambda b, h, i: (h, 0, 0, 0)),
            pl.BlockSpec((1, HEAD_DIM), lambda b, h, i: (0, 0)),
        ],
        out_specs=pl.BlockSpec((t, pair), lambda b, h, i: (b * nq + i, h)),
        out_shape=jax.ShapeDtypeStruct((n_seq * seq, n_heads * HEAD_DIM), BF16),
        compiler_params=_params("parallel", "parallel", "arbitrary"),
        name="attn_prompt",
    )(*lams, proj, proj, proj, bias_tiles, gn)


def _attn_sample_kernel(pt_ref, lq1, lk1, lq2, lk2, q_ref, *refs, n_pp, lam_init):
    kc_refs, vc_refs = refs[:n_pp], refs[n_pp:2 * n_pp]
    kn_ref, vn_ref, bias_ref, gn_ref, o_ref, m_ref, s_ref, acc_ref = refs[2 * n_pp:]
    p = pl.program_id(1)
    n_steps = pl.num_programs(1)

    @pl.when(p == 0)
    def _():
        m_ref[...] = jnp.full(m_ref.shape, MASK_VALUE, F32)
        s_ref[...] = jnp.zeros(s_ref.shape, F32)
        acc_ref[...] = jnp.zeros(acc_ref.shape, F32)

    q = q_ref[0]

    def step(k, v, bias):
        l = _dot_nt(q, k.astype(BF16)) + bias
        m = m_ref[...]
        m_new = jnp.maximum(m, jnp.max(l, axis=1, keepdims=True))
        a = jnp.exp2(m - m_new)
        pr = jnp.exp2(l - m_new)
        s_ref[...] = a * s_ref[...] + jnp.sum(pr, axis=1, keepdims=True)
        acc_ref[...] = a * acc_ref[...] + _dot(pr.astype(BF16), v.astype(BF16))
        m_ref[...] = m_new

    last = p == n_steps - 1
    for i in range(n_pp):
        table = jnp.where(last, 1, 0) if i == n_pp - 1 else 0
        step(kc_refs[i][0, 0], vc_refs[i][0, 0], bias_ref[table])

    @pl.when(last)
    def _():
        step(kn_ref[0], vn_ref[0], bias_ref[2])
        lam = _lam_value(lq1, lk1, lq2, lk2, lam_init)
        o = acc_ref[...] / s_ref[...]
        half = o.shape[0] // 2
        o_ref[0] = _rms_head(o[:half] - lam * o[half:], gn_ref[...], 1.0 - lam_init)


def _attn_sample(layer, q2, cache_k, cache_v, k_new, v_new, page_table, bias, lams, gn, lam_init):
    n_seq, n_pages = page_table.shape
    prow = cache_k.shape[2]
    rows = q2.shape[1]
    n_pp = 4 if n_pages % 4 == 0 else 1
    lam_spec = pl.BlockSpec((1, HALF_DIM), lambda b, p, pt: (0, 0))
    cache_specs = [pl.BlockSpec((1, 1, prow, HEAD_DIM),
                                functools.partial(lambda b, p, pt, i: (layer, pt[b, p * n_pp + i], 0, 0), i=i))
                   for i in range(n_pp)]
    new_spec = pl.BlockSpec((1, prow, HEAD_DIM), lambda b, p, pt: (b, 0, 0))
    grid_spec = pltpu.PrefetchScalarGridSpec(
        num_scalar_prefetch=1,
        grid=(n_seq, n_pages // n_pp),
        in_specs=[lam_spec] * 4 + [pl.BlockSpec((1, rows, HEAD_DIM), lambda b, p, pt: (b, 0, 0))]
        + cache_specs + cache_specs + [
            new_spec, new_spec,
            pl.BlockSpec((3, rows, prow), lambda b, p, pt: (0, 0, 0)),
            pl.BlockSpec((1, HEAD_DIM), lambda b, p, pt: (0, 0)),
        ],
        out_specs=pl.BlockSpec((1, rows // 2, HEAD_DIM), lambda b, p, pt: (b, 0, 0)),
        scratch_shapes=[pltpu.VMEM((rows, 1), F32), pltpu.VMEM((rows, 1), F32), pltpu.VMEM((rows, HEAD_DIM), F32)],
    )
    return pl.pallas_call(
        functools.partial(_attn_sample_kernel, n_pp=n_pp, lam_init=lam_init),
        grid_spec=grid_spec,
        out_shape=jax.ShapeDtypeStruct((n_seq, rows // 2, HEAD_DIM), F32),
        compiler_params=_params("parallel", "arbitrary"),
        name="attn_sample",
    )(page_table, *lams, q2, *([cache_k] * n_pp), *([cache_v] * n_pp), k_new, v_new, bias, gn)


def _gla_kernel(q_ref, k_ref, v_ref, r_ref, gp_ref, wh_ref, wl_ref, gb_ref, gn_ref, s0_ref, o_ref, sfin_ref, st_ref,
                *, chunk, n_chunks, n_heads, valid):
    t = pl.program_id(1)

    @pl.when(t == 0)
    def _():
        st_ref[...] = s0_ref[0]

    row = lax.broadcasted_iota(jnp.int32, (chunk, chunk), 0)
    col = lax.broadcasted_iota(jnp.int32, (chunk, chunk), 1)
    tril = row >= col
    tri_bf = jnp.where(tril, 1.0, 0.0).astype(BF16)
    lane = lax.broadcasted_iota(jnp.int32, (chunk, LANES), 1)
    lane_sq = lax.broadcasted_iota(jnp.int32, (HEAD_DIM, LANES), 1)
    mid = chunk // 2 - 1

    for c in range(n_chunks):
        rs = slice(c * chunk, (c + 1) * chunk)
        gh, gl = _split_bf16(gp_ref[rs, :])
        z = _dot(gh, wh_ref[...]) + _dot(gl, wh_ref[...]) + _dot(gh, wl_ref[...]) + gb_ref[...]
        g = -(jnp.maximum(-z, 0.0) + jnp.log(1.0 + jnp.exp(-jnp.abs(z)))) * (1.0 / GLA_TAU)
        k = k_ref[rs, :]
        if valid < chunk:
            ok = lax.broadcasted_iota(jnp.int32, g.shape, 0) < valid
            g = jnp.where(ok, g, 0.0)
            k = jnp.where(ok, k, 0.0)
        gh, gl = _split_bf16(g)
        b = _dot(tri_bf, gh) + _dot(tri_bf, gl)
        b_last = b[chunk - 1:chunk, :]
        b_mid = b[mid:mid + 1, :]
        q = q_ref[rs, :] * (HALF_DIM ** -0.5)
        q_state = q * jnp.exp(b)
        q_mid = q * jnp.exp(b - b_mid)
        k_mid = (k * jnp.exp(b_mid - b)).astype(BF16)
        k_end = (k * jnp.exp(b_last - b)).astype(BF16)
        decay = jnp.exp(b_last)

        for hp in range(n_heads // 2):
            ls = slice(hp * LANES, (hp + 1) * LANES)
            st = st_ref[hp]
            st_bf = st.astype(BF16)
            upd = []
            for par in range(2):
                h = 2 * hp + par
                mine = (lane < HALF_DIM) if par == 0 else (lane >= HALF_DIM)
                hs = slice(h * HEAD_DIM, (h + 1) * HEAD_DIM)
                v_h = v_ref[rs, hs].astype(BF16)
                att = _dot_nt(jnp.where(mine, q_mid[:, ls], 0.0).astype(BF16), k_mid[:, ls])
                att = jnp.where(tril, att, 0.0).astype(BF16)
                o = _dot(att, v_h) + _dot_nt(jnp.where(mine, q_state[:, ls], 0.0).astype(BF16), st_bf)
                r_h = r_ref[rs, hs]
                o_ref[rs, hs] = _rms_head(o, gn_ref[...], r_h * _sigmoid(r_h)).astype(o_ref.dtype)
                upd.append(_dot_tn(v_h, k_end[:, ls]))
            st_ref[hp] = st * decay[:, ls] + jnp.where(lane_sq < HALF_DIM, upd[0], upd[1])

    @pl.when(t == pl.num_programs(1) - 1)
    def _():
        sfin_ref[0] = st_ref[...]


def _gla(proj, gproj, wup_hi, wup_lo, gla_b, gn, s0, row0, n_seq, seq, n_heads, valid):
    gw = n_heads * HEAD_DIM
    hk = n_heads * HALF_DIM
    chunk = min(GLA_CHUNK, seq)
    tt = _pick(seq, 256, chunk)
    nt = seq // tt
    r0 = row0 // tt
    assert row0 % tt == 0
    qcol = 3 * gw // hk
    row_spec = lambda width, colblk: pl.BlockSpec((tt, width), lambda b, t: (r0 + b * nt + t, colblk))
    const = lambda shape: pl.BlockSpec(shape, lambda b, t: tuple(0 for _ in shape))
    st_spec = pl.BlockSpec((1, n_heads // 2, HEAD_DIM, LANES), lambda b, t: (b, 0, 0, 0))
    return pl.pallas_call(
        functools.partial(_gla_kernel, chunk=chunk, n_chunks=tt // chunk, n_heads=n_heads, valid=valid),
        grid=(n_seq, nt),
        in_specs=[row_spec(hk, qcol), row_spec(hk, qcol + 1), row_spec(gw, 4), row_spec(gw, 5),
                  pl.BlockSpec((tt, LANES), lambda b, t: (r0 + b * nt + t, 0)),
                  const((LANES, hk)), const((LANES, hk)), const((1, hk)), const((1, HEAD_DIM)), st_spec],
        out_specs=[pl.BlockSpec((tt, gw), lambda b, t: (b * nt + t, 0)), st_spec],
        out_shape=[jax.ShapeDtypeStruct((n_seq * seq, gw), BF16),
                   jax.ShapeDtypeStruct((n_seq, n_heads // 2, HEAD_DIM, LANES), F32)],
        scratch_shapes=[pltpu.VMEM((n_heads // 2, HEAD_DIM, LANES), F32)],
        compiler_params=_params("parallel", "arbitrary"),
        name="gla",
    )(proj, proj, proj, proj, gproj, wup_hi, wup_lo, gla_b, gn, s0)


def _conv_mlp_kernel(cb_ref, cc_ref, ch_ref, du_ref, dv_ref, cw_ref, buf_ref, lg_ref, lb_ref, ws_ref, bs_ref,
                     oc_ref, od_ref, cs_ref, vd_ref, carry_ref, *, tt, chunk, n_heads, valid):
    t = pl.program_id(1)

    @pl.when(t == 0)
    def _():
        carry_ref[...] = buf_ref[0]

    z = cc_ref[...] * ch_ref[...]
    row = lax.broadcasted_iota(jnp.int32, z.shape, 0)
    c0 = carry_ref[0:1, :]
    c1 = carry_ref[1:2, :]
    z1 = jnp.where(row == 0, c1, pltpu.roll(z, 1, 0))
    z2 = jnp.where(row == 0, c0, jnp.where(row == 1, c1, pltpu.roll(z, 2, 0)))
    w = cw_ref[...]
    oc_ref[...] = (cb_ref[...] * (z2 * w[0:1, :] + z1 * w[1:2, :] + z * w[2:3, :])).astype(oc_ref.dtype)
    if valid >= 2:
        carry_ref[...] = z[valid - 2:valid, :]
    else:
        carry_ref[...] = jnp.concatenate([c1, z[0:1, :]], axis=0)

    @pl.when(t == pl.num_programs(1) - 1)
    def _():
        cs_ref[0] = carry_ref[...]

    x = dv_ref[...]
    mu = jnp.mean(x, axis=1, keepdims=True)
    xc = x - mu
    var = jnp.mean(xc * xc, axis=1, keepdims=True)
    vd = xc * lax.rsqrt(var + NORM_EPS) * lg_ref[...] + lb_ref[...]
    vd_ref[...] = vd
    vd_bf = vd.astype(BF16)
    r2 = lax.broadcasted_iota(jnp.int32, (chunk, chunk), 0)
    c2 = lax.broadcasted_iota(jnp.int32, (chunk, chunk), 1)
    for g in range(n_heads):
        wc = jnp.where(r2 >= c2, ws_ref[g, :chunk, :chunk], 0.0).astype(BF16)
        bias = bs_ref[:chunk, g:g + 1]
        cols = slice(g * HEAD_DIM, (g + 1) * HEAD_DIM)
        for c in range(tt // chunk):
            rs = slice(c * chunk, (c + 1) * chunk)
            zc = _dot(wc, vd_bf[rs, cols]) + bias
            od_ref[rs, cols] = (du_ref[rs, cols] * zc).astype(od_ref.dtype)


def _conv_mlp(proj, conv_w, buf, ln_g, ln_b, ws, bs_t, row0, n_seq, seq, n_heads, valid):
    gw = n_heads * HEAD_DIM
    chunk = min(CHUNK_D, seq)
    tt = _pick(seq, 256, chunk)
    nt = seq // tt
    r0 = row0 // tt
    assert row0 % tt == 0 and (valid == seq or nt == 1)
    row_spec = lambda colblk: pl.BlockSpec((tt, gw), lambda b, t: (r0 + b * nt + t, colblk))
    const = lambda shape: pl.BlockSpec(shape, lambda b, t: tuple(0 for _ in shape))
    out_rows = pl.BlockSpec((tt, gw), lambda b, t: (b * nt + t, 0))
    state_spec = pl.BlockSpec((1, CONV_WIDTH - 1, gw), lambda b, t: (b, 0, 0))
    return pl.pallas_call(
        functools.partial(_conv_mlp_kernel, tt=tt, chunk=chunk, n_heads=n_heads, valid=min(valid, tt)),
        grid=(n_seq, nt),
        in_specs=[row_spec(6), row_spec(7), row_spec(8), row_spec(9), row_spec(10),
                  const((CONV_WIDTH, gw)), state_spec, const((1, gw)), const((1, gw)),
                  const((n_heads, CHUNK_D, CHUNK_D)), const((CHUNK_D, n_heads))],
        out_specs=[out_rows, out_rows, state_spec, out_rows],
        out_shape=[jax.ShapeDtypeStruct((n_seq * seq, gw), BF16), jax.ShapeDtypeStruct((n_seq * seq, gw), BF16),
                   jax.ShapeDtypeStruct((n_seq, CONV_WIDTH - 1, gw), F32),
                   jax.ShapeDtypeStruct((n_seq * seq, gw), F32)],
        scratch_shapes=[pltpu.VMEM((CONV_WIDTH - 1, gw), F32)],
        compiler_params=_params("parallel", "arbitrary"),
        name="conv_mlp",
    )(proj, proj, proj, proj, proj, conv_w, buf, ln_g, ln_b, ws, bs_t)


def _layer_norm_rows(x, g, b):
    mu = jnp.mean(x, axis=1, keepdims=True)
    xc = x - mu
    var = jnp.mean(xc * xc, axis=1, keepdims=True)
    return xc * lax.rsqrt(var + NORM_EPS) * g + b


def _ln_router_kernel(x_ref, mix_ref, g_ref, b_ref, wh_ref, wl_ref, y_ref, idx_ref, gate_ref, *, alpha):
    y = _layer_norm_rows(alpha * x_ref[...] + mix_ref[...], g_ref[...], b_ref[...])
    y_ref[...] = y
    yh, yl = _split_bf16(y)
    lg = _dot(yh, wh_ref[...]) + _dot(yl, wh_ref[...]) + _dot(yh, wl_ref[...])
    lane = lax.broadcasted_iota(jnp.int32, lg.shape, 1)
    neg = -jnp.inf
    glog = jnp.where(lane < N_GROUPS, lg, neg)
    gmax = jnp.max(glog, axis=1, keepdims=True)
    g_star = jnp.min(jnp.where(glog == gmax, lane, LANES), axis=1, keepdims=True)
    p_top = 1.0 / jnp.sum(jnp.exp(glog - gmax), axis=1, keepdims=True)
    lo = N_GROUPS + EXPERTS_PER_GROUP * g_star
    w1 = jnp.where((lane >= lo) & (lane < lo + EXPERTS_PER_GROUP), lg, neg)
    v1 = jnp.max(w1, axis=1, keepdims=True)
    i1 = jnp.min(jnp.where(w1 == v1, lane, LANES), axis=1, keepdims=True)
    w2 = jnp.where(lane == i1, neg, w1)
    v2 = jnp.max(w2, axis=1, keepdims=True)
    i2 = jnp.min(jnp.where(w2 == v2, lane, LANES), axis=1, keepdims=True)
    e21 = jnp.exp(v2 - v1)
    gate1 = p_top / (1.0 + e21)
    gate2 = p_top * e21 / (1.0 + e21)
    idx_ref[...] = jnp.where(lane == 0, i1 - N_GROUPS, jnp.where(lane == 1, i2 - N_GROUPS, 0))
    gate_ref[...] = jnp.where(lane == 0, gate1, jnp.where(lane == 1, gate2, 0.0))


def _ln_router(x, mix, g, b, wr_hi, wr_lo, alpha):
    m, d = x.shape
    tm = _pick(m, 256, 8)
    rows = pl.BlockSpec((tm, d), lambda i: (i, 0))
    const = lambda shape: pl.BlockSpec(shape, lambda i: (0, 0))
    small = pl.BlockSpec((tm, LANES), lambda i: (i, 0))
    return pl.pallas_call(
        functools.partial(_ln_router_kernel, alpha=alpha),
        grid=(m // tm,),
        in_specs=[rows, rows, const((1, d)), const((1, d)), const((d, LANES)), const((d, LANES))],
        out_specs=[rows, small, small],
        out_shape=[jax.ShapeDtypeStruct((m, d), F32),
                   jax.ShapeDtypeStruct((m, LANES), jnp.int32), jax.ShapeDtypeStruct((m, LANES), F32)],
        compiler_params=_params("parallel"),
        name="ln_router",
    )(x, mix, g, b, wr_hi, wr_lo)


SLAB_TILE = 8


def _token_copy(src, src_row, dst, dst_row, sem):
    rows = lambda r: pl.ds(pl.multiple_of(r * SLAB_TILE, SLAB_TILE), SLAB_TILE)
    return pltpu.make_async_copy(src.at[:, rows(src_row), :], dst.at[:, rows(dst_row), :], sem)


def _dispatch_kernel(dest_ref, x_ref, xs_hbm, stage, sem, *, tm):
    for p in range(stage.shape[0]):
        for c in range(SLAB_TILE):
            col = (p * SLAB_TILE + c) * LANES
            stage[p, pl.ds(c, tm, stride=SLAB_TILE), :] = x_ref[:, col:col + LANES]

    def copy(i, k):
        return _token_copy(stage, i, xs_hbm, dest_ref[0, 0, 2 * i + k], sem)

    def start(i, _):
        copy(i, 0).start()
        copy(i, 1).start()
        return 0

    def wait(i, _):
        copy(i, 0).wait()
        copy(i, 1).wait()
        return 0

    lax.fori_loop(0, tm, start, 0)
    lax.fori_loop(0, tm, wait, 0)


def _dispatch(x1, dest, n_rows):
    m, d = x1.shape
    planes = d // (SLAB_TILE * LANES)
    tm = _pick(m, 256, 8)
    dest3 = dest.reshape(m // tm, 1, TOP_K * tm)
    return pl.pallas_call(
        functools.partial(_dispatch_kernel, tm=tm),
        grid=(m // tm,),
        in_specs=[pl.BlockSpec((1, 1, TOP_K * tm), lambda i: (i, 0, 0), memory_space=pltpu.SMEM),
                  pl.BlockSpec((tm, d), lambda i: (i, 0))],
        out_specs=pl.BlockSpec(memory_space=pl.ANY),
        out_shape=jax.ShapeDtypeStruct((planes, n_rows * SLAB_TILE, LANES), F32),
        scratch_shapes=[pltpu.VMEM((planes, tm * SLAB_TILE, LANES), F32), pltpu.SemaphoreType.DMA(())],
        compiler_params=pltpu.CompilerParams(dimension_semantics=("arbitrary",), has_side_effects=True,
                                             vmem_limit_bytes=VMEM_LIMIT),
        name="moe_dispatch",
    )(dest3, x1)


def _expert_kernel(ie_ref, in_ref, nu_ref, x_ref, wg_ref, wu_ref, wd_ref, y_ref, xb, h_buf, *, planes, hs, sub, tiles):
    i = pl.program_id(0)
    s = pl.program_id(1)
    n_sub = (in_ref[i] + sub - 1) // sub
    live = i < nu_ref[0]
    per_blk = SLAB_TILE // tiles
    up0 = planes
    down0 = planes + hs

    def each_sub(fn):
        def body(r, _):
            fn(r * (sub * SLAB_TILE), pl.ds(pl.multiple_of(r * sub, sub), sub))
            return 0
        lax.fori_loop(0, n_sub, body, 0)

    @pl.when(live & (s < up0))
    def _():
        def convert(first, rows):
            xb[s, rows, :] = jnp.concatenate([x_ref[pl.ds(first + c, sub, stride=SLAB_TILE), :]
                                              for c in range(SLAB_TILE)], axis=1).astype(BF16)
        each_sub(convert)

    @pl.when(live & (s >= up0) & (s < down0))
    def _():
        wg = wg_ref[0].astype(BF16)
        wu = wu_ref[0].astype(BF16)

        def up(first, rows):
            x = jnp.concatenate([xb[p, rows, :] for p in range(planes)], axis=1)
            g = _dot(x, wg)
            h_buf[s - up0, rows, :] = (g * _sigmoid(g) * _dot(x, wu)).astype(BF16)
        each_sub(up)

    @pl.when(live & (s >= down0))
    def _():
        wd = wd_ref[0].astype(BF16)
        c0 = ((s - down0) % per_blk) * tiles

        def down(first, rows):
            y = _dot(jnp.concatenate([h_buf[j, rows, :] for j in range(hs)], axis=1), wd)
            for c in range(tiles):
                y_ref[pl.ds(first + c0 + c, sub, stride=SLAB_TILE), :] = y[:, c * LANES:(c + 1) * LANES]
        each_sub(down)


def _experts(xs, item_e, item_n, n_used, w_gate, w_up, w_down, layer, n_items):
    d, hidden = w_gate.shape[-2:]
    n_exp = w_gate.shape[1]
    planes = xs.shape[0]
    ht = min(MOE_HIDDEN_TILE, hidden)
    hs = hidden // ht
    ns = MOE_DOWN_STEPS
    nt = d // ns
    tiles = nt // LANES
    assert SLAB_TILE % tiles == 0 and hidden % ht == 0
    per_blk = SLAB_TILE // tiles
    n_steps = planes + hs + ns
    wg3 = w_gate.reshape(-1, d, hidden)
    wu3 = w_up.reshape(-1, d, hidden)
    wd3 = w_down.reshape(-1, hidden, d)
    e0 = layer * n_exp
    rows = MOE_ITEM_ROWS

    def item(i, nu):
        return jnp.minimum(i, nu[0] - 1)

    def step(i, s, nu):
        return jnp.where(i < nu[0], s, n_steps - 1)

    def expert(i, ie, nu):
        return e0 + ie[item(i, nu)]

    def x_plane(i, s, ie, inn, nu):
        return jnp.minimum(step(i, s, nu), planes - 1), item(i, nu), 0

    def up_slice(i, s, ie, inn, nu):
        return expert(i, ie, nu), 0, jnp.clip(step(i, s, nu) - planes, 0, hs - 1)

    def down_slice(i, s, ie, inn, nu):
        return expert(i, ie, nu), 0, jnp.maximum(step(i, s, nu) - planes - hs, 0)

    def y_plane(i, s, ie, inn, nu):
        return jnp.maximum(step(i, s, nu) - planes - hs, 0) // per_blk, item(i, nu), 0

    grid_spec = pltpu.PrefetchScalarGridSpec(
        num_scalar_prefetch=3,
        grid=(n_items, n_steps),
        in_specs=[
            pl.BlockSpec((None, rows * SLAB_TILE, LANES), x_plane),
            pl.BlockSpec((1, d, ht), up_slice),
            pl.BlockSpec((1, d, ht), up_slice),
            pl.BlockSpec((1, hidden, nt), down_slice),
        ],
        out_specs=pl.BlockSpec((None, rows * SLAB_TILE, LANES), y_plane),
        scratch_shapes=[pltpu.VMEM((planes, rows, SLAB_TILE * LANES), BF16), pltpu.VMEM((hs, rows, ht), BF16)],
    )
    return pl.pallas_call(
        functools.partial(_expert_kernel, planes=planes, hs=hs, sub=MOE_SUB_ROWS, tiles=tiles),
        grid_spec=grid_spec,
        out_shape=jax.ShapeDtypeStruct(xs.shape, F32),
        compiler_params=_params("arbitrary", "arbitrary"),
        name="moe_experts",
    )(item_e, item_n, n_used, xs, wg3, wu3, wd3)


def _combine_kernel(dest_ref, next_ref, x_ref, gate_ref, g_ref, b_ref, ys_hbm, y_ref, yb_ref, stage0, stage1, sems,
                    *, tm, alpha):
    i = pl.program_id(0)
    slot = i % 2
    stages = (stage0, stage1)

    def copy(dests, t, k, sl):
        return _token_copy(ys_hbm, dests[0, 0, 2 * t + k], stages[k].at[sl], t, sems.at[sl])

    def start_tile(dests, sl):
        def start(t, _):
            copy(dests, t, 0, sl).start()
            copy(dests, t, 1, sl).start()
            return 0
        lax.fori_loop(0, tm, start, 0)

    @pl.when(i == 0)
    def _():
        start_tile(dest_ref, slot)

    @pl.when(i + 1 < pl.num_programs(0))
    def _():
        start_tile(next_ref, 1 - slot)

    def wait(t, _):
        copy(dest_ref, t, 0, slot).wait()
        copy(dest_ref, t, 1, slot).wait()
        return 0

    lax.fori_loop(0, tm, wait, 0)
    gates = gate_ref[...]
    g0 = gates[:, 0:1]
    g1 = gates[:, 1:2]
    tile = lambda stage, p, c: stage[slot, p, pl.ds(c, tm, stride=SLAB_TILE), :]
    ffn = jnp.concatenate([g0 * tile(stage0, p, c) + g1 * tile(stage1, p, c)
                           for p in range(stage0.shape[1]) for c in range(SLAB_TILE)], axis=1)
    y = _layer_norm_rows(alpha * x_ref[...] + ffn, g_ref[...], b_ref[...])
    y_ref[...] = y
    yb_ref[...] = y.astype(BF16)


def _combine(x1, gates, dest, ys, g, b, alpha):
    m, d = x1.shape
    tm = _pick(m, 256, 8)
    n_tiles = m // tm
    stage = pltpu.VMEM((2, ys.shape[0], tm * SLAB_TILE, LANES), F32)
    dest3 = dest.reshape(n_tiles, 1, TOP_K * tm)
    dest_spec = lambda off: pl.BlockSpec((1, 1, TOP_K * tm), lambda i: (jnp.minimum(i + off, n_tiles - 1), 0, 0),
                                         memory_space=pltpu.SMEM)
    rows = pl.BlockSpec((tm, d), lambda i: (i, 0))
    const = pl.BlockSpec((1, d), lambda i: (0, 0))
    return pl.pallas_call(
        functools.partial(_combine_kernel, tm=tm, alpha=alpha),
        grid=(n_tiles,),
        in_specs=[dest_spec(0), dest_spec(1), rows, pl.BlockSpec((tm, LANES), lambda i: (i, 0)), const, const,
                  pl.BlockSpec(memory_space=pl.ANY)],
        out_specs=[rows, rows],
        out_shape=[jax.ShapeDtypeStruct((m, d), F32), jax.ShapeDtypeStruct((m, d), BF16)],
        scratch_shapes=[stage, stage, pltpu.SemaphoreType.DMA((2,))],
        compiler_params=_params("arbitrary"),
        name="moe_combine",
    )(dest3, dest3, x1, gates, g, b, ys)


def _route_plan(idx, n_items):
    e_flat = idx[:, :TOP_K].reshape(-1)
    onehot = (e_flat[:, None] == jnp.arange(N_EXPERTS, dtype=jnp.int32)[None, :]).astype(jnp.int32)
    csum = jnp.cumsum(onehot, axis=0)
    rank = jnp.sum(onehot * csum, axis=1) - 1
    counts = csum[-1]
    items_e = (counts + MOE_ITEM_ROWS - 1) // MOE_ITEM_ROWS
    item_end = jnp.cumsum(items_e)
    item_start = item_end - items_e
    dest = (item_start[e_flat] * MOE_ITEM_ROWS + rank).astype(jnp.int32)
    ids = jnp.arange(n_items, dtype=jnp.int32)
    item_e = jnp.minimum(jnp.searchsorted(item_end, ids, side="right"), N_EXPERTS - 1).astype(jnp.int32)
    n_used = item_end[-1].astype(jnp.int32)
    item_n = jnp.clip(counts[item_e] - (ids - item_start[item_e]) * MOE_ITEM_ROWS, 0, MOE_ITEM_ROWS)
    item_n = jnp.where(ids < n_used, item_n, 0).astype(jnp.int32)
    return dest, item_e, item_n, n_used.reshape(1)


def kernel(x_prompt, x_sample, cache_k, cache_v, page_table, state_gla, state_conv, rel_table,
           w_in, w_out, lam_q1, lam_k1, lam_q2, lam_k2, diff_norm_g, gla_w_up, gla_b, gla_norm_g,
           conv_w, cm_ln_g, cm_ln_b, cm_ws, cm_bs, ln1_g, ln1_b, ln2_g, ln2_b,
           router_group, router_expert, w_gate, w_up, w_down):
    bp, seq, d = x_prompt.shape
    bs, ts, _ = x_sample.shape
    depth = w_in.shape[0]
    gw = w_out.shape[1] // 4
    nh = gw // HEAD_DIM
    hk = nh * HALF_DIM
    page = cache_k.shape[2]
    n_phys = cache_k.shape[1]
    tp = SAMPLE_ROWS
    n_prompt = bp * seq
    n_all = n_prompt + bs * tp
    alpha = (2.0 * depth) ** 0.25
    assert ts <= tp and nh % 2 == 0 and n_prompt % tp == 0

    x = jnp.concatenate([x_prompt.reshape(n_prompt, d),
                         jnp.pad(x_sample, ((0, 0), (0, tp - ts), (0, 0))).reshape(bs * tp, d)], axis=0)
    xb = x.astype(BF16)

    bias_tiles = _prompt_bias_tiles(rel_table, ATTN_TILE)
    t_idx = jnp.arange(ts)
    k_idx = jnp.arange(page)
    dist_far = jnp.full((ts, page), 2 * page, jnp.int32)
    dist_last = page + t_idx[:, None] - k_idx[None, :]
    dist_new = t_idx[:, None] - k_idx[None, :]
    ok = jnp.stack([dist_far > 0, dist_last > 0, (dist_new >= 0) & (k_idx[None, :] < ts)])
    b3 = _rel_bias(jnp.stack([dist_far, dist_last, dist_new]), rel_table) * LOG2E
    same_head = jnp.eye(nh, dtype=bool)
    b3 = jnp.where(ok[:, None, :, :, None] & same_head[None, :, None, None, :],
                   jnp.transpose(b3, (0, 3, 1, 2))[..., None], MASK_VALUE)
    bias_s = jnp.tile(b3.reshape(3, nh * ts, page * nh), (1, 2, 1))
    half_mask = (jnp.arange(HEAD_DIM)[None, :] < HALF_DIM) == (jnp.arange(2)[:, None] == 0)

    cache_k4 = cache_k.reshape(depth, n_phys, page * nh, HEAD_DIM)
    cache_v4 = cache_v.reshape(depth, n_phys, page * nh, HEAD_DIM)
    n_items = (TOP_K * n_all) // MOE_ITEM_ROWS + N_EXPERTS
    tm_proj = _pick(n_all, 1100, 16)

    outs = {k: [] for k in ("kp", "vp", "ks", "vs", "gp", "gs", "cp", "cs", "ds")}
    for l in range(depth):
        lam_init = 0.8 - 0.6 * math.exp(-0.3 * l)
        lams = (lam_q1[l][None], lam_k1[l][None], lam_q2[l][None], lam_k2[l][None])
        gn_a = diff_norm_g[l][None]

        w_main = jnp.concatenate([w_in[l, :, :5 * gw], w_in[l, :, 5 * gw + GLA_RANK:]], axis=1).astype(BF16)
        w_gate_cols = jnp.pad(w_in[l, :, 5 * gw:5 * gw + GLA_RANK], ((0, 0), (0, LANES - GLA_RANK))).astype(BF16)
        proj = _matmul([xb], w_main, 11 * gw, tm_proj, _pick(11 * gw, 1024, LANES), "in_proj")
        gproj = _matmul([xb], w_gate_cols, LANES, tm_proj, LANES, "gate_proj")
        proj_s = proj[n_prompt:].reshape(bs, tp, 11 * gw)[:, :ts]

        oa_p = _attn_prompt(proj, lams, bias_tiles, gn_a, bp, seq, nh, lam_init)
        q_s = proj_s[..., :gw].reshape(bs, ts, nh, HEAD_DIM) * (HALF_DIM ** -0.5 * LOG2E)
        q2 = (jnp.transpose(q_s, (0, 2, 1, 3))[:, None] * half_mask.astype(F32)[None, :, None, None, :])
        q2 = q2.reshape(bs, 2 * nh * ts, HEAD_DIM).astype(BF16)
        k_s = proj_s[..., gw:2 * gw]
        v_s = proj_s[..., 2 * gw:3 * gw]
        new_rows = lambda a: jnp.pad(a.reshape(bs, ts * nh, HEAD_DIM), ((0, 0), (0, (page - ts) * nh), (0, 0)))
        oa_s = _attn_sample(l, q2, cache_k4, cache_v4, new_rows(k_s), new_rows(v_s), page_table, bias_s, lams, gn_a,
                            lam_init)
        oa_s = jnp.transpose(oa_s.reshape(bs, nh, ts, HEAD_DIM), (0, 2, 1, 3)).reshape(bs, ts, gw)
        oa_s = jnp.pad(oa_s, ((0, 0), (0, tp - ts), (0, 0))).reshape(bs * tp, gw).astype(BF16)

        wup = jnp.pad(gla_w_up[l], ((0, LANES - GLA_RANK), (0, 0)))
        wup_hi, wup_lo = _split_bf16(wup)
        gb = gla_b[l][None]
        gn_b = gla_norm_g[l][None]
        s0_s = jnp.transpose(state_gla[l].reshape(bs, nh // 2, 2 * HALF_DIM, HEAD_DIM), (0, 1, 3, 2))
        s0_p = jnp.zeros((bp, nh // 2, HEAD_DIM, LANES), F32)
        ob_p, sf_p = _gla(proj, gproj, wup_hi, wup_lo, gb, gn_b, s0_p, 0, bp, seq, nh, seq)
        ob_s, sf_s = _gla(proj, gproj, wup_hi, wup_lo, gb, gn_b, s0_s, n_prompt, bs, tp, nh, ts)
        unpair = lambda s: jnp.transpose(s, (0, 1, 3, 2)).reshape(s.shape[0], nh, HALF_DIM, HEAD_DIM)

        bs_t = jnp.transpose(cm_bs[l])
        cw, lg, lb = conv_w[l], cm_ln_g[l][None], cm_ln_b[l][None]
        oc_p, od_p, cs_p, _ = _conv_mlp(proj, cw, jnp.zeros((bp, CONV_WIDTH - 1, gw), F32), lg, lb, cm_ws[l], bs_t,
                                        0, bp, seq, nh, seq)
        oc_s, od_s, cs_s, vd_s = _conv_mlp(proj, cw, state_conv[l], lg, lb, cm_ws[l], bs_t, n_prompt, bs, tp, nh, ts)

        cat = lambda a, b: jnp.concatenate([a, b], axis=0)
        mix = _matmul([cat(oa_p, oa_s), cat(ob_p, ob_s), cat(oc_p, oc_s), cat(od_p, od_s)], w_out[l].astype(BF16),
                      d, tm_proj, _pick(d, 1024, LANES), "out_proj")
        wr = jnp.pad(jnp.concatenate([router_group[l], router_expert[l]], axis=1),
                     ((0, 0), (0, LANES - N_GROUPS - N_EXPERTS)))
        wr_hi, wr_lo = _split_bf16(wr)
        x1, ridx, rgate = _ln_router(x, mix, ln1_g[l][None], ln1_b[l][None], wr_hi, wr_lo, alpha)

        dest, item_e, item_n, n_used = _route_plan(ridx, n_items)
        xs = _dispatch(x1, dest, n_items * MOE_ITEM_ROWS)
        ys = _experts(xs, item_e, item_n, n_used, w_gate, w_up, w_down, l, n_items)
        x, xb = _combine(x1, rgate, dest, ys, ln2_g[l][None], ln2_b[l][None], alpha)

        heads = lambda a, n, t: a.reshape(n, t, nh, HEAD_DIM)
        outs["kp"].append(heads(proj[:n_prompt, gw:2 * gw], bp, seq))
        outs["vp"].append(heads(proj[:n_prompt, 2 * gw:3 * gw], bp, seq))
        outs["ks"].append(heads(k_s, bs, ts))
        outs["vs"].append(heads(v_s, bs, ts))
        outs["gp"].append(unpair(sf_p))
        outs["gs"].append(unpair(sf_s))
        outs["cp"].append(cs_p)
        outs["cs"].append(cs_s)
        outs["ds"].append(vd_s.reshape(bs, tp, gw)[:, :ts])

    y_prompt = x[:n_prompt].reshape(bp, seq, d)
    y_sample = x[n_prompt:].reshape(bs, tp, d)[:, :ts]
    st = lambda k: jnp.stack(outs[k])
    return (y_prompt, y_sample, st("kp"), st("vp"), st("ks"), st("vs"), st("gp"), st("gs"),
            st("cp"), st("cs"), st("ds"))
```

```python
import functools
import math

import jax
import jax.numpy as jnp
from jax import lax
from jax.experimental import pallas as pl
from jax.experimental.pallas import tpu as pltpu

F32 = jnp.float32
BF16 = jnp.bfloat16

HEAD_DIM = 128
HALF_DIM = HEAD_DIM // 2
GLA_RANK = 16
GLA_TAU = 16.0
GLA_CHUNK = 64
CONV_WIDTH = 3
CHUNK_D = 128
REL_BUCKETS = 32
REL_MAX_DIST = 128
N_GROUPS = 4
EXPERTS_PER_GROUP = 8
N_EXPERTS = N_GROUPS * EXPERTS_PER_GROUP
TOP_K = 2
NORM_EPS = 1e-5
MASK_VALUE = -1e30
LOG2E = 1.4426950408889634

LANES = 128
SAMPLE_ROWS = 64
ATTN_TILE = 256
SAMPLE_PAGES_PER_STEP = 8
MOE_ITEM_ROWS = 768
MOE_SUB_ROWS = 128
MOE_HIDDEN_TILE = 256
MOE_DOWN_STEPS = 8
VMEM_LIMIT = 52 * 1024 * 1024


def _pick(n, target, mult):
    if n <= target:
        return n
    best = None
    for d in range(mult, target + 1, mult):
        if n % d == 0:
            best = d
    assert best is not None, (n, target, mult)
    return best


def _params(*sem):
    return pltpu.CompilerParams(dimension_semantics=sem, vmem_limit_bytes=VMEM_LIMIT)


def _dot(a, b):
    return jnp.dot(a, b, preferred_element_type=F32)


def _dot_nt(a, b):
    return lax.dot_general(a, b, (((1,), (1,)), ((), ())), preferred_element_type=F32)


def _dot_tn(a, b):
    return lax.dot_general(a, b, (((0,), (0,)), ((), ())), preferred_element_type=F32)


def _split_bf16(x):
    hi = x.astype(BF16)
    lo = (x - hi.astype(F32)).astype(BF16)
    return hi, lo


def _sigmoid(x):
    return 1.0 / (1.0 + jnp.exp(-x))


def _mm_kernel(*refs, n_in):
    o_ref = refs[2 * n_in]
    acc = None
    for x_ref, w_ref in zip(refs[:n_in], refs[n_in:2 * n_in]):
        d = _dot(x_ref[...], w_ref[...])
        acc = d if acc is None else acc + d
    o_ref[...] = acc.astype(o_ref.dtype)


def _matmul(xs, w, n_out, tm, tn, name):
    n_in = len(xs)
    m, kg = xs[0].shape
    grid = (m // tm, n_out // tn)
    in_specs = [pl.BlockSpec((tm, kg), lambda i, j: (i, 0)) for _ in xs]
    in_specs += [pl.BlockSpec((kg, tn), functools.partial(lambda i, j, g: (g, j), g=g)) for g in range(n_in)]
    return pl.pallas_call(
        functools.partial(_mm_kernel, n_in=n_in),
        grid=grid,
        in_specs=in_specs,
        out_specs=pl.BlockSpec((tm, tn), lambda i, j: (i, j)),
        out_shape=jax.ShapeDtypeStruct((m, n_out), F32),
        compiler_params=_params("parallel", "arbitrary"),
        name=name,
    )(*xs, *([w] * n_in))


def _rel_bias(dist, rel_table):
    n = jnp.maximum(dist, 0)
    max_exact = REL_BUCKETS // 2
    nf = jnp.maximum(n, 1).astype(F32)
    large = max_exact + (jnp.log(nf / max_exact) / math.log(REL_MAX_DIST / max_exact)
                         * (REL_BUCKETS - max_exact)).astype(jnp.int32)
    bucket = jnp.where(n < max_exact, n, jnp.minimum(large, REL_BUCKETS - 1))
    onehot = (bucket[..., None] == jnp.arange(REL_BUCKETS, dtype=bucket.dtype)).astype(F32)
    return jnp.einsum("...k,kh->...h", onehot, rel_table.astype(F32), precision=lax.Precision.HIGHEST)


def _prompt_bias_tiles(rel_table, t):
    r = jnp.arange(t)[:, None]
    c = jnp.arange(t)[None, :]
    tiles = []
    for delta in range(3):
        dist = delta * t + r - c
        b = _rel_bias(dist, rel_table) * LOG2E
        tiles.append(jnp.where((dist >= 0)[..., None], b, MASK_VALUE))
    return jnp.transpose(jnp.stack(tiles), (3, 0, 1, 2))


def _lam_value(lq1, lk1, lq2, lk2, lam_init):
    return (jnp.exp(jnp.sum(lq1[...] * lk1[...], axis=1, keepdims=True))
            - jnp.exp(jnp.sum(lq2[...] * lk2[...], axis=1, keepdims=True)) + lam_init)


def _rms_head(o, gain, post):
    return o * lax.rsqrt(jnp.mean(o * o, axis=1, keepdims=True) + NORM_EPS) * gain * post


def _attn_prompt_kernel(lq1, lk1, lq2, lk2, q_ref, k_ref, v_ref, bias_ref, gn_ref, o_ref, *, t, lam_init):
    qi = pl.program_id(2)
    lam = _lam_value(lq1, lk1, lq2, lk2, lam_init)
    lane = lax.broadcasted_iota(jnp.int32, (t, HEAD_DIM), 1)
    heads = []
    for h in range(2):
        q = q_ref[:, h * HEAD_DIM:(h + 1) * HEAD_DIM] * (HALF_DIM ** -0.5 * LOG2E)
        heads.append(jnp.concatenate([jnp.where(lane < HALF_DIM, q, 0.0), jnp.where(lane >= HALF_DIM, q, 0.0)],
                                     axis=0).astype(BF16))

    def body(j, carry):
        start = pl.multiple_of(j * t, t)
        tile = jnp.minimum(qi - j, 2)
        out = []
        for h in range(2):
            m, s, acc = carry[h]
            cols = slice(h * HEAD_DIM, (h + 1) * HEAD_DIM)
            kt = k_ref[pl.ds(start, t), cols].astype(BF16)
            vt = v_ref[pl.ds(start, t), cols].astype(BF16)
            bt = bias_ref[h, tile]
            l = _dot_nt(heads[h], kt) + jnp.concatenate([bt, bt], axis=0)
            m_new = jnp.maximum(m, jnp.max(l, axis=1, keepdims=True))
            a = jnp.exp2(m - m_new)
            p = jnp.exp2(l - m_new)
            out.append((m_new, a * s + jnp.sum(p, axis=1, keepdims=True), a * acc + _dot(p.astype(BF16), vt)))
        return tuple(out)

    init = (jnp.full((2 * t, 1), MASK_VALUE, F32), jnp.zeros((2 * t, 1), F32), jnp.zeros((2 * t, HEAD_DIM), F32))
    res = lax.fori_loop(0, qi + 1, body, (init, init))
    for h in range(2):
        _, s, acc = res[h]
        o = acc / s
        o = o[:t] - lam * o[t:]
        o_ref[:, h * HEAD_DIM:(h + 1) * HEAD_DIM] = _rms_head(o, gn_ref[...], 1.0 - lam_init).astype(o_ref.dtype)


def _attn_prompt(proj, lams, bias_tiles, gn, n_seq, seq, n_heads, lam_init):
    t = ATTN_TILE
    assert seq % t == 0 and t >= REL_MAX_DIST and n_heads % 2 == 0
    nq = seq // t
    pair = 2 * HEAD_DIM
    lam_spec = pl.BlockSpec((1, HALF_DIM), lambda b, h, i: (0, 0))
    return pl.pallas_call(
        functools.partial(_attn_prompt_kernel, t=t, lam_init=lam_init),
        grid=(n_seq, n_heads // 2, nq),
        in_specs=[lam_spec] * 4 + [
            pl.BlockSpec((t, pair), lambda b, h, i: (b * nq + i, h)),
            pl.BlockSpec((seq, pair), lambda b, h, i: (b, n_heads // 2 + h)),
            pl.BlockSpec((seq, pair), lambda b, h, i: (b, n_heads + h)),
            pl.BlockSpec((2, 3, t, t), lambda b, h, i: (h, 0, 0, 0)),
            pl.BlockSpec((1, HEAD_DIM), lambda b, h, i: (0, 0)),
        ],
        out_specs=pl.BlockSpec((t, pair), lambda b, h, i: (b * nq + i, h)),
        out_shape=jax.ShapeDtypeStruct((proj.shape[0], n_heads * HEAD_DIM), BF16),
        compiler_params=_params("parallel", "parallel", "arbitrary"),
        name="attn_prompt",
    )(*lams, proj, proj, proj, bias_tiles, gn)


def _attn_sample_kernel(pt_ref, lq1, lk1, lq2, lk2, q_ref, *refs, n_pp, lam_init):
    kc_refs, vc_refs = refs[:n_pp], refs[n_pp:2 * n_pp]
    kn_ref, vn_ref, bias_ref, gn_ref, o_ref, m_ref, s_ref, acc_ref = refs[2 * n_pp:]
    p = pl.program_id(1)
    n_steps = pl.num_programs(1)

    @pl.when(p == 0)
    def _():
        m_ref[...] = jnp.full(m_ref.shape, MASK_VALUE, F32)
        s_ref[...] = jnp.zeros(s_ref.shape, F32)
        acc_ref[...] = jnp.zeros(acc_ref.shape, F32)

    q = q_ref[0]

    def step(ks, vs, biases):
        ls = [_dot_nt(q, k.astype(BF16)) + b for k, b in zip(ks, biases)]
        m = m_ref[...]
        m_new = m
        for l in ls:
            m_new = jnp.maximum(m_new, jnp.max(l, axis=1, keepdims=True))
        a = jnp.exp2(m - m_new)
        s = a * s_ref[...]
        acc = a * acc_ref[...]
        for l, v in zip(ls, vs):
            pr = jnp.exp2(l - m_new)
            s = s + jnp.sum(pr, axis=1, keepdims=True)
            acc = acc + _dot(pr.astype(BF16), v.astype(BF16))
        s_ref[...] = s
        acc_ref[...] = acc
        m_ref[...] = m_new

    last = p == n_steps - 1
    tables = [0] * (n_pp - 1) + [jnp.where(last, 1, 0)]
    step([r[0, 0] for r in kc_refs], [r[0, 0] for r in vc_refs], [bias_ref[t] for t in tables])

    @pl.when(last)
    def _():
        step([kn_ref[0]], [vn_ref[0]], [bias_ref[2]])
        lam = _lam_value(lq1, lk1, lq2, lk2, lam_init)
        o = acc_ref[...] / s_ref[...]
        half = o.shape[0] // 2
        o_ref[0] = _rms_head(o[:half] - lam * o[half:], gn_ref[...], 1.0 - lam_init)


def _attn_sample(layer, q2, cache_k, cache_v, k_new, v_new, page_table, bias, lams, gn, lam_init):
    n_seq, n_pages = page_table.shape
    prow = cache_k.shape[2]
    rows = q2.shape[1]
    n_pp = SAMPLE_PAGES_PER_STEP if n_pages % SAMPLE_PAGES_PER_STEP == 0 else 1
    lam_spec = pl.BlockSpec((1, HALF_DIM), lambda b, p, pt: (0, 0))
    cache_specs = [pl.BlockSpec((1, 1, prow, HEAD_DIM),
                                functools.partial(lambda b, p, pt, i: (layer, pt[b, p * n_pp + i], 0, 0), i=i))
                   for i in range(n_pp)]
    new_spec = pl.BlockSpec((1, prow, HEAD_DIM), lambda b, p, pt: (b, 0, 0))
    grid_spec = pltpu.PrefetchScalarGridSpec(
        num_scalar_prefetch=1,
        grid=(n_seq, n_pages // n_pp),
        in_specs=[lam_spec] * 4 + [pl.BlockSpec((1, rows, HEAD_DIM), lambda b, p, pt: (b, 0, 0))]
        + cache_specs + cache_specs + [
            new_spec, new_spec,
            pl.BlockSpec((3, rows, prow), lambda b, p, pt: (0, 0, 0)),
            pl.BlockSpec((1, HEAD_DIM), lambda b, p, pt: (0, 0)),
        ],
        out_specs=pl.BlockSpec((1, rows // 2, HEAD_DIM), lambda b, p, pt: (b, 0, 0)),
        scratch_shapes=[pltpu.VMEM((rows, 1), F32), pltpu.VMEM((rows, 1), F32), pltpu.VMEM((rows, HEAD_DIM), F32)],
    )
    return pl.pallas_call(
        functools.partial(_attn_sample_kernel, n_pp=n_pp, lam_init=lam_init),
        grid_spec=grid_spec,
        out_shape=jax.ShapeDtypeStruct((n_seq, rows // 2, HEAD_DIM), F32),
        compiler_params=_params("parallel", "arbitrary"),
        name="attn_sample",
    )(page_table, *lams, q2, *([cache_k] * n_pp), *([cache_v] * n_pp), k_new, v_new, bias, gn)


def _gla_kernel(q_ref, k_ref, v_ref, r_ref, gp_ref, wh_ref, wl_ref, gb_ref, gn_ref, s0_ref, *rest,
                chunk, n_chunks, n_heads, valid):
    o_ref, sfin_ref, st_ref = rest[-3:]
    t = pl.program_id(1)

    @pl.when(t == 0)
    def _():
        st_ref[...] = s0_ref[0]

    row = lax.broadcasted_iota(jnp.int32, (chunk, chunk), 0)
    col = lax.broadcasted_iota(jnp.int32, (chunk, chunk), 1)
    tril = row >= col
    tri_bf = jnp.where(tril, 1.0, 0.0).astype(BF16)
    lane = lax.broadcasted_iota(jnp.int32, (chunk, LANES), 1)
    lane_sq = lax.broadcasted_iota(jnp.int32, (HEAD_DIM, LANES), 1)
    mid = chunk // 2 - 1

    for c in range(n_chunks):
        rs = slice(c * chunk, (c + 1) * chunk)
        gh, gl = _split_bf16(gp_ref[rs, :])
        z = _dot(gh, wh_ref[...]) + _dot(gl, wh_ref[...]) + _dot(gh, wl_ref[...]) + gb_ref[...]
        g = -(jnp.maximum(-z, 0.0) + jnp.log(1.0 + jnp.exp(-jnp.abs(z)))) * (1.0 / GLA_TAU)
        k = k_ref[rs, :]
        if valid < chunk:
            ok = lax.broadcasted_iota(jnp.int32, g.shape, 0) < valid
            g = jnp.where(ok, g, 0.0)
            k = jnp.where(ok, k, 0.0)
        gh, gl = _split_bf16(g)
        b = _dot(tri_bf, gh) + _dot(tri_bf, gl)
        b_last = b[chunk - 1:chunk, :]
        b_mid = b[mid:mid + 1, :]
        q = q_ref[rs, :] * (HALF_DIM ** -0.5)
        q_state = q * jnp.exp(b)
        q_mid = q * jnp.exp(b - b_mid)
        k_mid = (k * jnp.exp(b_mid - b)).astype(BF16)
        k_end = (k * jnp.exp(b_last - b)).astype(BF16)
        decay = jnp.exp(b_last)

        for hp in range(n_heads // 2):
            ls = slice(hp * LANES, (hp + 1) * LANES)
            st = st_ref[hp]
            st_bf = st.astype(BF16)
            upd = []
            for par in range(2):
                h = 2 * hp + par
                mine = (lane < HALF_DIM) if par == 0 else (lane >= HALF_DIM)
                hs = slice(h * HEAD_DIM, (h + 1) * HEAD_DIM)
                v_h = v_ref[rs, hs].astype(BF16)
                att = _dot_nt(jnp.where(mine, q_mid[:, ls], 0.0).astype(BF16), k_mid[:, ls])
                att = jnp.where(tril, att, 0.0).astype(BF16)
                o = _dot(att, v_h) + _dot_nt(jnp.where(mine, q_state[:, ls], 0.0).astype(BF16), st_bf)
                r_h = r_ref[rs, hs]
                o_ref[rs, hs] = _rms_head(o, gn_ref[...], r_h * _sigmoid(r_h)).astype(o_ref.dtype)
                upd.append(_dot_tn(v_h, k_end[:, ls]))
            st_ref[hp] = st * decay[:, ls] + jnp.where(lane_sq < HALF_DIM, upd[0], upd[1])

    @pl.when(t == pl.num_programs(1) - 1)
    def _():
        sfin_ref[0] = st_ref[...]


def _into(prev):
    specs = [pl.BlockSpec(memory_space=pl.ANY) for _ in prev]
    return specs, tuple(prev)


def _gla(proj, gproj, wup_hi, wup_lo, gla_b, gn, s0, row0, n_seq, seq, n_heads, valid, prev=()):
    gw = n_heads * HEAD_DIM
    hk = n_heads * HALF_DIM
    chunk = min(GLA_CHUNK, seq)
    tt = _pick(seq, 256, chunk)
    nt = seq // tt
    r0 = row0 // tt
    assert row0 % tt == 0
    qcol = 3 * gw // hk
    row_spec = lambda width, colblk: pl.BlockSpec((tt, width), lambda b, t: (r0 + b * nt + t, colblk))
    const = lambda shape: pl.BlockSpec(shape, lambda b, t: tuple(0 for _ in shape))
    st_spec = pl.BlockSpec((1, n_heads // 2, HEAD_DIM, LANES), lambda b, t: (b, 0, 0, 0))
    prev_specs, prev_args = _into(prev)
    n_in = 10
    return pl.pallas_call(
        functools.partial(_gla_kernel, chunk=chunk, n_chunks=tt // chunk, n_heads=n_heads, valid=valid),
        grid=(n_seq, nt),
        in_specs=[row_spec(hk, qcol), row_spec(hk, qcol + 1), row_spec(gw, 4), row_spec(gw, 5),
                  pl.BlockSpec((tt, LANES), lambda b, t: (r0 + b * nt + t, 0)),
                  const((LANES, hk)), const((LANES, hk)), const((1, hk)), const((1, HEAD_DIM)), st_spec] + prev_specs,
        out_specs=[pl.BlockSpec((tt, gw), lambda b, t: (r0 + b * nt + t, 0)), st_spec],
        out_shape=[jax.ShapeDtypeStruct((proj.shape[0], gw), BF16),
                   jax.ShapeDtypeStruct((n_seq, n_heads // 2, HEAD_DIM, LANES), F32)],
        input_output_aliases={n_in + k: k for k in range(len(prev))},
        scratch_shapes=[pltpu.VMEM((n_heads // 2, HEAD_DIM, LANES), F32)],
        compiler_params=_params("parallel", "arbitrary"),
        name="gla",
    )(proj, proj, proj, proj, gproj, wup_hi, wup_lo, gla_b, gn, s0, *prev_args)


def _conv_mlp_kernel(cb_ref, cc_ref, ch_ref, du_ref, dv_ref, cw_ref, buf_ref, lg_ref, lb_ref, ws_ref, bs_ref, *rest,
                     tt, chunk, n_heads, valid):
    oc_ref, od_ref, cs_ref, vd_ref, carry_ref = rest[-5:]
    t = pl.program_id(1)

    @pl.when(t == 0)
    def _():
        carry_ref[...] = buf_ref[0]

    z = cc_ref[...] * ch_ref[...]
    row = lax.broadcasted_iota(jnp.int32, z.shape, 0)
    c0 = carry_ref[0:1, :]
    c1 = carry_ref[1:2, :]
    z1 = jnp.where(row == 0, c1, pltpu.roll(z, 1, 0))
    z2 = jnp.where(row == 0, c0, jnp.where(row == 1, c1, pltpu.roll(z, 2, 0)))
    w = cw_ref[...]
    oc_ref[...] = (cb_ref[...] * (z2 * w[0:1, :] + z1 * w[1:2, :] + z * w[2:3, :])).astype(oc_ref.dtype)
    if valid >= 2:
        carry_ref[...] = z[valid - 2:valid, :]
    else:
        carry_ref[...] = jnp.concatenate([c1, z[0:1, :]], axis=0)

    @pl.when(t == pl.num_programs(1) - 1)
    def _():
        cs_ref[0] = carry_ref[...]

    x = dv_ref[...]
    mu = jnp.mean(x, axis=1, keepdims=True)
    xc = x - mu
    var = jnp.mean(xc * xc, axis=1, keepdims=True)
    vd = xc * lax.rsqrt(var + NORM_EPS) * lg_ref[...] + lb_ref[...]
    vd_ref[...] = vd
    vd_bf = vd.astype(BF16)
    r2 = lax.broadcasted_iota(jnp.int32, (chunk, chunk), 0)
    c2 = lax.broadcasted_iota(jnp.int32, (chunk, chunk), 1)
    for g in range(n_heads):
        wc = jnp.where(r2 >= c2, ws_ref[g, :chunk, :chunk], 0.0).astype(BF16)
        bias = bs_ref[:chunk, g:g + 1]
        cols = slice(g * HEAD_DIM, (g + 1) * HEAD_DIM)
        for c in range(tt // chunk):
            rs = slice(c * chunk, (c + 1) * chunk)
            zc = _dot(wc, vd_bf[rs, cols]) + bias
            od_ref[rs, cols] = (du_ref[rs, cols] * zc).astype(od_ref.dtype)


def _conv_mlp(proj, conv_w, buf, ln_g, ln_b, ws, bs_t, row0, n_seq, seq, n_heads, valid, keep_vd, prev=()):
    gw = n_heads * HEAD_DIM
    chunk = min(CHUNK_D, seq)
    tt = _pick(seq, 256, chunk)
    nt = seq // tt
    r0 = row0 // tt
    assert row0 % tt == 0 and (valid == seq or nt == 1)
    row_spec = lambda colblk: pl.BlockSpec((tt, gw), lambda b, t: (r0 + b * nt + t, colblk))
    const = lambda shape: pl.BlockSpec(shape, lambda b, t: tuple(0 for _ in shape))
    out_rows = pl.BlockSpec((tt, gw), lambda b, t: (r0 + b * nt + t, 0))
    vd_spec = pl.BlockSpec((tt, gw), (lambda b, t: (b * nt + t, 0)) if keep_vd else (lambda b, t: (0, 0)))
    state_spec = pl.BlockSpec((1, CONV_WIDTH - 1, gw), lambda b, t: (b, 0, 0))
    prev_specs, prev_args = _into(prev)
    n_in = 11
    return pl.pallas_call(
        functools.partial(_conv_mlp_kernel, tt=tt, chunk=chunk, n_heads=n_heads, valid=min(valid, tt)),
        grid=(n_seq, nt),
        in_specs=[row_spec(6), row_spec(7), row_spec(8), row_spec(9), row_spec(10),
                  const((CONV_WIDTH, gw)), state_spec, const((1, gw)), const((1, gw)),
                  const((n_heads, CHUNK_D, CHUNK_D)), const((CHUNK_D, n_heads))] + prev_specs,
        out_specs=[out_rows, out_rows, state_spec, vd_spec],
        out_shape=[jax.ShapeDtypeStruct((proj.shape[0], gw), BF16), jax.ShapeDtypeStruct((proj.shape[0], gw), BF16),
                   jax.ShapeDtypeStruct((n_seq, CONV_WIDTH - 1, gw), F32),
                   jax.ShapeDtypeStruct((n_seq * seq if keep_vd else tt, gw), F32)],
        input_output_aliases={n_in + k: k for k in range(len(prev))},
        scratch_shapes=[pltpu.VMEM((CONV_WIDTH - 1, gw), F32)],
        compiler_params=_params("parallel", "arbitrary"),
        name="conv_mlp",
    )(proj, proj, proj, proj, proj, conv_w, buf, ln_g, ln_b, ws, bs_t, *prev_args)


def _layer_norm_rows(x, g, b):
    mu = jnp.mean(x, axis=1, keepdims=True)
    xc = x - mu
    var = jnp.mean(xc * xc, axis=1, keepdims=True)
    return xc * lax.rsqrt(var + NORM_EPS) * g + b


def _ln_router_kernel(xa_ref, xb_ref, mix_ref, g_ref, b_ref, wh_ref, wl_ref, y_ref, idx_ref, gate_ref, *, alpha, split):
    x = jnp.where(pl.program_id(0) < split, xa_ref[...], xb_ref[...])
    y = _layer_norm_rows(alpha * x + mix_ref[...], g_ref[...], b_ref[...])
    y_ref[...] = y
    yh, yl = _split_bf16(y)
    lg = _dot(yh, wh_ref[...]) + _dot(yl, wh_ref[...]) + _dot(yh, wl_ref[...])
    lane = lax.broadcasted_iota(jnp.int32, lg.shape, 1)
    neg = -jnp.inf
    glog = jnp.where(lane < N_GROUPS, lg, neg)
    gmax = jnp.max(glog, axis=1, keepdims=True)
    g_star = jnp.min(jnp.where(glog == gmax, lane, LANES), axis=1, keepdims=True)
    p_top = 1.0 / jnp.sum(jnp.exp(glog - gmax), axis=1, keepdims=True)
    lo = N_GROUPS + EXPERTS_PER_GROUP * g_star
    w1 = jnp.where((lane >= lo) & (lane < lo + EXPERTS_PER_GROUP), lg, neg)
    v1 = jnp.max(w1, axis=1, keepdims=True)
    i1 = jnp.min(jnp.where(w1 == v1, lane, LANES), axis=1, keepdims=True)
    w2 = jnp.where(lane == i1, neg, w1)
    v2 = jnp.max(w2, axis=1, keepdims=True)
    i2 = jnp.min(jnp.where(w2 == v2, lane, LANES), axis=1, keepdims=True)
    e21 = jnp.exp(v2 - v1)
    gate1 = p_top / (1.0 + e21)
    gate2 = p_top * e21 / (1.0 + e21)
    idx_ref[...] = jnp.where(lane == 0, i1 - N_GROUPS, jnp.where(lane == 1, i2 - N_GROUPS, 0))
    gate_ref[...] = jnp.where(lane == 0, gate1, jnp.where(lane == 1, gate2, 0.0))


def _ln_router(x_parts, mix, g, b, wr_hi, wr_lo, alpha):
    m, d = mix.shape
    tm = _pick(math.gcd(m, x_parts[0].shape[0]), 256, 8)
    xa, xb = x_parts if x_parts[1] is not None else (x_parts[0], x_parts[0])
    split = xa.shape[0] // tm if x_parts[1] is not None else m // tm
    rows = pl.BlockSpec((tm, d), lambda i: (i, 0))
    const = lambda shape: pl.BlockSpec(shape, lambda i: (0, 0))
    small = pl.BlockSpec((tm, LANES), lambda i: (i, 0))
    return pl.pallas_call(
        functools.partial(_ln_router_kernel, alpha=alpha, split=split),
        grid=(m // tm,),
        in_specs=[pl.BlockSpec((tm, d), lambda i: (jnp.minimum(i, split - 1), 0)),
                  pl.BlockSpec((tm, d), lambda i: (jnp.maximum(i - split, 0), 0)),
                  rows, const((1, d)), const((1, d)), const((d, LANES)), const((d, LANES))],
        out_specs=[rows, small, small],
        out_shape=[jax.ShapeDtypeStruct((m, d), F32),
                   jax.ShapeDtypeStruct((m, LANES), jnp.int32), jax.ShapeDtypeStruct((m, LANES), F32)],
        compiler_params=_params("parallel"),
        name="ln_router",
    )(xa, xb, mix, g, b, wr_hi, wr_lo)


SLAB_TILE = 8


def _token_copy(src, src_row, dst, dst_row, sem):
    rows = lambda r: pl.ds(pl.multiple_of(r * SLAB_TILE, SLAB_TILE), SLAB_TILE)
    return pltpu.make_async_copy(src.at[:, rows(src_row), :], dst.at[:, rows(dst_row), :], sem)


def _dispatch_kernel(dest_ref, x_ref, xs_hbm, stage, sem, *, tm):
    for p in range(stage.shape[0]):
        for c in range(SLAB_TILE):
            col = (p * SLAB_TILE + c) * LANES
            stage[p, pl.ds(c, tm, stride=SLAB_TILE), :] = x_ref[:, col:col + LANES]

    def copy(i, k):
        return _token_copy(stage, i, xs_hbm, dest_ref[0, 0, 2 * i + k], sem)

    def start(i, _):
        copy(i, 0).start()
        copy(i, 1).start()
        return 0

    def wait(i, _):
        copy(i, 0).wait()
        copy(i, 1).wait()
        return 0

    lax.fori_loop(0, tm, start, 0)
    lax.fori_loop(0, tm, wait, 0)


def _dispatch(x1, dest, n_rows):
    m, d = x1.shape
    planes = d // (SLAB_TILE * LANES)
    tm = _pick(m, 256, 8)
    dest3 = dest.reshape(m // tm, 1, TOP_K * tm)
    return pl.pallas_call(
        functools.partial(_dispatch_kernel, tm=tm),
        grid=(m // tm,),
        in_specs=[pl.BlockSpec((1, 1, TOP_K * tm), lambda i: (i, 0, 0), memory_space=pltpu.SMEM),
                  pl.BlockSpec((tm, d), lambda i: (i, 0))],
        out_specs=pl.BlockSpec(memory_space=pl.ANY),
        out_shape=jax.ShapeDtypeStruct((planes, n_rows * SLAB_TILE, LANES), F32),
        scratch_shapes=[pltpu.VMEM((planes, tm * SLAB_TILE, LANES), F32), pltpu.SemaphoreType.DMA(())],
        compiler_params=pltpu.CompilerParams(dimension_semantics=("arbitrary",), has_side_effects=True,
                                             vmem_limit_bytes=VMEM_LIMIT),
        name="moe_dispatch",
    )(dest3, x1)


def _expert_kernel(ie_ref, in_ref, nu_ref, x_ref, wg_ref, wu_ref, wd_ref, y_ref, xb, h_buf, *, planes, hs, sub, tiles):
    i = pl.program_id(0)
    s = pl.program_id(1)
    n_sub = (in_ref[i] + sub - 1) // sub
    live = i < nu_ref[0]
    per_blk = SLAB_TILE // tiles
    up0 = planes
    down0 = planes + hs
    kp = SLAB_TILE * LANES
    ht = h_buf.shape[2]

    def for_rows(fn):
        for k in range(1, xb.shape[1] // sub + 1):
            pl.when(n_sub == k)(functools.partial(fn, k * sub))

    @pl.when(live & (s < up0))
    def _():
        def convert(m):
            xb[s, :m, :] = jnp.concatenate([x_ref[pl.ds(c, m, stride=SLAB_TILE), :]
                                            for c in range(SLAB_TILE)], axis=1).astype(BF16)
        for_rows(convert)

    @pl.when(live & (s >= up0) & (s < down0))
    def _():
        wg = wg_ref[0].astype(BF16)
        wu = wu_ref[0].astype(BF16)

        def up(m):
            g = sum(_dot(xb[p, :m, :], wg[p * kp:(p + 1) * kp, :]) for p in range(planes))
            u = sum(_dot(xb[p, :m, :], wu[p * kp:(p + 1) * kp, :]) for p in range(planes))
            h_buf[s - up0, :m, :] = (g * _sigmoid(g) * u).astype(BF16)
        for_rows(up)

    @pl.when(live & (s >= down0))
    def _():
        wd = wd_ref[0].astype(BF16)
        c0 = ((s - down0) % per_blk) * tiles

        def down(m):
            y = sum(_dot(h_buf[j, :m, :], wd[j * ht:(j + 1) * ht, :]) for j in range(hs))
            for c in range(tiles):
                y_ref[pl.ds(c0 + c, m, stride=SLAB_TILE), :] = y[:, c * LANES:(c + 1) * LANES]
        for_rows(down)


def _experts(xs, item_e, item_n, n_used, w_gate, w_up, w_down, layer, n_items):
    d, hidden = w_gate.shape[-2:]
    n_exp = w_gate.shape[1]
    planes = xs.shape[0]
    ht = min(MOE_HIDDEN_TILE, hidden)
    hs = hidden // ht
    ns = MOE_DOWN_STEPS
    nt = d // ns
    tiles = nt // LANES
    assert SLAB_TILE % tiles == 0 and hidden % ht == 0
    per_blk = SLAB_TILE // tiles
    n_steps = planes + hs + ns
    wg3 = w_gate.reshape(-1, d, hidden)
    wu3 = w_up.reshape(-1, d, hidden)
    wd3 = w_down.reshape(-1, hidden, d)
    e0 = layer * n_exp
    rows = MOE_ITEM_ROWS

    def item(i, nu):
        return jnp.minimum(i, nu[0] - 1)

    def step(i, s, nu):
        return jnp.where(i < nu[0], s, n_steps - 1)

    def expert(i, ie, nu):
        return e0 + ie[item(i, nu)]

    def x_plane(i, s, ie, inn, nu):
        return jnp.minimum(step(i, s, nu), planes - 1), item(i, nu), 0

    def up_slice(i, s, ie, inn, nu):
        return expert(i, ie, nu), 0, jnp.clip(step(i, s, nu) - planes, 0, hs - 1)

    def down_slice(i, s, ie, inn, nu):
        return expert(i, ie, nu), 0, jnp.maximum(step(i, s, nu) - planes - hs, 0)

    def y_plane(i, s, ie, inn, nu):
        return jnp.maximum(step(i, s, nu) - planes - hs, 0) // per_blk, item(i, nu), 0

    grid_spec = pltpu.PrefetchScalarGridSpec(
        num_scalar_prefetch=3,
        grid=(n_items, n_steps),
        in_specs=[
            pl.BlockSpec((None, rows * SLAB_TILE, LANES), x_plane),
            pl.BlockSpec((1, d, ht), up_slice),
            pl.BlockSpec((1, d, ht), up_slice),
            pl.BlockSpec((1, hidden, nt), down_slice),
        ],
        out_specs=pl.BlockSpec((None, rows * SLAB_TILE, LANES), y_plane),
        scratch_shapes=[pltpu.VMEM((planes, rows, SLAB_TILE * LANES), BF16), pltpu.VMEM((hs, rows, ht), BF16)],
    )
    return pl.pallas_call(
        functools.partial(_expert_kernel, planes=planes, hs=hs, sub=MOE_SUB_ROWS, tiles=tiles),
        grid_spec=grid_spec,
        out_shape=jax.ShapeDtypeStruct(xs.shape, F32),
        compiler_params=_params("arbitrary", "arbitrary"),
        name="moe_experts",
    )(item_e, item_n, n_used, xs, wg3, wu3, wd3)


def _combine_kernel(dest_ref, next_ref, x_ref, gate_ref, g_ref, b_ref, ys_hbm, y_ref, yb_ref, stage0, stage1, sems,
                    *, tm, alpha):
    i = pl.program_id(0)
    slot = i % 2
    stages = (stage0, stage1)

    def copy(dests, t, k, sl):
        return _token_copy(ys_hbm, dests[0, 0, 2 * t + k], stages[k].at[sl], t, sems.at[sl])

    def start_tile(dests, sl):
        def start(t, _):
            copy(dests, t, 0, sl).start()
            copy(dests, t, 1, sl).start()
            return 0
        lax.fori_loop(0, tm, start, 0)

    @pl.when(i == 0)
    def _():
        start_tile(dest_ref, slot)

    @pl.when(i + 1 < pl.num_programs(0))
    def _():
        start_tile(next_ref, 1 - slot)

    def wait(t, _):
        copy(dest_ref, t, 0, slot).wait()
        copy(dest_ref, t, 1, slot).wait()
        return 0

    lax.fori_loop(0, tm, wait, 0)
    gates = gate_ref[...]
    g0 = gates[:, 0:1]
    g1 = gates[:, 1:2]
    tile = lambda stage, p, c: stage[slot, p, pl.ds(c, tm, stride=SLAB_TILE), :]
    ffn = jnp.concatenate([g0 * tile(stage0, p, c) + g1 * tile(stage1, p, c)
                           for p in range(stage0.shape[1]) for c in range(SLAB_TILE)], axis=1)
    y = _layer_norm_rows(alpha * x_ref[...] + ffn, g_ref[...], b_ref[...])
    y_ref[...] = y
    yb_ref[...] = y.astype(BF16)


def _combine(x1, gates, dest, ys, g, b, alpha):
    m, d = x1.shape
    tm = _pick(m, 256, 8)
    n_tiles = m // tm
    stage = pltpu.VMEM((2, ys.shape[0], tm * SLAB_TILE, LANES), F32)
    dest3 = dest.reshape(n_tiles, 1, TOP_K * tm)
    dest_spec = lambda off: pl.BlockSpec((1, 1, TOP_K * tm), lambda i: (jnp.minimum(i + off, n_tiles - 1), 0, 0),
                                         memory_space=pltpu.SMEM)
    rows = pl.BlockSpec((tm, d), lambda i: (i, 0))
    const = pl.BlockSpec((1, d), lambda i: (0, 0))
    return pl.pallas_call(
        functools.partial(_combine_kernel, tm=tm, alpha=alpha),
        grid=(n_tiles,),
        in_specs=[dest_spec(0), dest_spec(1), rows, pl.BlockSpec((tm, LANES), lambda i: (i, 0)), const, const,
                  pl.BlockSpec(memory_space=pl.ANY)],
        out_specs=[rows, rows],
        out_shape=[jax.ShapeDtypeStruct((m, d), F32), jax.ShapeDtypeStruct((m, d), BF16)],
        scratch_shapes=[stage, stage, pltpu.SemaphoreType.DMA((2,))],
        compiler_params=_params("arbitrary"),
        name="moe_combine",
    )(dest3, dest3, x1, gates, g, b, ys)


def _route_plan(idx, n_items):
    e_flat = idx[:, :TOP_K].reshape(-1)
    onehot = (e_flat[:, None] == jnp.arange(N_EXPERTS, dtype=jnp.int32)[None, :]).astype(jnp.int32)
    csum = jnp.cumsum(onehot, axis=0)
    rank = jnp.sum(onehot * csum, axis=1) - 1
    counts = csum[-1]
    items_e = (counts + MOE_ITEM_ROWS - 1) // MOE_ITEM_ROWS
    item_end = jnp.cumsum(items_e)
    item_start = item_end - items_e
    dest = (item_start[e_flat] * MOE_ITEM_ROWS + rank).astype(jnp.int32)
    ids = jnp.arange(n_items, dtype=jnp.int32)
    item_e = jnp.minimum(jnp.searchsorted(item_end, ids, side="right"), N_EXPERTS - 1).astype(jnp.int32)
    n_used = item_end[-1].astype(jnp.int32)
    item_n = jnp.clip(counts[item_e] - (ids - item_start[item_e]) * MOE_ITEM_ROWS, 0, MOE_ITEM_ROWS)
    item_n = jnp.where(ids < n_used, item_n, 0).astype(jnp.int32)
    return dest, item_e, item_n, n_used.reshape(1)


def kernel(x_prompt, x_sample, cache_k, cache_v, page_table, state_gla, state_conv, rel_table,
           w_in, w_out, lam_q1, lam_k1, lam_q2, lam_k2, diff_norm_g, gla_w_up, gla_b, gla_norm_g,
           conv_w, cm_ln_g, cm_ln_b, cm_ws, cm_bs, ln1_g, ln1_b, ln2_g, ln2_b,
           router_group, router_expert, w_gate, w_up, w_down):
    bp, seq, d = x_prompt.shape
    bs, ts, _ = x_sample.shape
    depth = w_in.shape[0]
    gw = w_out.shape[1] // 4
    nh = gw // HEAD_DIM
    hk = nh * HALF_DIM
    page = cache_k.shape[2]
    n_phys = cache_k.shape[1]
    tp = SAMPLE_ROWS
    n_prompt = bp * seq
    n_all = n_prompt + bs * tp
    alpha = (2.0 * depth) ** 0.25
    assert ts <= tp and nh % 2 == 0 and n_prompt % tp == 0

    x_parts = (x_prompt.reshape(n_prompt, d), jnp.pad(x_sample, ((0, 0), (0, tp - ts), (0, 0))).reshape(bs * tp, d))
    xb = jnp.concatenate([x_parts[0].astype(BF16), x_parts[1].astype(BF16)], axis=0)

    bias_tiles = _prompt_bias_tiles(rel_table, ATTN_TILE)
    t_idx = jnp.arange(ts)
    k_idx = jnp.arange(page)
    dist_far = jnp.full((ts, page), 2 * page, jnp.int32)
    dist_last = page + t_idx[:, None] - k_idx[None, :]
    dist_new = t_idx[:, None] - k_idx[None, :]
    ok = jnp.stack([dist_far > 0, dist_last > 0, (dist_new >= 0) & (k_idx[None, :] < ts)])
    b3 = _rel_bias(jnp.stack([dist_far, dist_last, dist_new]), rel_table) * LOG2E
    same_head = jnp.eye(nh, dtype=bool)
    b3 = jnp.where(ok[:, None, :, :, None] & same_head[None, :, None, None, :],
                   jnp.transpose(b3, (0, 3, 1, 2))[..., None], MASK_VALUE)
    bias_s = jnp.tile(b3.reshape(3, nh * ts, page * nh), (1, 2, 1))
    half_mask = (jnp.arange(HEAD_DIM)[None, :] < HALF_DIM) == (jnp.arange(2)[:, None] == 0)

    cache_k4 = cache_k.reshape(depth, n_phys, page * nh, HEAD_DIM)
    cache_v4 = cache_v.reshape(depth, n_phys, page * nh, HEAD_DIM)
    n_items = (TOP_K * n_all) // MOE_ITEM_ROWS + N_EXPERTS
    tm_proj = _pick(n_all, 1100, 16)

    outs = {k: [] for k in ("kp", "vp", "ks", "vs", "gp", "gs", "cp", "cs", "ds")}
    for l in range(depth):
        lam_init = 0.8 - 0.6 * math.exp(-0.3 * l)
        lams = (lam_q1[l][None], lam_k1[l][None], lam_q2[l][None], lam_k2[l][None])
        gn_a = diff_norm_g[l][None]

        w_main = jnp.concatenate([w_in[l, :, :5 * gw], w_in[l, :, 5 * gw + GLA_RANK:]], axis=1).astype(BF16)
        w_gate_cols = jnp.pad(w_in[l, :, 5 * gw:5 * gw + GLA_RANK], ((0, 0), (0, LANES - GLA_RANK))).astype(BF16)
        proj = _matmul([xb], w_main, 11 * gw, tm_proj, _pick(11 * gw, 1024, LANES), "in_proj")
        gproj = _matmul([xb], w_gate_cols, LANES, tm_proj, LANES, "gate_proj")
        proj_s = proj[n_prompt:].reshape(bs, tp, 11 * gw)[:, :ts]

        oa_p = _attn_prompt(proj, lams, bias_tiles, gn_a, bp, seq, nh, lam_init)
        q_s = proj_s[..., :gw].reshape(bs, ts, nh, HEAD_DIM) * (HALF_DIM ** -0.5 * LOG2E)
        q2 = (jnp.transpose(q_s, (0, 2, 1, 3))[:, None] * half_mask.astype(F32)[None, :, None, None, :])
        q2 = q2.reshape(bs, 2 * nh * ts, HEAD_DIM).astype(BF16)
        k_s = proj_s[..., gw:2 * gw]
        v_s = proj_s[..., 2 * gw:3 * gw]
        new_rows = lambda a: jnp.pad(a.reshape(bs, ts * nh, HEAD_DIM), ((0, 0), (0, (page - ts) * nh), (0, 0)))
        oa_s = _attn_sample(l, q2, cache_k4, cache_v4, new_rows(k_s), new_rows(v_s), page_table, bias_s, lams, gn_a,
                            lam_init)
        oa_s = jnp.transpose(oa_s.reshape(bs, nh, ts, HEAD_DIM), (0, 2, 1, 3)).reshape(bs, ts, gw)
        oa_s = jnp.pad(oa_s, ((0, 0), (0, tp - ts), (0, 0))).reshape(bs * tp, gw).astype(BF16)
        oa = lax.dynamic_update_slice(oa_p, oa_s, (n_prompt, 0))

        wup = jnp.pad(gla_w_up[l], ((0, LANES - GLA_RANK), (0, 0)))
        wup_hi, wup_lo = _split_bf16(wup)
        gb = gla_b[l][None]
        gn_b = gla_norm_g[l][None]
        s0_s = jnp.transpose(state_gla[l].reshape(bs, nh // 2, 2 * HALF_DIM, HEAD_DIM), (0, 1, 3, 2))
        s0_p = jnp.zeros((bp, nh // 2, HEAD_DIM, LANES), F32)
        ob, sf_p = _gla(proj, gproj, wup_hi, wup_lo, gb, gn_b, s0_p, 0, bp, seq, nh, seq)
        ob, sf_s = _gla(proj, gproj, wup_hi, wup_lo, gb, gn_b, s0_s, n_prompt, bs, tp, nh, ts, prev=(ob,))
        unpair = lambda s: jnp.transpose(s, (0, 1, 3, 2)).reshape(s.shape[0], nh, HALF_DIM, HEAD_DIM)

        bs_t = jnp.transpose(cm_bs[l])
        cw, lg, lb = conv_w[l], cm_ln_g[l][None], cm_ln_b[l][None]
        oc, od, cs_p, _ = _conv_mlp(proj, cw, jnp.zeros((bp, CONV_WIDTH - 1, gw), F32), lg, lb, cm_ws[l], bs_t,
                                    0, bp, seq, nh, seq, False)
        oc, od, cs_s, vd_s = _conv_mlp(proj, cw, state_conv[l], lg, lb, cm_ws[l], bs_t, n_prompt, bs, tp, nh, ts, True,
                                       prev=(oc, od))

        mix = _matmul([oa, ob, oc, od], w_out[l].astype(BF16), d, tm_proj, _pick(d, 1024, LANES), "out_proj")
        wr = jnp.pad(jnp.concatenate([router_group[l], router_expert[l]], axis=1),
                     ((0, 0), (0, LANES - N_GROUPS - N_EXPERTS)))
        wr_hi, wr_lo = _split_bf16(wr)
        x1, ridx, rgate = _ln_router(x_parts, mix, ln1_g[l][None], ln1_b[l][None], wr_hi, wr_lo, alpha)

        dest, item_e, item_n, n_used = _route_plan(ridx, n_items)
        xs = _dispatch(x1, dest, n_items * MOE_ITEM_ROWS)
        ys = _experts(xs, item_e, item_n, n_used, w_gate, w_up, w_down, l, n_items)
        x, xb = _combine(x1, rgate, dest, ys, ln2_g[l][None], ln2_b[l][None], alpha)
        x_parts = (x, None)

        heads = lambda a, n, t: a.reshape(n, t, nh, HEAD_DIM)
        outs["kp"].append(heads(proj[:n_prompt, gw:2 * gw], bp, seq))
        outs["vp"].append(heads(proj[:n_prompt, 2 * gw:3 * gw], bp, seq))
        outs["ks"].append(heads(k_s, bs, ts))
        outs["vs"].append(heads(v_s, bs, ts))
        outs["gp"].append(unpair(sf_p))
        outs["gs"].append(unpair(sf_s))
        outs["cp"].append(cs_p)
        outs["cs"].append(cs_s)
        outs["ds"].append(vd_s.reshape(bs, tp, gw)[:, :ts])

    y_prompt = x[:n_prompt].reshape(bp, seq, d)
    y_sample = x[n_prompt:].reshape(bs, tp, d)[:, :ts]
    st = lambda k: jnp.stack(outs[k])
    return (y_prompt, y_sample, st("kp"), st("vp"), st("ks"), st("vs"), st("gp"), st("gs"),
            st("cp"), st("cs"), st("ds"))
```

```python
import functools
import math

import jax
import jax.numpy as jnp
from jax import lax
from jax.experimental import pallas as pl
from jax.experimental.pallas import tpu as pltpu

F32 = jnp.float32
BF16 = jnp.bfloat16

HEAD_DIM = 128
HALF_DIM = HEAD_DIM // 2
GLA_RANK = 16
GLA_TAU = 16.0
GLA_CHUNK = 64
CONV_WIDTH = 3
CHUNK_D = 128
REL_BUCKETS = 32
REL_MAX_DIST = 128
N_GROUPS = 4
EXPERTS_PER_GROUP = 8
N_EXPERTS = N_GROUPS * EXPERTS_PER_GROUP
TOP_K = 2
NORM_EPS = 1e-5
MASK_VALUE = -1e30
LOG2E = 1.4426950408889634

LANES = 128
SAMPLE_ROWS = 64
ATTN_TILE = 256
SAMPLE_PAGES_PER_STEP = 8
MOE_ITEM_ROWS = 768
MOE_SUB_ROWS = 128
MOE_HIDDEN_TILE = 256
MOE_DOWN_STEPS = 8
VMEM_LIMIT = 52 * 1024 * 1024


def _pick(n, target, mult):
    if n <= target:
        return n
    best = None
    for d in range(mult, target + 1, mult):
        if n % d == 0:
            best = d
    assert best is not None, (n, target, mult)
    return best


def _params(*sem):
    return pltpu.CompilerParams(dimension_semantics=sem, vmem_limit_bytes=VMEM_LIMIT)


def _dot(a, b):
    return jnp.dot(a, b, preferred_element_type=F32)


def _dot_nt(a, b):
    return lax.dot_general(a, b, (((1,), (1,)), ((), ())), preferred_element_type=F32)


def _dot_tn(a, b):
    return lax.dot_general(a, b, (((0,), (0,)), ((), ())), preferred_element_type=F32)


def _split_bf16(x):
    hi = x.astype(BF16)
    lo = (x - hi.astype(F32)).astype(BF16)
    return hi, lo


def _sigmoid(x):
    return 1.0 / (1.0 + jnp.exp(-x))


def _mm_kernel(*refs, n_in, n_side):
    w_side = refs[2 * n_in:2 * n_in + n_side]
    o_ref = refs[2 * n_in + n_side]
    o_side = refs[2 * n_in + n_side + 1:]
    acc = None
    for x_ref, w_ref in zip(refs[:n_in], refs[n_in:2 * n_in]):
        d = _dot(x_ref[...], w_ref[...])
        acc = d if acc is None else acc + d
    o_ref[...] = acc.astype(o_ref.dtype)

    if n_side:
        @pl.when(pl.program_id(1) == 0)
        def _():
            o_side[0][...] = _dot(refs[0][...], w_side[0][...])


def _matmul(xs, w, n_out, tm, tn, name, w_side=None):
    n_in = len(xs)
    n_side = 0 if w_side is None else 1
    m, kg = xs[0].shape
    grid = (m // tm, n_out // tn)
    in_specs = [pl.BlockSpec((tm, kg), lambda i, j: (i, 0)) for _ in xs]
    in_specs += [pl.BlockSpec((kg, tn), functools.partial(lambda i, j, g: (g, j), g=g)) for g in range(n_in)]
    out_specs = [pl.BlockSpec((tm, tn), lambda i, j: (i, j))]
    out_shape = [jax.ShapeDtypeStruct((m, n_out), F32)]
    side_args = ()
    if n_side:
        in_specs.append(pl.BlockSpec((kg, LANES), lambda i, j: (0, 0)))
        out_specs.append(pl.BlockSpec((tm, LANES), lambda i, j: (i, 0)))
        out_shape.append(jax.ShapeDtypeStruct((m, LANES), F32))
        side_args = (w_side,)
    out = pl.pallas_call(
        functools.partial(_mm_kernel, n_in=n_in, n_side=n_side),
        grid=grid,
        in_specs=in_specs,
        out_specs=out_specs,
        out_shape=out_shape,
        compiler_params=_params("parallel", "arbitrary"),
        name=name,
    )(*xs, *([w] * n_in), *side_args)
    return out if n_side else out[0]


def _rel_bias(dist, rel_table):
    n = jnp.maximum(dist, 0)
    max_exact = REL_BUCKETS // 2
    nf = jnp.maximum(n, 1).astype(F32)
    large = max_exact + (jnp.log(nf / max_exact) / math.log(REL_MAX_DIST / max_exact)
                         * (REL_BUCKETS - max_exact)).astype(jnp.int32)
    bucket = jnp.where(n < max_exact, n, jnp.minimum(large, REL_BUCKETS - 1))
    onehot = (bucket[..., None] == jnp.arange(REL_BUCKETS, dtype=bucket.dtype)).astype(F32)
    return jnp.einsum("...k,kh->...h", onehot, rel_table.astype(F32), precision=lax.Precision.HIGHEST)


def _prompt_bias_tiles(rel_table, t):
    r = jnp.arange(t)[:, None]
    c = jnp.arange(t)[None, :]
    tiles = []
    for delta in range(3):
        dist = delta * t + r - c
        b = _rel_bias(dist, rel_table) * LOG2E
        tiles.append(jnp.where((dist >= 0)[..., None], b, MASK_VALUE))
    return jnp.transpose(jnp.stack(tiles), (3, 0, 1, 2))


def _lam_value(lq1, lk1, lq2, lk2, lam_init):
    return (jnp.exp(jnp.sum(lq1[...] * lk1[...], axis=1, keepdims=True))
            - jnp.exp(jnp.sum(lq2[...] * lk2[...], axis=1, keepdims=True)) + lam_init)


def _rms_head(o, gain, post):
    return o * lax.rsqrt(jnp.mean(o * o, axis=1, keepdims=True) + NORM_EPS) * gain * post


def _attn_prompt_kernel(lq1, lk1, lq2, lk2, q_ref, k_ref, v_ref, bias_ref, gn_ref, o_ref, *, t, lam_init):
    qi = pl.program_id(2)
    lam = _lam_value(lq1, lk1, lq2, lk2, lam_init)
    lane = lax.broadcasted_iota(jnp.int32, (t, HEAD_DIM), 1)
    heads = []
    for h in range(2):
        q = q_ref[:, h * HEAD_DIM:(h + 1) * HEAD_DIM] * (HALF_DIM ** -0.5 * LOG2E)
        heads.append(jnp.concatenate([jnp.where(lane < HALF_DIM, q, 0.0), jnp.where(lane >= HALF_DIM, q, 0.0)],
                                     axis=0).astype(BF16))

    def body(j, carry):
        start = pl.multiple_of(j * t, t)
        tile = jnp.minimum(qi - j, 2)
        out = []
        for h in range(2):
            m, s, acc = carry[h]
            cols = slice(h * HEAD_DIM, (h + 1) * HEAD_DIM)
            kt = k_ref[pl.ds(start, t), cols].astype(BF16)
            vt = v_ref[pl.ds(start, t), cols].astype(BF16)
            bt = bias_ref[h, tile]
            l = _dot_nt(heads[h], kt) + jnp.concatenate([bt, bt], axis=0)
            m_new = jnp.maximum(m, jnp.max(l, axis=1, keepdims=True))
            a = jnp.exp2(m - m_new)
            p = jnp.exp2(l - m_new)
            out.append((m_new, a * s + jnp.sum(p, axis=1, keepdims=True), a * acc + _dot(p.astype(BF16), vt)))
        return tuple(out)

    init = (jnp.full((2 * t, 1), MASK_VALUE, F32), jnp.zeros((2 * t, 1), F32), jnp.zeros((2 * t, HEAD_DIM), F32))
    res = lax.fori_loop(0, qi + 1, body, (init, init))
    for h in range(2):
        _, s, acc = res[h]
        o = acc / s
        o = o[:t] - lam * o[t:]
        o_ref[:, h * HEAD_DIM:(h + 1) * HEAD_DIM] = _rms_head(o, gn_ref[...], 1.0 - lam_init).astype(o_ref.dtype)


def _attn_prompt(proj, lams, bias_tiles, gn, n_seq, seq, n_heads, lam_init):
    t = ATTN_TILE
    assert seq % t == 0 and t >= REL_MAX_DIST and n_heads % 2 == 0
    nq = seq // t
    pair = 2 * HEAD_DIM
    lam_spec = pl.BlockSpec((1, HALF_DIM), lambda b, h, i: (0, 0))
    return pl.pallas_call(
        functools.partial(_attn_prompt_kernel, t=t, lam_init=lam_init),
        grid=(n_seq, n_heads // 2, nq),
        in_specs=[lam_spec] * 4 + [
            pl.BlockSpec((t, pair), lambda b, h, i: (b * nq + i, h)),
            pl.BlockSpec((seq, pair), lambda b, h, i: (b, n_heads // 2 + h)),
            pl.BlockSpec((seq, pair), lambda b, h, i: (b, n_heads + h)),
            pl.BlockSpec((2, 3, t, t), lambda b, h, i: (h, 0, 0, 0)),
            pl.BlockSpec((1, HEAD_DIM), lambda b, h, i: (0, 0)),
        ],
        out_specs=pl.BlockSpec((t, pair), lambda b, h, i: (b * nq + i, h)),
        out_shape=jax.ShapeDtypeStruct((proj.shape[0], n_heads * HEAD_DIM), BF16),
        compiler_params=_params("parallel", "parallel", "arbitrary"),
        name="attn_prompt",
    )(*lams, proj, proj, proj, bias_tiles, gn)


def _attn_sample_kernel(pt_ref, lq1, lk1, lq2, lk2, q_ref, *refs, n_pp, lam_init):
    kc_refs, vc_refs = refs[:n_pp], refs[n_pp:2 * n_pp]
    kn_ref, vn_ref, bias_ref, gn_ref, o_ref, m_ref, s_ref, acc_ref = refs[2 * n_pp:]
    p = pl.program_id(1)
    n_steps = pl.num_programs(1)

    @pl.when(p == 0)
    def _():
        m_ref[...] = jnp.full(m_ref.shape, MASK_VALUE, F32)
        s_ref[...] = jnp.zeros(s_ref.shape, F32)
        acc_ref[...] = jnp.zeros(acc_ref.shape, F32)

    q = q_ref[0]

    def step(ks, vs, biases):
        ls = [_dot_nt(q, k.astype(BF16)) + b for k, b in zip(ks, biases)]
        m = m_ref[...]
        m_new = m
        for l in ls:
            m_new = jnp.maximum(m_new, jnp.max(l, axis=1, keepdims=True))
        a = jnp.exp2(m - m_new)
        s = a * s_ref[...]
        acc = a * acc_ref[...]
        for l, v in zip(ls, vs):
            pr = jnp.exp2(l - m_new)
            s = s + jnp.sum(pr, axis=1, keepdims=True)
            acc = acc + _dot(pr.astype(BF16), v.astype(BF16))
        s_ref[...] = s
        acc_ref[...] = acc
        m_ref[...] = m_new

    last = p == n_steps - 1
    tables = [0] * (n_pp - 1) + [jnp.where(last, 1, 0)]
    step([r[0, 0] for r in kc_refs], [r[0, 0] for r in vc_refs], [bias_ref[t] for t in tables])

    @pl.when(last)
    def _():
        step([kn_ref[0]], [vn_ref[0]], [bias_ref[2]])
        lam = _lam_value(lq1, lk1, lq2, lk2, lam_init)
        o = acc_ref[...] / s_ref[...]
        half = o.shape[0] // 2
        o_ref[0] = _rms_head(o[:half] - lam * o[half:], gn_ref[...], 1.0 - lam_init)


def _attn_sample(layer, q2, cache_k, cache_v, k_new, v_new, page_table, bias, lams, gn, lam_init):
    n_seq, n_pages = page_table.shape
    prow = cache_k.shape[2]
    rows = q2.shape[1]
    n_pp = SAMPLE_PAGES_PER_STEP if n_pages % SAMPLE_PAGES_PER_STEP == 0 else 1
    lam_spec = pl.BlockSpec((1, HALF_DIM), lambda b, p, pt: (0, 0))
    cache_specs = [pl.BlockSpec((1, 1, prow, HEAD_DIM),
                                functools.partial(lambda b, p, pt, i: (layer, pt[b, p * n_pp + i], 0, 0), i=i))
                   for i in range(n_pp)]
    new_spec = pl.BlockSpec((1, prow, HEAD_DIM), lambda b, p, pt: (b, 0, 0))
    grid_spec = pltpu.PrefetchScalarGridSpec(
        num_scalar_prefetch=1,
        grid=(n_seq, n_pages // n_pp),
        in_specs=[lam_spec] * 4 + [pl.BlockSpec((1, rows, HEAD_DIM), lambda b, p, pt: (b, 0, 0))]
        + cache_specs + cache_specs + [
            new_spec, new_spec,
            pl.BlockSpec((3, rows, prow), lambda b, p, pt: (0, 0, 0)),
            pl.BlockSpec((1, HEAD_DIM), lambda b, p, pt: (0, 0)),
        ],
        out_specs=pl.BlockSpec((1, rows // 2, HEAD_DIM), lambda b, p, pt: (b, 0, 0)),
        scratch_shapes=[pltpu.VMEM((rows, 1), F32), pltpu.VMEM((rows, 1), F32), pltpu.VMEM((rows, HEAD_DIM), F32)],
    )
    return pl.pallas_call(
        functools.partial(_attn_sample_kernel, n_pp=n_pp, lam_init=lam_init),
        grid_spec=grid_spec,
        out_shape=jax.ShapeDtypeStruct((n_seq, rows // 2, HEAD_DIM), F32),
        compiler_params=_params("parallel", "arbitrary"),
        name="attn_sample",
    )(page_table, *lams, q2, *([cache_k] * n_pp), *([cache_v] * n_pp), k_new, v_new, bias, gn)


def _gla_kernel(q_ref, k_ref, v_ref, r_ref, gp_ref, wh_ref, wl_ref, gb_ref, gn_ref, s0_ref, *rest,
                chunk, n_chunks, n_heads, valid):
    o_ref, sfin_ref, st_ref = rest[-3:]
    t = pl.program_id(1)

    @pl.when(t == 0)
    def _():
        st_ref[...] = s0_ref[0]

    half = chunk // 2
    quarter = chunk // 4
    row = lax.broadcasted_iota(jnp.int32, (chunk, chunk), 0)
    col = lax.broadcasted_iota(jnp.int32, (chunk, chunk), 1)
    tril = row >= col
    same_half = (row < half) == (col < half)
    tri_bf = jnp.where(tril, 1.0, 0.0).astype(BF16)
    lane = lax.broadcasted_iota(jnp.int32, (chunk, LANES), 1)
    lane_sq = lax.broadcasted_iota(jnp.int32, (HEAD_DIM, LANES), 1)

    for c in range(n_chunks):
        rs = slice(c * chunk, (c + 1) * chunk)
        gh, gl = _split_bf16(gp_ref[rs, :])
        z = _dot(gh, wh_ref[...]) + _dot(gl, wh_ref[...]) + _dot(gh, wl_ref[...]) + gb_ref[...]
        g = -(jnp.maximum(-z, 0.0) + jnp.log(1.0 + jnp.exp(-jnp.abs(z)))) * (1.0 / GLA_TAU)
        k = k_ref[rs, :]
        if valid < chunk:
            ok = lax.broadcasted_iota(jnp.int32, g.shape, 0) < valid
            g = jnp.where(ok, g, 0.0)
            k = jnp.where(ok, k, 0.0)
        gh, gl = _split_bf16(g)
        b = _dot(tri_bf, gh) + _dot(tri_bf, gl)
        b_last = b[chunk - 1:chunk, :]
        b_half = b[half - 1:half, :]
        in_lo = lax.broadcasted_iota(jnp.int32, b.shape, 0) < half
        b_near = jnp.where(in_lo, b[quarter - 1:quarter, :], b[half + quarter - 1:half + quarter, :])
        q = q_ref[rs, :] * (HALF_DIM ** -0.5)
        q_state = q * jnp.exp(b)
        q_near = q * jnp.exp(b - b_near)
        k_near = (k * jnp.exp(b_near - b)).astype(BF16)
        q_far = q * jnp.exp(jnp.minimum(b - b_half, 0.0))
        k_far = (k * jnp.exp(jnp.minimum(b_half - b, 0.0))).astype(BF16)
        k_end = (k * jnp.exp(b_last - b)).astype(BF16)
        decay = jnp.exp(b_last)

        for hp in range(n_heads // 2):
            ls = slice(hp * LANES, (hp + 1) * LANES)
            st = st_ref[hp]
            st_bf = st.astype(BF16)
            upd = []
            for par in range(2):
                h = 2 * hp + par
                mine = (lane < HALF_DIM) if par == 0 else (lane >= HALF_DIM)
                hs = slice(h * HEAD_DIM, (h + 1) * HEAD_DIM)
                v_h = v_ref[rs, hs].astype(BF16)
                att = jnp.where(same_half,
                                _dot_nt(jnp.where(mine, q_near[:, ls], 0.0).astype(BF16), k_near[:, ls]),
                                _dot_nt(jnp.where(mine, q_far[:, ls], 0.0).astype(BF16), k_far[:, ls]))
                att = jnp.where(tril, att, 0.0).astype(BF16)
                o = _dot(att, v_h) + _dot_nt(jnp.where(mine, q_state[:, ls], 0.0).astype(BF16), st_bf)
                r_h = r_ref[rs, hs]
                o_ref[rs, hs] = _rms_head(o, gn_ref[...], r_h * _sigmoid(r_h)).astype(o_ref.dtype)
                upd.append(_dot_tn(v_h, k_end[:, ls]))
            st_ref[hp] = st * decay[:, ls] + jnp.where(lane_sq < HALF_DIM, upd[0], upd[1])

    @pl.when(t == pl.num_programs(1) - 1)
    def _():
        sfin_ref[0] = st_ref[...]


def _into(prev):
    specs = [pl.BlockSpec(memory_space=pl.ANY) for _ in prev]
    return specs, tuple(prev)


def _gla(proj, gproj, wup_hi, wup_lo, gla_b, gn, s0, row0, n_seq, seq, n_heads, valid, prev=()):
    gw = n_heads * HEAD_DIM
    hk = n_heads * HALF_DIM
    chunk = min(GLA_CHUNK, seq)
    tt = _pick(seq, 256, chunk)
    nt = seq // tt
    r0 = row0 // tt
    assert row0 % tt == 0 and chunk % 4 == 0
    qcol = 3 * gw // hk
    row_spec = lambda width, colblk: pl.BlockSpec((tt, width), lambda b, t: (r0 + b * nt + t, colblk))
    const = lambda shape: pl.BlockSpec(shape, lambda b, t: tuple(0 for _ in shape))
    st_spec = pl.BlockSpec((1, n_heads // 2, HEAD_DIM, LANES), lambda b, t: (b, 0, 0, 0))
    prev_specs, prev_args = _into(prev)
    n_in = 10
    return pl.pallas_call(
        functools.partial(_gla_kernel, chunk=chunk, n_chunks=tt // chunk, n_heads=n_heads, valid=valid),
        grid=(n_seq, nt),
        in_specs=[row_spec(hk, qcol), row_spec(hk, qcol + 1), row_spec(gw, 4), row_spec(gw, 5),
                  pl.BlockSpec((tt, LANES), lambda b, t: (r0 + b * nt + t, 0)),
                  const((LANES, hk)), const((LANES, hk)), const((1, hk)), const((1, HEAD_DIM)), st_spec] + prev_specs,
        out_specs=[pl.BlockSpec((tt, gw), lambda b, t: (r0 + b * nt + t, 0)), st_spec],
        out_shape=[jax.ShapeDtypeStruct((proj.shape[0], gw), BF16),
                   jax.ShapeDtypeStruct((n_seq, n_heads // 2, HEAD_DIM, LANES), F32)],
        input_output_aliases={n_in + k: k for k in range(len(prev))},
        scratch_shapes=[pltpu.VMEM((n_heads // 2, HEAD_DIM, LANES), F32)],
        compiler_params=_params("parallel", "arbitrary"),
        name="gla",
    )(proj, proj, proj, proj, gproj, wup_hi, wup_lo, gla_b, gn, s0, *prev_args)


def _conv_mlp_kernel(cb_ref, cc_ref, ch_ref, du_ref, dv_ref, cw_ref, buf_ref, lg_ref, lb_ref, ws_ref, bs_ref, *rest,
                     tt, chunk, n_heads, valid):
    oc_ref, od_ref, cs_ref, vd_ref, carry_ref = rest[-5:]
    t = pl.program_id(1)

    @pl.when(t == 0)
    def _():
        carry_ref[...] = buf_ref[0]

    z = cc_ref[...] * ch_ref[...]
    row = lax.broadcasted_iota(jnp.int32, z.shape, 0)
    c0 = carry_ref[0:1, :]
    c1 = carry_ref[1:2, :]
    z1 = jnp.where(row == 0, c1, pltpu.roll(z, 1, 0))
    z2 = jnp.where(row == 0, c0, jnp.where(row == 1, c1, pltpu.roll(z, 2, 0)))
    w = cw_ref[...]
    oc_ref[...] = (cb_ref[...] * (z2 * w[0:1, :] + z1 * w[1:2, :] + z * w[2:3, :])).astype(oc_ref.dtype)
    if valid >= 2:
        carry_ref[...] = z[valid - 2:valid, :]
    else:
        carry_ref[...] = jnp.concatenate([c1, z[0:1, :]], axis=0)

    @pl.when(t == pl.num_programs(1) - 1)
    def _():
        cs_ref[0] = carry_ref[...]

    x = dv_ref[...]
    mu = jnp.mean(x, axis=1, keepdims=True)
    xc = x - mu
    var = jnp.mean(xc * xc, axis=1, keepdims=True)
    vd = xc * lax.rsqrt(var + NORM_EPS) * lg_ref[...] + lb_ref[...]
    vd_ref[...] = vd
    vd_bf = vd.astype(BF16)
    r2 = lax.broadcasted_iota(jnp.int32, (chunk, chunk), 0)
    c2 = lax.broadcasted_iota(jnp.int32, (chunk, chunk), 1)
    for g in range(n_heads):
        wc = jnp.where(r2 >= c2, ws_ref[g, :chunk, :chunk], 0.0).astype(BF16)
        bias = bs_ref[:chunk, g:g + 1]
        cols = slice(g * HEAD_DIM, (g + 1) * HEAD_DIM)
        for c in range(tt // chunk):
            rs = slice(c * chunk, (c + 1) * chunk)
            zc = _dot(wc, vd_bf[rs, cols]) + bias
            od_ref[rs, cols] = (du_ref[rs, cols] * zc).astype(od_ref.dtype)


def _conv_mlp(proj, conv_w, buf, ln_g, ln_b, ws, bs_t, row0, n_seq, seq, n_heads, valid, keep_vd, prev=()):
    gw = n_heads * HEAD_DIM
    chunk = min(CHUNK_D, seq)
    tt = _pick(seq, 256, chunk)
    nt = seq // tt
    r0 = row0 // tt
    assert row0 % tt == 0 and (valid == seq or nt == 1)
    row_spec = lambda colblk: pl.BlockSpec((tt, gw), lambda b, t: (r0 + b * nt + t, colblk))
    const = lambda shape: pl.BlockSpec(shape, lambda b, t: tuple(0 for _ in shape))
    out_rows = pl.BlockSpec((tt, gw), lambda b, t: (r0 + b * nt + t, 0))
    vd_spec = pl.BlockSpec((tt, gw), (lambda b, t: (b * nt + t, 0)) if keep_vd else (lambda b, t: (0, 0)))
    state_spec = pl.BlockSpec((1, CONV_WIDTH - 1, gw), lambda b, t: (b, 0, 0))
    prev_specs, prev_args = _into(prev)
    n_in = 11
    return pl.pallas_call(
        functools.partial(_conv_mlp_kernel, tt=tt, chunk=chunk, n_heads=n_heads, valid=min(valid, tt)),
        grid=(n_seq, nt),
        in_specs=[row_spec(6), row_spec(7), row_spec(8), row_spec(9), row_spec(10),
                  const((CONV_WIDTH, gw)), state_spec, const((1, gw)), const((1, gw)),
                  const((n_heads, CHUNK_D, CHUNK_D)), const((CHUNK_D, n_heads))] + prev_specs,
        out_specs=[out_rows, out_rows, state_spec, vd_spec],
        out_shape=[jax.ShapeDtypeStruct((proj.shape[0], gw), BF16), jax.ShapeDtypeStruct((proj.shape[0], gw), BF16),
                   jax.ShapeDtypeStruct((n_seq, CONV_WIDTH - 1, gw), F32),
                   jax.ShapeDtypeStruct((n_seq * seq if keep_vd else tt, gw), F32)],
        input_output_aliases={n_in + k: k for k in range(len(prev))},
        scratch_shapes=[pltpu.VMEM((CONV_WIDTH - 1, gw), F32)],
        compiler_params=_params("parallel", "arbitrary"),
        name="conv_mlp",
    )(proj, proj, proj, proj, proj, conv_w, buf, ln_g, ln_b, ws, bs_t, *prev_args)


def _layer_norm_rows(x, g, b):
    mu = jnp.mean(x, axis=1, keepdims=True)
    xc = x - mu
    var = jnp.mean(xc * xc, axis=1, keepdims=True)
    return xc * lax.rsqrt(var + NORM_EPS) * g + b


def _ln_router_kernel(xa_ref, xb_ref, mix_ref, g_ref, b_ref, wh_ref, wl_ref, y_ref, idx_ref, gate_ref, *, alpha, split):
    x = jnp.where(pl.program_id(0) < split, xa_ref[...], xb_ref[...])
    y = _layer_norm_rows(alpha * x + mix_ref[...], g_ref[...], b_ref[...])
    y_ref[...] = y
    yh, yl = _split_bf16(y)
    lg = _dot(yh, wh_ref[...]) + _dot(yl, wh_ref[...]) + _dot(yh, wl_ref[...])
    lane = lax.broadcasted_iota(jnp.int32, lg.shape, 1)
    neg = -jnp.inf
    glog = jnp.where(lane < N_GROUPS, lg, neg)
    gmax = jnp.max(glog, axis=1, keepdims=True)
    g_star = jnp.min(jnp.where(glog == gmax, lane, LANES), axis=1, keepdims=True)
    p_top = 1.0 / jnp.sum(jnp.exp(glog - gmax), axis=1, keepdims=True)
    lo = N_GROUPS + EXPERTS_PER_GROUP * g_star
    w1 = jnp.where((lane >= lo) & (lane < lo + EXPERTS_PER_GROUP), lg, neg)
    v1 = jnp.max(w1, axis=1, keepdims=True)
    i1 = jnp.min(jnp.where(w1 == v1, lane, LANES), axis=1, keepdims=True)
    w2 = jnp.where(lane == i1, neg, w1)
    v2 = jnp.max(w2, axis=1, keepdims=True)
    i2 = jnp.min(jnp.where(w2 == v2, lane, LANES), axis=1, keepdims=True)
    e21 = jnp.exp(v2 - v1)
    gate1 = p_top / (1.0 + e21)
    gate2 = p_top * e21 / (1.0 + e21)
    idx_ref[...] = jnp.where(lane == 0, i1 - N_GROUPS, jnp.where(lane == 1, i2 - N_GROUPS, 0))
    gate_ref[...] = jnp.where(lane == 0, gate1, jnp.where(lane == 1, gate2, 0.0))


def _ln_router(x_parts, mix, g, b, wr_hi, wr_lo, alpha):
    m, d = mix.shape
    tm = _pick(math.gcd(m, x_parts[0].shape[0]), 256, 8)
    xa, xb = x_parts if x_parts[1] is not None else (x_parts[0], x_parts[0])
    split = xa.shape[0] // tm if x_parts[1] is not None else m // tm
    rows = pl.BlockSpec((tm, d), lambda i: (i, 0))
    const = lambda shape: pl.BlockSpec(shape, lambda i: (0, 0))
    small = pl.BlockSpec((tm, LANES), lambda i: (i, 0))
    return pl.pallas_call(
        functools.partial(_ln_router_kernel, alpha=alpha, split=split),
        grid=(m // tm,),
        in_specs=[pl.BlockSpec((tm, d), lambda i: (jnp.minimum(i, split - 1), 0)),
                  pl.BlockSpec((tm, d), lambda i: (jnp.maximum(i - split, 0), 0)),
                  rows, const((1, d)), const((1, d)), const((d, LANES)), const((d, LANES))],
        out_specs=[rows, small, small],
        out_shape=[jax.ShapeDtypeStruct((m, d), F32),
                   jax.ShapeDtypeStruct((m, LANES), jnp.int32), jax.ShapeDtypeStruct((m, LANES), F32)],
        compiler_params=_params("parallel"),
        name="ln_router",
    )(xa, xb, mix, g, b, wr_hi, wr_lo)


SLAB_TILE = 8


def _token_copy(src, src_row, dst, dst_row, sem):
    rows = lambda r: pl.ds(pl.multiple_of(r * SLAB_TILE, SLAB_TILE), SLAB_TILE)
    return pltpu.make_async_copy(src.at[:, rows(src_row), :], dst.at[:, rows(dst_row), :], sem)


def _dispatch_kernel(dest_ref, x_ref, xs_hbm, stage, sem, *, tm):
    for p in range(stage.shape[0]):
        for c in range(SLAB_TILE):
            col = (p * SLAB_TILE + c) * LANES
            stage[p, pl.ds(c, tm, stride=SLAB_TILE), :] = x_ref[:, col:col + LANES]

    def copy(i, k):
        return _token_copy(stage, i, xs_hbm, dest_ref[0, 0, 2 * i + k], sem)

    def start(i, _):
        copy(i, 0).start()
        copy(i, 1).start()
        return 0

    def wait(i, _):
        copy(i, 0).wait()
        copy(i, 1).wait()
        return 0

    lax.fori_loop(0, tm, start, 0)
    lax.fori_loop(0, tm, wait, 0)


def _dispatch(x1, dest, n_rows):
    m, d = x1.shape
    planes = d // (SLAB_TILE * LANES)
    tm = _pick(m, 256, 8)
    dest3 = dest.reshape(m // tm, 1, TOP_K * tm)
    return pl.pallas_call(
        functools.partial(_dispatch_kernel, tm=tm),
        grid=(m // tm,),
        in_specs=[pl.BlockSpec((1, 1, TOP_K * tm), lambda i: (i, 0, 0), memory_space=pltpu.SMEM),
                  pl.BlockSpec((tm, d), lambda i: (i, 0))],
        out_specs=pl.BlockSpec(memory_space=pl.ANY),
        out_shape=jax.ShapeDtypeStruct((planes, n_rows * SLAB_TILE, LANES), F32),
        scratch_shapes=[pltpu.VMEM((planes, tm * SLAB_TILE, LANES), F32), pltpu.SemaphoreType.DMA(())],
        compiler_params=pltpu.CompilerParams(dimension_semantics=("arbitrary",), has_side_effects=True,
                                             vmem_limit_bytes=VMEM_LIMIT),
        name="moe_dispatch",
    )(dest3, x1)


def _expert_kernel(ie_ref, in_ref, nu_ref, x_ref, wg_ref, wu_ref, wd_ref, y_ref, xb, h_buf, *, planes, hs, sub, tiles):
    i = pl.program_id(0)
    s = pl.program_id(1)
    n_sub = (in_ref[i] + sub - 1) // sub
    live = i < nu_ref[0]
    per_blk = SLAB_TILE // tiles
    up0 = planes
    down0 = planes + hs
    kp = SLAB_TILE * LANES
    ht = h_buf.shape[2]

    def for_rows(fn):
        for k in range(1, xb.shape[1] // sub + 1):
            pl.when(n_sub == k)(functools.partial(fn, k * sub))

    @pl.when(live & (s < up0))
    def _():
        def convert(m):
            xb[s, :m, :] = jnp.concatenate([x_ref[pl.ds(c, m, stride=SLAB_TILE), :]
                                            for c in range(SLAB_TILE)], axis=1).astype(BF16)
        for_rows(convert)

    @pl.when(live & (s >= up0) & (s < down0))
    def _():
        wg = wg_ref[0].astype(BF16)
        wu = wu_ref[0].astype(BF16)

        def up(m):
            g = sum(_dot(xb[p, :m, :], wg[p * kp:(p + 1) * kp, :]) for p in range(planes))
            u = sum(_dot(xb[p, :m, :], wu[p * kp:(p + 1) * kp, :]) for p in range(planes))
            h_buf[s - up0, :m, :] = (g * _sigmoid(g) * u).astype(BF16)
        for_rows(up)

    @pl.when(live & (s >= down0))
    def _():
        wd = wd_ref[0].astype(BF16)
        c0 = ((s - down0) % per_blk) * tiles

        def down(m):
            y = sum(_dot(h_buf[j, :m, :], wd[j * ht:(j + 1) * ht, :]) for j in range(hs))
            for c in range(tiles):
                y_ref[pl.ds(c0 + c, m, stride=SLAB_TILE), :] = y[:, c * LANES:(c + 1) * LANES]
        for_rows(down)


def _experts(xs, item_e, item_n, n_used, w_gate, w_up, w_down, layer, n_items):
    d, hidden = w_gate.shape[-2:]
    n_exp = w_gate.shape[1]
    planes = xs.shape[0]
    ht = min(MOE_HIDDEN_TILE, hidden)
    hs = hidden // ht
    ns = MOE_DOWN_STEPS
    nt = d // ns
    tiles = nt // LANES
    assert SLAB_TILE % tiles == 0 and hidden % ht == 0
    per_blk = SLAB_TILE // tiles
    n_steps = planes + hs + ns
    wg3 = w_gate.reshape(-1, d, hidden)
    wu3 = w_up.reshape(-1, d, hidden)
    wd3 = w_down.reshape(-1, hidden, d)
    e0 = layer * n_exp
    rows = MOE_ITEM_ROWS

    def item(i, nu):
        return jnp.minimum(i, nu[0] - 1)

    def step(i, s, nu):
        return jnp.where(i < nu[0], s, n_steps - 1)

    def expert(i, ie, nu):
        return e0 + ie[item(i, nu)]

    def x_plane(i, s, ie, inn, nu):
        return jnp.minimum(step(i, s, nu), planes - 1), item(i, nu), 0

    def up_slice(i, s, ie, inn, nu):
        return expert(i, ie, nu), 0, jnp.clip(step(i, s, nu) - planes, 0, hs - 1)

    def down_slice(i, s, ie, inn, nu):
        return expert(i, ie, nu), 0, jnp.maximum(step(i, s, nu) - planes - hs, 0)

    def y_plane(i, s, ie, inn, nu):
        return jnp.maximum(step(i, s, nu) - planes - hs, 0) // per_blk, item(i, nu), 0

    grid_spec = pltpu.PrefetchScalarGridSpec(
        num_scalar_prefetch=3,
        grid=(n_items, n_steps),
        in_specs=[
            pl.BlockSpec((None, rows * SLAB_TILE, LANES), x_plane),
            pl.BlockSpec((1, d, ht), up_slice),
            pl.BlockSpec((1, d, ht), up_slice),
            pl.BlockSpec((1, hidden, nt), down_slice),
        ],
        out_specs=pl.BlockSpec((None, rows * SLAB_TILE, LANES), y_plane),
        scratch_shapes=[pltpu.VMEM((planes, rows, SLAB_TILE * LANES), BF16), pltpu.VMEM((hs, rows, ht), BF16)],
    )
    return pl.pallas_call(
        functools.partial(_expert_kernel, planes=planes, hs=hs, sub=MOE_SUB_ROWS, tiles=tiles),
        grid_spec=grid_spec,
        out_shape=jax.ShapeDtypeStruct(xs.shape, F32),
        compiler_params=_params("arbitrary", "arbitrary"),
        name="moe_experts",
    )(item_e, item_n, n_used, xs, wg3, wu3, wd3)


def _combine_kernel(dest_ref, next_ref, x_ref, gate_ref, g_ref, b_ref, ys_hbm, y_ref, yb_ref, stage0, stage1, sems,
                    *, tm, alpha):
    i = pl.program_id(0)
    slot = i % 2
    stages = (stage0, stage1)

    def copy(dests, t, k, sl):
        return _token_copy(ys_hbm, dests[0, 0, 2 * t + k], stages[k].at[sl], t, sems.at[sl])

    def start_tile(dests, sl):
        def start(t, _):
            copy(dests, t, 0, sl).start()
            copy(dests, t, 1, sl).start()
            return 0
        lax.fori_loop(0, tm, start, 0)

    @pl.when(i == 0)
    def _():
        start_tile(dest_ref, slot)

    @pl.when(i + 1 < pl.num_programs(0))
    def _():
        start_tile(next_ref, 1 - slot)

    def wait(t, _):
        copy(dest_ref, t, 0, slot).wait()
        copy(dest_ref, t, 1, slot).wait()
        return 0

    lax.fori_loop(0, tm, wait, 0)
    gates = gate_ref[...]
    g0 = gates[:, 0:1]
    g1 = gates[:, 1:2]
    tile = lambda stage, p, c: stage[slot, p, pl.ds(c, tm, stride=SLAB_TILE), :]
    ffn = jnp.concatenate([g0 * tile(stage0, p, c) + g1 * tile(stage1, p, c)
                           for p in range(stage0.shape[1]) for c in range(SLAB_TILE)], axis=1)
    y = _layer_norm_rows(alpha * x_ref[...] + ffn, g_ref[...], b_ref[...])
    y_ref[...] = y
    yb_ref[...] = y.astype(BF16)


def _combine(x1, gates, dest, ys, g, b, alpha):
    m, d = x1.shape
    tm = _pick(m, 256, 8)
    n_tiles = m // tm
    stage = pltpu.VMEM((2, ys.shape[0], tm * SLAB_TILE, LANES), F32)
    dest3 = dest.reshape(n_tiles, 1, TOP_K * tm)
    dest_spec = lambda off: pl.BlockSpec((1, 1, TOP_K * tm), lambda i: (jnp.minimum(i + off, n_tiles - 1), 0, 0),
                                         memory_space=pltpu.SMEM)
    rows = pl.BlockSpec((tm, d), lambda i: (i, 0))
    const = pl.BlockSpec((1, d), lambda i: (0, 0))
    return pl.pallas_call(
        functools.partial(_combine_kernel, tm=tm, alpha=alpha),
        grid=(n_tiles,),
        in_specs=[dest_spec(0), dest_spec(1), rows, pl.BlockSpec((tm, LANES), lambda i: (i, 0)), const, const,
                  pl.BlockSpec(memory_space=pl.ANY)],
        out_specs=[rows, rows],
        out_shape=[jax.ShapeDtypeStruct((m, d), F32), jax.ShapeDtypeStruct((m, d), BF16)],
        scratch_shapes=[stage, stage, pltpu.SemaphoreType.DMA((2,))],
        compiler_params=_params("arbitrary"),
        name="moe_combine",
    )(dest3, dest3, x1, gates, g, b, ys)


def _route_plan(idx, n_items):
    e_flat = idx[:, :TOP_K].reshape(-1)
    onehot = (e_flat[:, None] == jnp.arange(N_EXPERTS, dtype=jnp.int32)[None, :]).astype(jnp.int32)
    csum = jnp.cumsum(onehot, axis=0)
    rank = jnp.sum(onehot * csum, axis=1) - 1
    counts = csum[-1]
    items_e = (counts + MOE_ITEM_ROWS - 1) // MOE_ITEM_ROWS
    item_end = jnp.cumsum(items_e)
    item_start = item_end - items_e
    dest = (item_start[e_flat] * MOE_ITEM_ROWS + rank).astype(jnp.int32)
    ids = jnp.arange(n_items, dtype=jnp.int32)
    item_e = jnp.minimum(jnp.searchsorted(item_end, ids, side="right"), N_EXPERTS - 1).astype(jnp.int32)
    n_used = item_end[-1].astype(jnp.int32)
    item_n = jnp.clip(counts[item_e] - (ids - item_start[item_e]) * MOE_ITEM_ROWS, 0, MOE_ITEM_ROWS)
    item_n = jnp.where(ids < n_used, item_n, 0).astype(jnp.int32)
    return dest, item_e, item_n, n_used.reshape(1)


def kernel(x_prompt, x_sample, cache_k, cache_v, page_table, state_gla, state_conv, rel_table,
           w_in, w_out, lam_q1, lam_k1, lam_q2, lam_k2, diff_norm_g, gla_w_up, gla_b, gla_norm_g,
           conv_w, cm_ln_g, cm_ln_b, cm_ws, cm_bs, ln1_g, ln1_b, ln2_g, ln2_b,
           router_group, router_expert, w_gate, w_up, w_down):
    bp, seq, d = x_prompt.shape
    bs, ts, _ = x_sample.shape
    depth = w_in.shape[0]
    gw = w_out.shape[1] // 4
    nh = gw // HEAD_DIM
    hk = nh * HALF_DIM
    page = cache_k.shape[2]
    n_phys = cache_k.shape[1]
    tp = SAMPLE_ROWS
    n_prompt = bp * seq
    n_all = n_prompt + bs * tp
    alpha = (2.0 * depth) ** 0.25
    assert ts <= tp and nh % 2 == 0 and n_prompt % tp == 0

    x_parts = (x_prompt.reshape(n_prompt, d), jnp.pad(x_sample, ((0, 0), (0, tp - ts), (0, 0))).reshape(bs * tp, d))
    xb = jnp.concatenate([x_parts[0].astype(BF16), x_parts[1].astype(BF16)], axis=0)

    bias_tiles = _prompt_bias_tiles(rel_table, ATTN_TILE)
    t_idx = jnp.arange(ts)
    k_idx = jnp.arange(page)
    dist_far = jnp.full((ts, page), 2 * page, jnp.int32)
    dist_last = page + t_idx[:, None] - k_idx[None, :]
    dist_new = t_idx[:, None] - k_idx[None, :]
    ok = jnp.stack([dist_far > 0, dist_last > 0, (dist_new >= 0) & (k_idx[None, :] < ts)])
    b3 = _rel_bias(jnp.stack([dist_far, dist_last, dist_new]), rel_table) * LOG2E
    same_head = jnp.eye(nh, dtype=bool)
    b3 = jnp.where(ok[:, None, :, :, None] & same_head[None, :, None, None, :],
                   jnp.transpose(b3, (0, 3, 1, 2))[..., None], MASK_VALUE)
    bias_s = jnp.tile(b3.reshape(3, nh * ts, page * nh), (1, 2, 1))
    half_mask = (jnp.arange(HEAD_DIM)[None, :] < HALF_DIM) == (jnp.arange(2)[:, None] == 0)

    cache_k4 = cache_k.reshape(depth, n_phys, page * nh, HEAD_DIM)
    cache_v4 = cache_v.reshape(depth, n_phys, page * nh, HEAD_DIM)
    n_items = (TOP_K * n_all) // MOE_ITEM_ROWS + N_EXPERTS
    tm_proj = _pick(n_all, 1024, 16)

    outs = {k: [] for k in ("kp", "vp", "ks", "vs", "gp", "gs", "cp", "cs", "ds")}
    for l in range(depth):
        lam_init = 0.8 - 0.6 * math.exp(-0.3 * l)
        lams = (lam_q1[l][None], lam_k1[l][None], lam_q2[l][None], lam_k2[l][None])
        gn_a = diff_norm_g[l][None]

        w_main = jnp.concatenate([w_in[l, :, :5 * gw], w_in[l, :, 5 * gw + GLA_RANK:]], axis=1).astype(BF16)
        w_gate_cols = jnp.pad(w_in[l, :, 5 * gw:5 * gw + GLA_RANK], ((0, 0), (0, LANES - GLA_RANK))).astype(BF16)
        proj, gproj = _matmul([xb], w_main, 11 * gw, tm_proj, _pick(11 * gw, 1024, LANES), "in_proj",
                              w_side=w_gate_cols)
        proj_s = proj[n_prompt:].reshape(bs, tp, 11 * gw)[:, :ts]

        oa_p = _attn_prompt(proj, lams, bias_tiles, gn_a, bp, seq, nh, lam_init)
        q_s = proj_s[..., :gw].reshape(bs, ts, nh, HEAD_DIM) * (HALF_DIM ** -0.5 * LOG2E)
        q2 = (jnp.transpose(q_s, (0, 2, 1, 3))[:, None] * half_mask.astype(F32)[None, :, None, None, :])
        q2 = q2.reshape(bs, 2 * nh * ts, HEAD_DIM).astype(BF16)
        k_s = proj_s[..., gw:2 * gw]
        v_s = proj_s[..., 2 * gw:3 * gw]
        new_rows = lambda a: jnp.pad(a.reshape(bs, ts * nh, HEAD_DIM), ((0, 0), (0, (page - ts) * nh), (0, 0)))
        oa_s = _attn_sample(l, q2, cache_k4, cache_v4, new_rows(k_s), new_rows(v_s), page_table, bias_s, lams, gn_a,
                            lam_init)
        oa_s = jnp.transpose(oa_s.reshape(bs, nh, ts, HEAD_DIM), (0, 2, 1, 3)).reshape(bs, ts, gw)
        oa_s = jnp.pad(oa_s, ((0, 0), (0, tp - ts), (0, 0))).reshape(bs * tp, gw).astype(BF16)
        oa = lax.dynamic_update_slice(oa_p, oa_s, (n_prompt, 0))

        wup = jnp.pad(gla_w_up[l], ((0, LANES - GLA_RANK), (0, 0)))
        wup_hi, wup_lo = _split_bf16(wup)
        gb = gla_b[l][None]
        gn_b = gla_norm_g[l][None]
        s0_s = jnp.transpose(state_gla[l].reshape(bs, nh // 2, 2 * HALF_DIM, HEAD_DIM), (0, 1, 3, 2))
        s0_p = jnp.zeros((bp, nh // 2, HEAD_DIM, LANES), F32)
        ob, sf_p = _gla(proj, gproj, wup_hi, wup_lo, gb, gn_b, s0_p, 0, bp, seq, nh, seq)
        ob, sf_s = _gla(proj, gproj, wup_hi, wup_lo, gb, gn_b, s0_s, n_prompt, bs, tp, nh, ts, prev=(ob,))
        unpair = lambda s: jnp.transpose(s, (0, 1, 3, 2)).reshape(s.shape[0], nh, HALF_DIM, HEAD_DIM)

        bs_t = jnp.transpose(cm_bs[l])
        cw, lg, lb = conv_w[l], cm_ln_g[l][None], cm_ln_b[l][None]
        oc, od, cs_p, _ = _conv_mlp(proj, cw, jnp.zeros((bp, CONV_WIDTH - 1, gw), F32), lg, lb, cm_ws[l], bs_t,
                                    0, bp, seq, nh, seq, False)
        oc, od, cs_s, vd_s = _conv_mlp(proj, cw, state_conv[l], lg, lb, cm_ws[l], bs_t, n_prompt, bs, tp, nh, ts, True,
                                       prev=(oc, od))

        mix = _matmul([oa, ob, oc, od], w_out[l].astype(BF16), d, tm_proj, _pick(d, 1024, LANES), "out_proj")
        wr = jnp.pad(jnp.concatenate([router_group[l], router_expert[l]], axis=1),
                     ((0, 0), (0, LANES - N_GROUPS - N_EXPERTS)))
        wr_hi, wr_lo = _split_bf16(wr)
        x1, ridx, rgate = _ln_router(x_parts, mix, ln1_g[l][None], ln1_b[l][None], wr_hi, wr_lo, alpha)

        dest, item_e, item_n, n_used = _route_plan(ridx, n_items)
        xs = _dispatch(x1, dest, n_items * MOE_ITEM_ROWS)
        ys = _experts(xs, item_e, item_n, n_used, w_gate, w_up, w_down, l, n_items)
        x, xb = _combine(x1, rgate, dest, ys, ln2_g[l][None], ln2_b[l][None], alpha)
        x_parts = (x, None)

        heads = lambda a, n, t: a.reshape(n, t, nh, HEAD_DIM)
        outs["kp"].append(heads(proj[:n_prompt, gw:2 * gw], bp, seq))
        outs["vp"].append(heads(proj[:n_prompt, 2 * gw:3 * gw], bp, seq))
        outs["ks"].append(heads(k_s, bs, ts))
        outs["vs"].append(heads(v_s, bs, ts))
        outs["gp"].append(unpair(sf_p))
        outs["gs"].append(unpair(sf_s))
        outs["cp"].append(cs_p)
        outs["cs"].append(cs_s)
        outs["ds"].append(vd_s.reshape(bs, tp, gw)[:, :ts])

    y_prompt = x[:n_prompt].reshape(bp, seq, d)
    y_sample = x[n_prompt:].reshape(bs, tp, d)[:, :ts]
    st = lambda k: jnp.stack(outs[k])
    return (y_prompt, y_sample, st("kp"), st("vp"), st("ks"), st("vs"), st("gp"), st("gs"),
            st("cp"), st("cs"), st("ds"))
```

```python
import functools
import math

import jax
import jax.numpy as jnp
from jax import lax
from jax.experimental import pallas as pl
from jax.experimental.pallas import tpu as pltpu

F32 = jnp.float32
BF16 = jnp.bfloat16

HEAD_DIM = 128
HALF_DIM = HEAD_DIM // 2
GLA_RANK = 16
GLA_TAU = 16.0
GLA_CHUNK = 64
CONV_WIDTH = 3
CHUNK_D = 128
REL_BUCKETS = 32
REL_MAX_DIST = 128
N_GROUPS = 4
EXPERTS_PER_GROUP = 8
N_EXPERTS = N_GROUPS * EXPERTS_PER_GROUP
TOP_K = 2
NORM_EPS = 1e-5
MASK_VALUE = -1e30
LOG2E = 1.4426950408889634

LANES = 128
SAMPLE_ROWS = 64
ATTN_TILE = 256
ATTN_HEADS = 4
SAMPLE_PAGES_PER_STEP = 8
MOE_ITEM_ROWS = 768
MOE_SUB_ROWS = 128
MOE_HIDDEN_TILE = 256
MOE_DOWN_STEPS = 4
VMEM_LIMIT = 52 * 1024 * 1024


def _pick(n, target, mult):
    if n <= target:
        return n
    best = None
    for d in range(mult, target + 1, mult):
        if n % d == 0:
            best = d
    assert best is not None, (n, target, mult)
    return best


def _params(*sem):
    return pltpu.CompilerParams(dimension_semantics=sem, vmem_limit_bytes=VMEM_LIMIT)


def _dot(a, b):
    return jnp.dot(a, b, preferred_element_type=F32)


def _dot_nt(a, b):
    return lax.dot_general(a, b, (((1,), (1,)), ((), ())), preferred_element_type=F32)


def _dot_tn(a, b):
    return lax.dot_general(a, b, (((0,), (0,)), ((), ())), preferred_element_type=F32)


def _split_bf16(x):
    hi = x.astype(BF16)
    lo = (x - hi.astype(F32)).astype(BF16)
    return hi, lo


def _sigmoid(x):
    return 1.0 / (1.0 + jnp.exp(-x))


def _mm_kernel(*refs, n_in, n_side):
    w_side = refs[2 * n_in:2 * n_in + n_side]
    o_ref = refs[2 * n_in + n_side]
    o_side = refs[2 * n_in + n_side + 1:]
    acc = None
    for x_ref, w_ref in zip(refs[:n_in], refs[n_in:2 * n_in]):
        d = _dot(x_ref[...], w_ref[...])
        acc = d if acc is None else acc + d
    o_ref[...] = acc.astype(o_ref.dtype)

    if n_side:
        @pl.when(pl.program_id(1) == 0)
        def _():
            o_side[0][...] = _dot(refs[0][...], w_side[0][...])


def _matmul(xs, w, n_out, tm, tn, name, w_side=None):
    n_in = len(xs)
    n_side = 0 if w_side is None else 1
    m, kg = xs[0].shape
    grid = (m // tm, n_out // tn)
    in_specs = [pl.BlockSpec((tm, kg), lambda i, j: (i, 0)) for _ in xs]
    in_specs += [pl.BlockSpec((kg, tn), functools.partial(lambda i, j, g: (g, j), g=g)) for g in range(n_in)]
    out_specs = [pl.BlockSpec((tm, tn), lambda i, j: (i, j))]
    out_shape = [jax.ShapeDtypeStruct((m, n_out), F32)]
    side_args = ()
    if n_side:
        in_specs.append(pl.BlockSpec((kg, LANES), lambda i, j: (0, 0)))
        out_specs.append(pl.BlockSpec((tm, LANES), lambda i, j: (i, 0)))
        out_shape.append(jax.ShapeDtypeStruct((m, LANES), F32))
        side_args = (w_side,)
    out = pl.pallas_call(
        functools.partial(_mm_kernel, n_in=n_in, n_side=n_side),
        grid=grid,
        in_specs=in_specs,
        out_specs=out_specs,
        out_shape=out_shape,
        compiler_params=_params("parallel", "arbitrary"),
        name=name,
    )(*xs, *([w] * n_in), *side_args)
    return out if n_side else out[0]


def _rel_bias(dist, rel_table):
    n = jnp.maximum(dist, 0)
    max_exact = REL_BUCKETS // 2
    nf = jnp.maximum(n, 1).astype(F32)
    large = max_exact + (jnp.log(nf / max_exact) / math.log(REL_MAX_DIST / max_exact)
                         * (REL_BUCKETS - max_exact)).astype(jnp.int32)
    bucket = jnp.where(n < max_exact, n, jnp.minimum(large, REL_BUCKETS - 1))
    onehot = (bucket[..., None] == jnp.arange(REL_BUCKETS, dtype=bucket.dtype)).astype(F32)
    return jnp.einsum("...k,kh->...h", onehot, rel_table.astype(F32), precision=lax.Precision.HIGHEST)


def _prompt_bias_tiles(rel_table, t):
    r = jnp.arange(t)[:, None]
    c = jnp.arange(t)[None, :]
    tiles = []
    for delta in range(3):
        dist = delta * t + r - c
        b = _rel_bias(dist, rel_table) * LOG2E
        tiles.append(jnp.where((dist >= 0)[..., None], b, MASK_VALUE))
    return jnp.transpose(jnp.stack(tiles), (3, 0, 1, 2))


def _lam_value(lq1, lk1, lq2, lk2, lam_init):
    return (jnp.exp(jnp.sum(lq1[...] * lk1[...], axis=1, keepdims=True))
            - jnp.exp(jnp.sum(lq2[...] * lk2[...], axis=1, keepdims=True)) + lam_init)


def _rms_head(o, gain, post):
    return o * lax.rsqrt(jnp.mean(o * o, axis=1, keepdims=True) + NORM_EPS) * gain * post


def _attn_prompt_kernel(lq1, lk1, lq2, lk2, q_ref, k_ref, v_ref, bias_ref, gn_ref, o_ref, *, t, lam_init):
    qi = pl.program_id(2)
    n_grp = bias_ref.shape[0]
    lam = _lam_value(lq1, lk1, lq2, lk2, lam_init)
    lane = lax.broadcasted_iota(jnp.int32, (t, HEAD_DIM), 1)
    heads = []
    for h in range(n_grp):
        q = q_ref[:, h * HEAD_DIM:(h + 1) * HEAD_DIM] * (HALF_DIM ** -0.5 * LOG2E)
        heads.append(jnp.concatenate([jnp.where(lane < HALF_DIM, q, 0.0), jnp.where(lane >= HALF_DIM, q, 0.0)],
                                     axis=0).astype(BF16))

    def body(j, carry):
        start = pl.multiple_of(j * t, t)
        tile = jnp.minimum(qi - j, 2)
        out = []
        for h in range(n_grp):
            m, s, acc = carry[h]
            cols = slice(h * HEAD_DIM, (h + 1) * HEAD_DIM)
            kt = k_ref[pl.ds(start, t), cols].astype(BF16)
            vt = v_ref[pl.ds(start, t), cols].astype(BF16)
            bt = bias_ref[h, tile]
            l = _dot_nt(heads[h], kt) + jnp.concatenate([bt, bt], axis=0)
            m_new = jnp.maximum(m, jnp.max(l, axis=1, keepdims=True))
            a = jnp.exp2(m - m_new)
            p = jnp.exp2(l - m_new)
            out.append((m_new, a * s + jnp.sum(p, axis=1, keepdims=True), a * acc + _dot(p.astype(BF16), vt)))
        return tuple(out)

    init = (jnp.full((2 * t, 1), MASK_VALUE, F32), jnp.zeros((2 * t, 1), F32), jnp.zeros((2 * t, HEAD_DIM), F32))
    res = lax.fori_loop(0, qi + 1, body, (init,) * n_grp)
    for h in range(n_grp):
        _, s, acc = res[h]
        o = acc / s
        o = o[:t] - lam * o[t:]
        o_ref[:, h * HEAD_DIM:(h + 1) * HEAD_DIM] = _rms_head(o, gn_ref[...], 1.0 - lam_init).astype(o_ref.dtype)


def _attn_prompt(proj, lams, bias_tiles, gn, n_seq, seq, n_heads, lam_init):
    t = ATTN_TILE
    n_grp = math.gcd(ATTN_HEADS, n_heads)
    assert seq % t == 0 and t >= REL_MAX_DIST
    nq = seq // t
    pair = n_grp * HEAD_DIM
    groups = n_heads // n_grp
    lam_spec = pl.BlockSpec((1, HALF_DIM), lambda b, h, i: (0, 0))
    return pl.pallas_call(
        functools.partial(_attn_prompt_kernel, t=t, lam_init=lam_init),
        grid=(n_seq, groups, nq),
        in_specs=[lam_spec] * 4 + [
            pl.BlockSpec((t, pair), lambda b, h, i: (b * nq + i, h)),
            pl.BlockSpec((seq, pair), lambda b, h, i: (b, groups + h)),
            pl.BlockSpec((seq, pair), lambda b, h, i: (b, 2 * groups + h)),
            pl.BlockSpec((n_grp, 3, t, t), lambda b, h, i: (h, 0, 0, 0)),
            pl.BlockSpec((1, HEAD_DIM), lambda b, h, i: (0, 0)),
        ],
        out_specs=pl.BlockSpec((t, pair), lambda b, h, i: (b * nq + i, h)),
        out_shape=jax.ShapeDtypeStruct((proj.shape[0], n_heads * HEAD_DIM), BF16),
        compiler_params=_params("parallel", "parallel", "arbitrary"),
        name="attn_prompt",
    )(*lams, proj, proj, proj, bias_tiles, gn)


def _attn_sample_kernel(pt_ref, lq1, lk1, lq2, lk2, q_ref, *refs, n_pp, lam_init):
    kc_refs, vc_refs = refs[:n_pp], refs[n_pp:2 * n_pp]
    kn_ref, vn_ref, bias_ref, gn_ref, o_ref, m_ref, s_ref, acc_ref = refs[2 * n_pp:]
    p = pl.program_id(1)
    n_steps = pl.num_programs(1)

    @pl.when(p == 0)
    def _():
        m_ref[...] = jnp.full(m_ref.shape, MASK_VALUE, F32)
        s_ref[...] = jnp.zeros(s_ref.shape, F32)
        acc_ref[...] = jnp.zeros(acc_ref.shape, F32)

    q = q_ref[0]

    def step(ks, vs, biases):
        ls = [_dot_nt(q, k.astype(BF16)) + b for k, b in zip(ks, biases)]
        m = m_ref[...]
        m_new = m
        for l in ls:
            m_new = jnp.maximum(m_new, jnp.max(l, axis=1, keepdims=True))
        a = jnp.exp2(m - m_new)
        s = a * s_ref[...]
        acc = a * acc_ref[...]
        for l, v in zip(ls, vs):
            pr = jnp.exp2(l - m_new)
            s = s + jnp.sum(pr, axis=1, keepdims=True)
            acc = acc + _dot(pr.astype(BF16), v.astype(BF16))
        s_ref[...] = s
        acc_ref[...] = acc
        m_ref[...] = m_new

    last = p == n_steps - 1
    tables = [0] * (n_pp - 1) + [jnp.where(last, 1, 0)]
    step([r[0, 0] for r in kc_refs], [r[0, 0] for r in vc_refs], [bias_ref[t] for t in tables])

    @pl.when(last)
    def _():
        step([kn_ref[0]], [vn_ref[0]], [bias_ref[2]])
        lam = _lam_value(lq1, lk1, lq2, lk2, lam_init)
        o = acc_ref[...] / s_ref[...]
        half = o.shape[0] // 2
        o_ref[0] = _rms_head(o[:half] - lam * o[half:], gn_ref[...], 1.0 - lam_init)


def _attn_sample(layer, q2, cache_k, cache_v, k_new, v_new, page_table, bias, lams, gn, lam_init):
    n_seq, n_pages = page_table.shape
    prow = cache_k.shape[2]
    rows = q2.shape[1]
    n_pp = SAMPLE_PAGES_PER_STEP if n_pages % SAMPLE_PAGES_PER_STEP == 0 else 1
    lam_spec = pl.BlockSpec((1, HALF_DIM), lambda b, p, pt: (0, 0))
    cache_specs = [pl.BlockSpec((1, 1, prow, HEAD_DIM),
                                functools.partial(lambda b, p, pt, i: (layer, pt[b, p * n_pp + i], 0, 0), i=i))
                   for i in range(n_pp)]
    new_spec = pl.BlockSpec((1, prow, HEAD_DIM), lambda b, p, pt: (b, 0, 0))
    grid_spec = pltpu.PrefetchScalarGridSpec(
        num_scalar_prefetch=1,
        grid=(n_seq, n_pages // n_pp),
        in_specs=[lam_spec] * 4 + [pl.BlockSpec((1, rows, HEAD_DIM), lambda b, p, pt: (b, 0, 0))]
        + cache_specs + cache_specs + [
            new_spec, new_spec,
            pl.BlockSpec((3, rows, prow), lambda b, p, pt: (0, 0, 0)),
            pl.BlockSpec((1, HEAD_DIM), lambda b, p, pt: (0, 0)),
        ],
        out_specs=pl.BlockSpec((1, rows // 2, HEAD_DIM), lambda b, p, pt: (b, 0, 0)),
        scratch_shapes=[pltpu.VMEM((rows, 1), F32), pltpu.VMEM((rows, 1), F32), pltpu.VMEM((rows, HEAD_DIM), F32)],
    )
    return pl.pallas_call(
        functools.partial(_attn_sample_kernel, n_pp=n_pp, lam_init=lam_init),
        grid_spec=grid_spec,
        out_shape=jax.ShapeDtypeStruct((n_seq, rows // 2, HEAD_DIM), F32),
        compiler_params=_params("parallel", "arbitrary"),
        name="attn_sample",
    )(page_table, *lams, q2, *([cache_k] * n_pp), *([cache_v] * n_pp), k_new, v_new, bias, gn)


def _gla_kernel(q_ref, k_ref, v_ref, r_ref, gp_ref, wh_ref, wl_ref, gb_ref, gn_ref, s0_ref, *rest,
                chunk, n_chunks, n_heads, valid):
    o_ref, sfin_ref, st_ref = rest[-3:]
    t = pl.program_id(1)

    @pl.when(t == 0)
    def _():
        st_ref[...] = s0_ref[0]

    half = chunk // 2
    quarter = chunk // 4
    row = lax.broadcasted_iota(jnp.int32, (chunk, chunk), 0)
    col = lax.broadcasted_iota(jnp.int32, (chunk, chunk), 1)
    tril = row >= col
    same_half = (row < half) == (col < half)
    tri_bf = jnp.where(tril, 1.0, 0.0).astype(BF16)
    lane = lax.broadcasted_iota(jnp.int32, (chunk, LANES), 1)
    lane_sq = lax.broadcasted_iota(jnp.int32, (HEAD_DIM, LANES), 1)

    for c in range(n_chunks):
        rs = slice(c * chunk, (c + 1) * chunk)
        gh, gl = _split_bf16(gp_ref[rs, :])
        z = _dot(gh, wh_ref[...]) + _dot(gl, wh_ref[...]) + _dot(gh, wl_ref[...]) + gb_ref[...]
        g = -(jnp.maximum(-z, 0.0) + jnp.log(1.0 + jnp.exp(-jnp.abs(z)))) * (1.0 / GLA_TAU)
        k = k_ref[rs, :]
        if valid < chunk:
            ok = lax.broadcasted_iota(jnp.int32, g.shape, 0) < valid
            g = jnp.where(ok, g, 0.0)
            k = jnp.where(ok, k, 0.0)
        gh, gl = _split_bf16(g)
        b = _dot(tri_bf, gh) + _dot(tri_bf, gl)
        b_last = b[chunk - 1:chunk, :]
        b_half = b[half - 1:half, :]
        in_lo = lax.broadcasted_iota(jnp.int32, b.shape, 0) < half
        b_near = jnp.where(in_lo, b[quarter - 1:quarter, :], b[half + quarter - 1:half + quarter, :])
        q = q_ref[rs, :] * (HALF_DIM ** -0.5)
        q_state = q * jnp.exp(b)
        q_near = q * jnp.exp(b - b_near)
        k_near = (k * jnp.exp(b_near - b)).astype(BF16)
        q_far = q * jnp.exp(jnp.minimum(b - b_half, 0.0))
        k_far = (k * jnp.exp(jnp.minimum(b_half - b, 0.0))).astype(BF16)
        k_end = (k * jnp.exp(b_last - b)).astype(BF16)
        decay = jnp.exp(b_last)

        for hp in range(n_heads // 2):
            ls = slice(hp * LANES, (hp + 1) * LANES)
            st = st_ref[hp]
            st_bf = st.astype(BF16)
            upd = []
            for par in range(2):
                h = 2 * hp + par
                mine = (lane < HALF_DIM) if par == 0 else (lane >= HALF_DIM)
                hs = slice(h * HEAD_DIM, (h + 1) * HEAD_DIM)
                v_h = v_ref[rs, hs].astype(BF16)
                att = jnp.where(same_half,
                                _dot_nt(jnp.where(mine, q_near[:, ls], 0.0).astype(BF16), k_near[:, ls]),
                                _dot_nt(jnp.where(mine, q_far[:, ls], 0.0).astype(BF16), k_far[:, ls]))
                att = jnp.where(tril, att, 0.0).astype(BF16)
                o = _dot(att, v_h) + _dot_nt(jnp.where(mine, q_state[:, ls], 0.0).astype(BF16), st_bf)
                r_h = r_ref[rs, hs]
                o_ref[rs, hs] = _rms_head(o, gn_ref[...], r_h * _sigmoid(r_h)).astype(o_ref.dtype)
                upd.append(_dot_tn(v_h, k_end[:, ls]))
            st_ref[hp] = st * decay[:, ls] + jnp.where(lane_sq < HALF_DIM, upd[0], upd[1])

    @pl.when(t == pl.num_programs(1) - 1)
    def _():
        sfin_ref[0] = st_ref[...]


def _into(prev):
    specs = [pl.BlockSpec(memory_space=pl.ANY) for _ in prev]
    return specs, tuple(prev)


def _gla(proj, gproj, wup_hi, wup_lo, gla_b, gn, s0, row0, n_seq, seq, n_heads, valid, prev=()):
    gw = n_heads * HEAD_DIM
    hk = n_heads * HALF_DIM
    chunk = min(GLA_CHUNK, seq)
    tt = _pick(seq, 256, chunk)
    nt = seq // tt
    r0 = row0 // tt
    assert row0 % tt == 0 and chunk % 4 == 0
    qcol = 3 * gw // hk
    row_spec = lambda width, colblk: pl.BlockSpec((tt, width), lambda b, t: (r0 + b * nt + t, colblk))
    const = lambda shape: pl.BlockSpec(shape, lambda b, t: tuple(0 for _ in shape))
    st_spec = pl.BlockSpec((1, n_heads // 2, HEAD_DIM, LANES), lambda b, t: (b, 0, 0, 0))
    prev_specs, prev_args = _into(prev)
    n_in = 10
    return pl.pallas_call(
        functools.partial(_gla_kernel, chunk=chunk, n_chunks=tt // chunk, n_heads=n_heads, valid=valid),
        grid=(n_seq, nt),
        in_specs=[row_spec(hk, qcol), row_spec(hk, qcol + 1), row_spec(gw, 4), row_spec(gw, 5),
                  pl.BlockSpec((tt, LANES), lambda b, t: (r0 + b * nt + t, 0)),
                  const((LANES, hk)), const((LANES, hk)), const((1, hk)), const((1, HEAD_DIM)), st_spec] + prev_specs,
        out_specs=[pl.BlockSpec((tt, gw), lambda b, t: (r0 + b * nt + t, 0)), st_spec],
        out_shape=[jax.ShapeDtypeStruct((proj.shape[0], gw), BF16),
                   jax.ShapeDtypeStruct((n_seq, n_heads // 2, HEAD_DIM, LANES), F32)],
        input_output_aliases={n_in + k: k for k in range(len(prev))},
        scratch_shapes=[pltpu.VMEM((n_heads // 2, HEAD_DIM, LANES), F32)],
        compiler_params=_params("parallel", "arbitrary"),
        name="gla",
    )(proj, proj, proj, proj, gproj, wup_hi, wup_lo, gla_b, gn, s0, *prev_args)


def _conv_mlp_kernel(cb_ref, cc_ref, ch_ref, du_ref, dv_ref, cw_ref, buf_ref, lg_ref, lb_ref, ws_ref, bs_ref, *rest,
                     tt, chunk, n_heads, valid):
    oc_ref, od_ref, cs_ref, vd_ref, carry_ref = rest[-5:]
    t = pl.program_id(1)

    @pl.when(t == 0)
    def _():
        carry_ref[...] = buf_ref[0]

    z = cc_ref[...] * ch_ref[...]
    row = lax.broadcasted_iota(jnp.int32, z.shape, 0)
    c0 = carry_ref[0:1, :]
    c1 = carry_ref[1:2, :]
    z1 = jnp.where(row == 0, c1, pltpu.roll(z, 1, 0))
    z2 = jnp.where(row == 0, c0, jnp.where(row == 1, c1, pltpu.roll(z, 2, 0)))
    w = cw_ref[...]
    oc_ref[...] = (cb_ref[...] * (z2 * w[0:1, :] + z1 * w[1:2, :] + z * w[2:3, :])).astype(oc_ref.dtype)
    if valid >= 2:
        carry_ref[...] = z[valid - 2:valid, :]
    else:
        carry_ref[...] = jnp.concatenate([c1, z[0:1, :]], axis=0)

    @pl.when(t == pl.num_programs(1) - 1)
    def _():
        cs_ref[0] = carry_ref[...]

    x = dv_ref[...]
    mu = jnp.mean(x, axis=1, keepdims=True)
    xc = x - mu
    var = jnp.mean(xc * xc, axis=1, keepdims=True)
    vd = xc * lax.rsqrt(var + NORM_EPS) * lg_ref[...] + lb_ref[...]
    vd_ref[...] = vd
    vd_bf = vd.astype(BF16)
    r2 = lax.broadcasted_iota(jnp.int32, (chunk, chunk), 0)
    c2 = lax.broadcasted_iota(jnp.int32, (chunk, chunk), 1)
    for g in range(n_heads):
        wc = jnp.where(r2 >= c2, ws_ref[g, :chunk, :chunk], 0.0).astype(BF16)
        bias = bs_ref[:chunk, g:g + 1]
        cols = slice(g * HEAD_DIM, (g + 1) * HEAD_DIM)
        for c in range(tt // chunk):
            rs = slice(c * chunk, (c + 1) * chunk)
            zc = _dot(wc, vd_bf[rs, cols]) + bias
            od_ref[rs, cols] = (du_ref[rs, cols] * zc).astype(od_ref.dtype)


def _conv_mlp(proj, conv_w, buf, ln_g, ln_b, ws, bs_t, row0, n_seq, seq, n_heads, valid, keep_vd, prev=()):
    gw = n_heads * HEAD_DIM
    chunk = min(CHUNK_D, seq)
    tt = _pick(seq, 256, chunk)
    nt = seq // tt
    r0 = row0 // tt
    assert row0 % tt == 0 and (valid == seq or nt == 1)
    row_spec = lambda colblk: pl.BlockSpec((tt, gw), lambda b, t: (r0 + b * nt + t, colblk))
    const = lambda shape: pl.BlockSpec(shape, lambda b, t: tuple(0 for _ in shape))
    out_rows = pl.BlockSpec((tt, gw), lambda b, t: (r0 + b * nt + t, 0))
    vd_spec = pl.BlockSpec((tt, gw), (lambda b, t: (b * nt + t, 0)) if keep_vd else (lambda b, t: (0, 0)))
    state_spec = pl.BlockSpec((1, CONV_WIDTH - 1, gw), lambda b, t: (b, 0, 0))
    prev_specs, prev_args = _into(prev)
    n_in = 11
    return pl.pallas_call(
        functools.partial(_conv_mlp_kernel, tt=tt, chunk=chunk, n_heads=n_heads, valid=min(valid, tt)),
        grid=(n_seq, nt),
        in_specs=[row_spec(6), row_spec(7), row_spec(8), row_spec(9), row_spec(10),
                  const((CONV_WIDTH, gw)), state_spec, const((1, gw)), const((1, gw)),
                  const((n_heads, CHUNK_D, CHUNK_D)), const((CHUNK_D, n_heads))] + prev_specs,
        out_specs=[out_rows, out_rows, state_spec, vd_spec],
        out_shape=[jax.ShapeDtypeStruct((proj.shape[0], gw), BF16), jax.ShapeDtypeStruct((proj.shape[0], gw), BF16),
                   jax.ShapeDtypeStruct((n_seq, CONV_WIDTH - 1, gw), F32),
                   jax.ShapeDtypeStruct((n_seq * seq if keep_vd else tt, gw), F32)],
        input_output_aliases={n_in + k: k for k in range(len(prev))},
        scratch_shapes=[pltpu.VMEM((CONV_WIDTH - 1, gw), F32)],
        compiler_params=_params("parallel", "arbitrary"),
        name="conv_mlp",
    )(proj, proj, proj, proj, proj, conv_w, buf, ln_g, ln_b, ws, bs_t, *prev_args)


def _layer_norm_rows(x, g, b):
    mu = jnp.mean(x, axis=1, keepdims=True)
    xc = x - mu
    var = jnp.mean(xc * xc, axis=1, keepdims=True)
    return xc * lax.rsqrt(var + NORM_EPS) * g + b


def _ln_router_kernel(xa_ref, xb_ref, mix_ref, g_ref, b_ref, wh_ref, wl_ref, y_ref, idx_ref, gate_ref, *, alpha, split):
    x = jnp.where(pl.program_id(0) < split, xa_ref[...], xb_ref[...])
    y = _layer_norm_rows(alpha * x + mix_ref[...], g_ref[...], b_ref[...])
    y_ref[...] = y
    yh, yl = _split_bf16(y)
    lg = _dot(yh, wh_ref[...]) + _dot(yl, wh_ref[...]) + _dot(yh, wl_ref[...])
    lane = lax.broadcasted_iota(jnp.int32, lg.shape, 1)
    neg = -jnp.inf
    glog = jnp.where(lane < N_GROUPS, lg, neg)
    gmax = jnp.max(glog, axis=1, keepdims=True)
    g_star = jnp.min(jnp.where(glog == gmax, lane, LANES), axis=1, keepdims=True)
    p_top = 1.0 / jnp.sum(jnp.exp(glog - gmax), axis=1, keepdims=True)
    lo = N_GROUPS + EXPERTS_PER_GROUP * g_star
    w1 = jnp.where((lane >= lo) & (lane < lo + EXPERTS_PER_GROUP), lg, neg)
    v1 = jnp.max(w1, axis=1, keepdims=True)
    i1 = jnp.min(jnp.where(w1 == v1, lane, LANES), axis=1, keepdims=True)
    w2 = jnp.where(lane == i1, neg, w1)
    v2 = jnp.max(w2, axis=1, keepdims=True)
    i2 = jnp.min(jnp.where(w2 == v2, lane, LANES), axis=1, keepdims=True)
    e21 = jnp.exp(v2 - v1)
    gate1 = p_top / (1.0 + e21)
    gate2 = p_top * e21 / (1.0 + e21)
    idx_ref[...] = jnp.where(lane == 0, i1 - N_GROUPS, jnp.where(lane == 1, i2 - N_GROUPS, 0))
    gate_ref[...] = jnp.where(lane == 0, gate1, jnp.where(lane == 1, gate2, 0.0))


def _ln_router(x_parts, mix, g, b, wr_hi, wr_lo, alpha):
    m, d = mix.shape
    tm = _pick(math.gcd(m, x_parts[0].shape[0]), 256, 8)
    xa, xb = x_parts if x_parts[1] is not None else (x_parts[0], x_parts[0])
    split = xa.shape[0] // tm if x_parts[1] is not None else m // tm
    rows = pl.BlockSpec((tm, d), lambda i: (i, 0))
    const = lambda shape: pl.BlockSpec(shape, lambda i: (0, 0))
    small = pl.BlockSpec((tm, LANES), lambda i: (i, 0))
    return pl.pallas_call(
        functools.partial(_ln_router_kernel, alpha=alpha, split=split),
        grid=(m // tm,),
        in_specs=[pl.BlockSpec((tm, d), lambda i: (jnp.minimum(i, split - 1), 0)),
                  pl.BlockSpec((tm, d), lambda i: (jnp.maximum(i - split, 0), 0)),
                  rows, const((1, d)), const((1, d)), const((d, LANES)), const((d, LANES))],
        out_specs=[rows, small, small],
        out_shape=[jax.ShapeDtypeStruct((m, d), F32),
                   jax.ShapeDtypeStruct((m, LANES), jnp.int32), jax.ShapeDtypeStruct((m, LANES), F32)],
        compiler_params=_params("parallel"),
        name="ln_router",
    )(xa, xb, mix, g, b, wr_hi, wr_lo)


SLAB_TILE = 8


def _token_copy(src, src_row, dst, dst_row, sem):
    rows = lambda r: pl.ds(pl.multiple_of(r * SLAB_TILE, SLAB_TILE), SLAB_TILE)
    return pltpu.make_async_copy(src.at[:, rows(src_row), :], dst.at[:, rows(dst_row), :], sem)


def _dispatch_kernel(dest_ref, x_ref, xs_hbm, stage, sem, *, tm):
    for p in range(stage.shape[0]):
        for c in range(SLAB_TILE):
            col = (p * SLAB_TILE + c) * LANES
            stage[p, pl.ds(c, tm, stride=SLAB_TILE), :] = x_ref[:, col:col + LANES]

    def copy(i, k):
        return _token_copy(stage, i, xs_hbm, dest_ref[0, 0, 2 * i + k], sem)

    def start(i, _):
        copy(i, 0).start()
        copy(i, 1).start()
        return 0

    def wait(i, _):
        copy(i, 0).wait()
        copy(i, 1).wait()
        return 0

    lax.fori_loop(0, tm, start, 0)
    lax.fori_loop(0, tm, wait, 0)


def _dispatch(x1, dest, n_rows):
    m, d = x1.shape
    planes = d // (SLAB_TILE * LANES)
    tm = _pick(m, 256, 8)
    dest3 = dest.reshape(m // tm, 1, TOP_K * tm)
    return pl.pallas_call(
        functools.partial(_dispatch_kernel, tm=tm),
        grid=(m // tm,),
        in_specs=[pl.BlockSpec((1, 1, TOP_K * tm), lambda i: (i, 0, 0), memory_space=pltpu.SMEM),
                  pl.BlockSpec((tm, d), lambda i: (i, 0))],
        out_specs=pl.BlockSpec(memory_space=pl.ANY),
        out_shape=jax.ShapeDtypeStruct((planes, n_rows * SLAB_TILE, LANES), F32),
        scratch_shapes=[pltpu.VMEM((planes, tm * SLAB_TILE, LANES), F32), pltpu.SemaphoreType.DMA(())],
        compiler_params=pltpu.CompilerParams(dimension_semantics=("arbitrary",), has_side_effects=True,
                                             vmem_limit_bytes=VMEM_LIMIT),
        name="moe_dispatch",
    )(dest3, x1)


def _expert_kernel(ie_ref, in_ref, nu_ref, x_ref, wg_ref, wu_ref, wd_ref, y_ref, xb, h_buf, *, planes, hs, sub, tiles):
    i = pl.program_id(0)
    s = pl.program_id(1)
    n_sub = (in_ref[i] + sub - 1) // sub
    live = i < nu_ref[0]
    per_blk = SLAB_TILE // tiles
    up0 = planes
    down0 = planes + hs
    kp = SLAB_TILE * LANES
    ht = h_buf.shape[2]

    def for_rows(fn):
        for k in range(1, xb.shape[1] // sub + 1):
            pl.when(n_sub == k)(functools.partial(fn, k * sub))

    @pl.when(live & (s < up0))
    def _():
        def convert(m):
            xb[s, :m, :] = jnp.concatenate([x_ref[pl.ds(c, m, stride=SLAB_TILE), :]
                                            for c in range(SLAB_TILE)], axis=1).astype(BF16)
        for_rows(convert)

    @pl.when(live & (s >= up0) & (s < down0))
    def _():
        wg = wg_ref[0].astype(BF16)
        wu = wu_ref[0].astype(BF16)

        def up(m):
            g = sum(_dot(xb[p, :m, :], wg[p * kp:(p + 1) * kp, :]) for p in range(planes))
            u = sum(_dot(xb[p, :m, :], wu[p * kp:(p + 1) * kp, :]) for p in range(planes))
            h_buf[s - up0, :m, :] = (g * _sigmoid(g) * u).astype(BF16)
        for_rows(up)

    @pl.when(live & (s >= down0))
    def _():
        wd = wd_ref[0].astype(BF16)
        c0 = ((s - down0) % per_blk) * tiles

        def down(m):
            y = sum(_dot(h_buf[j, :m, :], wd[j * ht:(j + 1) * ht, :]) for j in range(hs))
            for c in range(tiles):
                y_ref[pl.ds(c0 + c, m, stride=SLAB_TILE), :] = y[:, c * LANES:(c + 1) * LANES]
        for_rows(down)


def _experts(xs, item_e, item_n, n_used, w_gate, w_up, w_down, layer, n_items):
    d, hidden = w_gate.shape[-2:]
    n_exp = w_gate.shape[1]
    planes = xs.shape[0]
    ht = min(MOE_HIDDEN_TILE, hidden)
    hs = hidden // ht
    ns = MOE_DOWN_STEPS
    nt = d // ns
    tiles = nt // LANES
    assert SLAB_TILE % tiles == 0 and hidden % ht == 0
    per_blk = SLAB_TILE // tiles
    n_steps = planes + hs + ns
    wg3 = w_gate.reshape(-1, d, hidden)
    wu3 = w_up.reshape(-1, d, hidden)
    wd3 = w_down.reshape(-1, hidden, d)
    e0 = layer * n_exp
    rows = MOE_ITEM_ROWS

    def item(i, nu):
        return jnp.minimum(i, nu[0] - 1)

    def step(i, s, nu):
        return jnp.where(i < nu[0], s, n_steps - 1)

    def expert(i, ie, nu):
        return e0 + ie[item(i, nu)]

    def x_plane(i, s, ie, inn, nu):
        return jnp.minimum(step(i, s, nu), planes - 1), item(i, nu), 0

    def up_slice(i, s, ie, inn, nu):
        return expert(i, ie, nu), 0, jnp.clip(step(i, s, nu) - planes, 0, hs - 1)

    def down_slice(i, s, ie, inn, nu):
        return expert(i, ie, nu), 0, jnp.maximum(step(i, s, nu) - planes - hs, 0)

    def y_plane(i, s, ie, inn, nu):
        return jnp.maximum(step(i, s, nu) - planes - hs, 0) // per_blk, item(i, nu), 0

    grid_spec = pltpu.PrefetchScalarGridSpec(
        num_scalar_prefetch=3,
        grid=(n_items, n_steps),
        in_specs=[
            pl.BlockSpec((None, rows * SLAB_TILE, LANES), x_plane),
            pl.BlockSpec((1, d, ht), up_slice),
            pl.BlockSpec((1, d, ht), up_slice),
            pl.BlockSpec((1, hidden, nt), down_slice),
        ],
        out_specs=pl.BlockSpec((None, rows * SLAB_TILE, LANES), y_plane),
        scratch_shapes=[pltpu.VMEM((planes, rows, SLAB_TILE * LANES), BF16), pltpu.VMEM((hs, rows, ht), BF16)],
    )
    return pl.pallas_call(
        functools.partial(_expert_kernel, planes=planes, hs=hs, sub=MOE_SUB_ROWS, tiles=tiles),
        grid_spec=grid_spec,
        out_shape=jax.ShapeDtypeStruct(xs.shape, F32),
        compiler_params=_params("arbitrary", "arbitrary"),
        name="moe_experts",
    )(item_e, item_n, n_used, xs, wg3, wu3, wd3)


def _combine_kernel(dest_ref, next_ref, x_ref, gate_ref, g_ref, b_ref, ys_hbm, y_ref, yb_ref, stage0, stage1, sems,
                    *, tm, alpha):
    i = pl.program_id(0)
    slot = i % 2
    stages = (stage0, stage1)

    def copy(dests, t, k, sl):
        return _token_copy(ys_hbm, dests[0, 0, 2 * t + k], stages[k].at[sl], t, sems.at[sl])

    def start_tile(dests, sl):
        def start(t, _):
            copy(dests, t, 0, sl).start()
            copy(dests, t, 1, sl).start()
            return 0
        lax.fori_loop(0, tm, start, 0)

    @pl.when(i == 0)
    def _():
        start_tile(dest_ref, slot)

    @pl.when(i + 1 < pl.num_programs(0))
    def _():
        start_tile(next_ref, 1 - slot)

    def wait(t, _):
        copy(dest_ref, t, 0, slot).wait()
        copy(dest_ref, t, 1, slot).wait()
        return 0

    lax.fori_loop(0, tm, wait, 0)
    gates = gate_ref[...]
    g0 = gates[:, 0:1]
    g1 = gates[:, 1:2]
    tile = lambda stage, p, c: stage[slot, p, pl.ds(c, tm, stride=SLAB_TILE), :]
    ffn = jnp.concatenate([g0 * tile(stage0, p, c) + g1 * tile(stage1, p, c)
                           for p in range(stage0.shape[1]) for c in range(SLAB_TILE)], axis=1)
    y = _layer_norm_rows(alpha * x_ref[...] + ffn, g_ref[...], b_ref[...])
    y_ref[...] = y
    yb_ref[...] = y.astype(BF16)


def _combine(x1, gates, dest, ys, g, b, alpha):
    m, d = x1.shape
    tm = _pick(m, 256, 8)
    n_tiles = m // tm
    stage = pltpu.VMEM((2, ys.shape[0], tm * SLAB_TILE, LANES), F32)
    dest3 = dest.reshape(n_tiles, 1, TOP_K * tm)
    dest_spec = lambda off: pl.BlockSpec((1, 1, TOP_K * tm), lambda i: (jnp.minimum(i + off, n_tiles - 1), 0, 0),
                                         memory_space=pltpu.SMEM)
    rows = pl.BlockSpec((tm, d), lambda i: (i, 0))
    const = pl.BlockSpec((1, d), lambda i: (0, 0))
    return pl.pallas_call(
        functools.partial(_combine_kernel, tm=tm, alpha=alpha),
        grid=(n_tiles,),
        in_specs=[dest_spec(0), dest_spec(1), rows, pl.BlockSpec((tm, LANES), lambda i: (i, 0)), const, const,
                  pl.BlockSpec(memory_space=pl.ANY)],
        out_specs=[rows, rows],
        out_shape=[jax.ShapeDtypeStruct((m, d), F32), jax.ShapeDtypeStruct((m, d), BF16)],
        scratch_shapes=[stage, stage, pltpu.SemaphoreType.DMA((2,))],
        compiler_params=_params("arbitrary"),
        name="moe_combine",
    )(dest3, dest3, x1, gates, g, b, ys)


def _route_plan(idx, n_items):
    e_flat = idx[:, :TOP_K].reshape(-1)
    onehot = (e_flat[:, None] == jnp.arange(N_EXPERTS, dtype=jnp.int32)[None, :]).astype(jnp.int32)
    csum = jnp.cumsum(onehot, axis=0)
    rank = jnp.sum(onehot * csum, axis=1) - 1
    counts = csum[-1]
    items_e = (counts + MOE_ITEM_ROWS - 1) // MOE_ITEM_ROWS
    item_end = jnp.cumsum(items_e)
    item_start = item_end - items_e
    dest = (item_start[e_flat] * MOE_ITEM_ROWS + rank).astype(jnp.int32)
    ids = jnp.arange(n_items, dtype=jnp.int32)
    item_e = jnp.minimum(jnp.searchsorted(item_end, ids, side="right"), N_EXPERTS - 1).astype(jnp.int32)
    n_used = item_end[-1].astype(jnp.int32)
    item_n = jnp.clip(counts[item_e] - (ids - item_start[item_e]) * MOE_ITEM_ROWS, 0, MOE_ITEM_ROWS)
    item_n = jnp.where(ids < n_used, item_n, 0).astype(jnp.int32)
    return dest, item_e, item_n, n_used.reshape(1)


def kernel(x_prompt, x_sample, cache_k, cache_v, page_table, state_gla, state_conv, rel_table,
           w_in, w_out, lam_q1, lam_k1, lam_q2, lam_k2, diff_norm_g, gla_w_up, gla_b, gla_norm_g,
           conv_w, cm_ln_g, cm_ln_b, cm_ws, cm_bs, ln1_g, ln1_b, ln2_g, ln2_b,
           router_group, router_expert, w_gate, w_up, w_down):
    bp, seq, d = x_prompt.shape
    bs, ts, _ = x_sample.shape
    depth = w_in.shape[0]
    gw = w_out.shape[1] // 4
    nh = gw // HEAD_DIM
    hk = nh * HALF_DIM
    page = cache_k.shape[2]
    n_phys = cache_k.shape[1]
    tp = SAMPLE_ROWS
    n_prompt = bp * seq
    n_all = n_prompt + bs * tp
    alpha = (2.0 * depth) ** 0.25
    assert ts <= tp and nh % 2 == 0 and n_prompt % tp == 0

    x_parts = (x_prompt.reshape(n_prompt, d), jnp.pad(x_sample, ((0, 0), (0, tp - ts), (0, 0))).reshape(bs * tp, d))
    xb = jnp.concatenate([x_parts[0].astype(BF16), x_parts[1].astype(BF16)], axis=0)

    bias_tiles = _prompt_bias_tiles(rel_table, ATTN_TILE)
    t_idx = jnp.arange(ts)
    k_idx = jnp.arange(page)
    dist_far = jnp.full((ts, page), 2 * page, jnp.int32)
    dist_last = page + t_idx[:, None] - k_idx[None, :]
    dist_new = t_idx[:, None] - k_idx[None, :]
    ok = jnp.stack([dist_far > 0, dist_last > 0, (dist_new >= 0) & (k_idx[None, :] < ts)])
    b3 = _rel_bias(jnp.stack([dist_far, dist_last, dist_new]), rel_table) * LOG2E
    same_head = jnp.eye(nh, dtype=bool)
    b3 = jnp.where(ok[:, None, :, :, None] & same_head[None, :, None, None, :],
                   jnp.transpose(b3, (0, 3, 1, 2))[..., None], MASK_VALUE)
    bias_s = jnp.tile(b3.reshape(3, nh * ts, page * nh), (1, 2, 1))
    half_mask = (jnp.arange(HEAD_DIM)[None, :] < HALF_DIM) == (jnp.arange(2)[:, None] == 0)

    cache_k4 = cache_k.reshape(depth, n_phys, page * nh, HEAD_DIM)
    cache_v4 = cache_v.reshape(depth, n_phys, page * nh, HEAD_DIM)
    n_items = (TOP_K * n_all) // MOE_ITEM_ROWS + N_EXPERTS
    tm_proj = _pick(n_all, 1024, 16)

    outs = {k: [] for k in ("kp", "vp", "ks", "vs", "gp", "gs", "cp", "cs", "ds")}
    for l in range(depth):
        lam_init = 0.8 - 0.6 * math.exp(-0.3 * l)
        lams = (lam_q1[l][None], lam_k1[l][None], lam_q2[l][None], lam_k2[l][None])
        gn_a = diff_norm_g[l][None]

        w_main = jnp.concatenate([w_in[l, :, :5 * gw], w_in[l, :, 5 * gw + GLA_RANK:]], axis=1).astype(BF16)
        w_gate_cols = jnp.pad(w_in[l, :, 5 * gw:5 * gw + GLA_RANK], ((0, 0), (0, LANES - GLA_RANK))).astype(BF16)
        proj, gproj = _matmul([xb], w_main, 11 * gw, _pick(n_all, 1100, 16), _pick(11 * gw, 512, LANES), "in_proj",
                              w_side=w_gate_cols)
        proj_s = proj[n_prompt:].reshape(bs, tp, 11 * gw)[:, :ts]

        oa_p = _attn_prompt(proj, lams, bias_tiles, gn_a, bp, seq, nh, lam_init)
        q_s = proj_s[..., :gw].reshape(bs, ts, nh, HEAD_DIM) * (HALF_DIM ** -0.5 * LOG2E)
        q2 = (jnp.transpose(q_s, (0, 2, 1, 3))[:, None] * half_mask.astype(F32)[None, :, None, None, :])
        q2 = q2.reshape(bs, 2 * nh * ts, HEAD_DIM).astype(BF16)
        k_s = proj_s[..., gw:2 * gw]
        v_s = proj_s[..., 2 * gw:3 * gw]
        new_rows = lambda a: jnp.pad(a.reshape(bs, ts * nh, HEAD_DIM), ((0, 0), (0, (page - ts) * nh), (0, 0)))
        oa_s = _attn_sample(l, q2, cache_k4, cache_v4, new_rows(k_s), new_rows(v_s), page_table, bias_s, lams, gn_a,
                            lam_init)
        oa_s = jnp.transpose(oa_s.reshape(bs, nh, ts, HEAD_DIM), (0, 2, 1, 3)).reshape(bs, ts, gw)
        oa_s = jnp.pad(oa_s, ((0, 0), (0, tp - ts), (0, 0))).reshape(bs * tp, gw).astype(BF16)
        oa = lax.dynamic_update_slice(oa_p, oa_s, (n_prompt, 0))

        wup = jnp.pad(gla_w_up[l], ((0, LANES - GLA_RANK), (0, 0)))
        wup_hi, wup_lo = _split_bf16(wup)
        gb = gla_b[l][None]
        gn_b = gla_norm_g[l][None]
        s0_s = jnp.transpose(state_gla[l].reshape(bs, nh // 2, 2 * HALF_DIM, HEAD_DIM), (0, 1, 3, 2))
        s0_p = jnp.zeros((bp, nh // 2, HEAD_DIM, LANES), F32)
        ob, sf_p = _gla(proj, gproj, wup_hi, wup_lo, gb, gn_b, s0_p, 0, bp, seq, nh, seq)
        ob, sf_s = _gla(proj, gproj, wup_hi, wup_lo, gb, gn_b, s0_s, n_prompt, bs, tp, nh, ts, prev=(ob,))
        unpair = lambda s: jnp.transpose(s, (0, 1, 3, 2)).reshape(s.shape[0], nh, HALF_DIM, HEAD_DIM)

        bs_t = jnp.transpose(cm_bs[l])
        cw, lg, lb = conv_w[l], cm_ln_g[l][None], cm_ln_b[l][None]
        oc, od, cs_p, _ = _conv_mlp(proj, cw, jnp.zeros((bp, CONV_WIDTH - 1, gw), F32), lg, lb, cm_ws[l], bs_t,
                                    0, bp, seq, nh, seq, False)
        oc, od, cs_s, vd_s = _conv_mlp(proj, cw, state_conv[l], lg, lb, cm_ws[l], bs_t, n_prompt, bs, tp, nh, ts, True,
                                       prev=(oc, od))

        mix = _matmul([oa, ob, oc, od], w_out[l].astype(BF16), d, tm_proj, _pick(d, 1024, LANES), "out_proj")
        wr = jnp.pad(jnp.concatenate([router_group[l], router_expert[l]], axis=1),
                     ((0, 0), (0, LANES - N_GROUPS - N_EXPERTS)))
        wr_hi, wr_lo = _split_bf16(wr)
        x1, ridx, rgate = _ln_router(x_parts, mix, ln1_g[l][None], ln1_b[l][None], wr_hi, wr_lo, alpha)

        dest, item_e, item_n, n_used = _route_plan(ridx, n_items)
        xs = _dispatch(x1, dest, n_items * MOE_ITEM_ROWS)
        ys = _experts(xs, item_e, item_n, n_used, w_gate, w_up, w_down, l, n_items)
        x, xb = _combine(x1, rgate, dest, ys, ln2_g[l][None], ln2_b[l][None], alpha)
        x_parts = (x, None)

        heads = lambda a, n, t: a.reshape(n, t, nh, HEAD_DIM)
        outs["kp"].append(heads(proj[:n_prompt, gw:2 * gw], bp, seq))
        outs["vp"].append(heads(proj[:n_prompt, 2 * gw:3 * gw], bp, seq))
        outs["ks"].append(heads(k_s, bs, ts))
        outs["vs"].append(heads(v_s, bs, ts))
        outs["gp"].append(unpair(sf_p))
        outs["gs"].append(unpair(sf_s))
        outs["cp"].append(cs_p)
        outs["cs"].append(cs_s)
        outs["ds"].append(vd_s.reshape(bs, tp, gw)[:, :ts])

    y_prompt = x[:n_prompt].reshape(bp, seq, d)
    y_sample = x[n_prompt:].reshape(bs, tp, d)[:, :ts]
    st = lambda k: jnp.stack(outs[k])
    return (y_prompt, y_sample, st("kp"), st("vp"), st("ks"), st("vs"), st("gp"), st("gs"),
            st("cp"), st("cs"), st("ds"))
```

```python
import functools
import math

import jax
import jax.numpy as jnp
from jax import lax
from jax.experimental import pallas as pl
from jax.experimental.pallas import tpu as pltpu

F32 = jnp.float32
BF16 = jnp.bfloat16

HEAD_DIM = 128
HALF_DIM = HEAD_DIM // 2
GLA_RANK = 16
GLA_TAU = 16.0
GLA_CHUNK = 64
CONV_WIDTH = 3
CHUNK_D = 128
REL_BUCKETS = 32
REL_MAX_DIST = 128
N_GROUPS = 4
EXPERTS_PER_GROUP = 8
N_EXPERTS = N_GROUPS * EXPERTS_PER_GROUP
TOP_K = 2
NORM_EPS = 1e-5
MASK_VALUE = -1e30
LOG2E = 1.4426950408889634

LANES = 128
SAMPLE_ROWS = 64
ATTN_TILE = 256
ATTN_HEADS = 4
SAMPLE_PAGES_PER_STEP = 8
MOE_ITEM_ROWS = 768
MOE_SUB_ROWS = 128
MOE_HIDDEN_TILE = 256
MOE_DOWN_STEPS = 4
VMEM_LIMIT = 52 * 1024 * 1024


def _pick(n, target, mult):
    if n <= target:
        return n
    best = None
    for d in range(mult, target + 1, mult):
        if n % d == 0:
            best = d
    assert best is not None, (n, target, mult)
    return best


def _params(*sem):
    return pltpu.CompilerParams(dimension_semantics=sem, vmem_limit_bytes=VMEM_LIMIT)


def _dot(a, b):
    return jnp.dot(a, b, preferred_element_type=F32)


def _dot_nt(a, b):
    return lax.dot_general(a, b, (((1,), (1,)), ((), ())), preferred_element_type=F32)


def _dot_tn(a, b):
    return lax.dot_general(a, b, (((0,), (0,)), ((), ())), preferred_element_type=F32)


def _split_bf16(x):
    hi = x.astype(BF16)
    lo = (x - hi.astype(F32)).astype(BF16)
    return hi, lo


def _sigmoid(x):
    return 1.0 / (1.0 + jnp.exp(-x))


def _mm_kernel(*refs, n_in, n_side):
    w_side = refs[2 * n_in:2 * n_in + n_side]
    o_ref = refs[2 * n_in + n_side]
    o_side = refs[2 * n_in + n_side + 1:]
    acc = None
    for x_ref, w_ref in zip(refs[:n_in], refs[n_in:2 * n_in]):
        d = _dot(x_ref[...], w_ref[...])
        acc = d if acc is None else acc + d
    o_ref[...] = acc.astype(o_ref.dtype)

    if n_side:
        @pl.when(pl.program_id(1) == 0)
        def _():
            o_side[0][...] = _dot(refs[0][...], w_side[0][...])


def _matmul(xs, w, n_out, tm, tn, name, w_side=None):
    n_in = len(xs)
    n_side = 0 if w_side is None else 1
    m, kg = xs[0].shape
    grid = (m // tm, n_out // tn)
    in_specs = [pl.BlockSpec((tm, kg), lambda i, j: (i, 0)) for _ in xs]
    in_specs += [pl.BlockSpec((kg, tn), functools.partial(lambda i, j, g: (g, j), g=g)) for g in range(n_in)]
    out_specs = [pl.BlockSpec((tm, tn), lambda i, j: (i, j))]
    out_shape = [jax.ShapeDtypeStruct((m, n_out), F32)]
    side_args = ()
    if n_side:
        in_specs.append(pl.BlockSpec((kg, LANES), lambda i, j: (0, 0)))
        out_specs.append(pl.BlockSpec((tm, LANES), lambda i, j: (i, 0)))
        out_shape.append(jax.ShapeDtypeStruct((m, LANES), F32))
        side_args = (w_side,)
    out = pl.pallas_call(
        functools.partial(_mm_kernel, n_in=n_in, n_side=n_side),
        grid=grid,
        in_specs=in_specs,
        out_specs=out_specs,
        out_shape=out_shape,
        compiler_params=_params("parallel", "arbitrary"),
        name=name,
    )(*xs, *([w] * n_in), *side_args)
    return out if n_side else out[0]


def _w_in_kernel(a_ref, b_ref, w_ref, g_ref, *, gate_tile):
    j = pl.program_id(1)

    @pl.when(j < gate_tile)
    def _():
        w_ref[...] = a_ref[...].astype(BF16)

    @pl.when(j >= gate_tile)
    def _():
        w_ref[...] = jnp.concatenate([a_ref[:, GLA_RANK:], b_ref[:, :GLA_RANK]], axis=1).astype(BF16)

    @pl.when(j == gate_tile)
    def _():
        first = a_ref[:, :LANES]
        lane = lax.broadcasted_iota(jnp.int32, first.shape, 1)
        g_ref[...] = jnp.where(lane < GLA_RANK, first, 0.0).astype(BF16)


def _w_in_layout(w_in, layer, gw):
    d = w_in.shape[1]
    tn = gw
    tk = _pick(d, 512, 16)
    gate_tile = 5
    per = tn // LANES
    return pl.pallas_call(
        functools.partial(_w_in_kernel, gate_tile=gate_tile),
        grid=(d // tk, 11),
        in_specs=[pl.BlockSpec((None, tk, tn), lambda i, j: (layer, i, j)),
                  pl.BlockSpec((None, tk, LANES), lambda i, j: (layer, i, jnp.maximum(j, gate_tile) * per + per))],
        out_specs=[pl.BlockSpec((tk, tn), lambda i, j: (i, j)), pl.BlockSpec((tk, LANES), lambda i, j: (i, 0))],
        out_shape=[jax.ShapeDtypeStruct((d, 11 * gw), BF16), jax.ShapeDtypeStruct((d, LANES), BF16)],
        compiler_params=_params("parallel", "arbitrary"),
        name="w_in_layout",
    )(w_in, w_in)


def _rel_bias(dist, rel_table):
    n = jnp.maximum(dist, 0)
    max_exact = REL_BUCKETS // 2
    nf = jnp.maximum(n, 1).astype(F32)
    large = max_exact + (jnp.log(nf / max_exact) / math.log(REL_MAX_DIST / max_exact)
                         * (REL_BUCKETS - max_exact)).astype(jnp.int32)
    bucket = jnp.where(n < max_exact, n, jnp.minimum(large, REL_BUCKETS - 1))
    onehot = (bucket[..., None] == jnp.arange(REL_BUCKETS, dtype=bucket.dtype)).astype(F32)
    return jnp.einsum("...k,kh->...h", onehot, rel_table.astype(F32), precision=lax.Precision.HIGHEST)


def _prompt_bias_tiles(rel_table, t):
    r = jnp.arange(t)[:, None]
    c = jnp.arange(t)[None, :]
    tiles = []
    for delta in range(3):
        dist = delta * t + r - c
        b = _rel_bias(dist, rel_table) * LOG2E
        tiles.append(jnp.where((dist >= 0)[..., None], b, MASK_VALUE))
    return jnp.transpose(jnp.stack(tiles), (3, 0, 1, 2))


def _lam_value(lq1, lk1, lq2, lk2, lam_init):
    return (jnp.exp(jnp.sum(lq1[...] * lk1[...], axis=1, keepdims=True))
            - jnp.exp(jnp.sum(lq2[...] * lk2[...], axis=1, keepdims=True)) + lam_init)


def _rms_head(o, gain, post):
    return o * lax.rsqrt(jnp.mean(o * o, axis=1, keepdims=True) + NORM_EPS) * gain * post


def _attn_prompt_kernel(lq1, lk1, lq2, lk2, q_ref, k_ref, v_ref, bias_ref, gn_ref, o_ref, *, t, lam_init):
    qi = pl.program_id(2)
    n_grp = bias_ref.shape[0]
    lam = _lam_value(lq1, lk1, lq2, lk2, lam_init)
    lane = lax.broadcasted_iota(jnp.int32, (t, HEAD_DIM), 1)
    heads = []
    for h in range(n_grp):
        q = q_ref[:, h * HEAD_DIM:(h + 1) * HEAD_DIM] * (HALF_DIM ** -0.5 * LOG2E)
        heads.append(jnp.concatenate([jnp.where(lane < HALF_DIM, q, 0.0), jnp.where(lane >= HALF_DIM, q, 0.0)],
                                     axis=0).astype(BF16))

    def body(j, carry):
        start = pl.multiple_of(j * t, t)
        tile = jnp.minimum(qi - j, 2)
        out = []
        for h in range(n_grp):
            m, s, acc = carry[h]
            cols = slice(h * HEAD_DIM, (h + 1) * HEAD_DIM)
            kt = k_ref[pl.ds(start, t), cols].astype(BF16)
            vt = v_ref[pl.ds(start, t), cols].astype(BF16)
            bt = bias_ref[h, tile]
            l = _dot_nt(heads[h], kt) + jnp.concatenate([bt, bt], axis=0)
            m_new = jnp.maximum(m, jnp.max(l, axis=1, keepdims=True))
            a = jnp.exp2(m - m_new)
            p = jnp.exp2(l - m_new)
            out.append((m_new, a * s + jnp.sum(p, axis=1, keepdims=True), a * acc + _dot(p.astype(BF16), vt)))
        return tuple(out)

    init = (jnp.full((2 * t, 1), MASK_VALUE, F32), jnp.zeros((2 * t, 1), F32), jnp.zeros((2 * t, HEAD_DIM), F32))
    res = lax.fori_loop(0, qi + 1, body, (init,) * n_grp)
    for h in range(n_grp):
        _, s, acc = res[h]
        o = acc / s
        o = o[:t] - lam * o[t:]
        o_ref[:, h * HEAD_DIM:(h + 1) * HEAD_DIM] = _rms_head(o, gn_ref[...], 1.0 - lam_init).astype(o_ref.dtype)


def _attn_prompt(proj, lams, bias_tiles, gn, n_seq, seq, n_heads, lam_init):
    t = ATTN_TILE
    n_grp = math.gcd(ATTN_HEADS, n_heads)
    assert seq % t == 0 and t >= REL_MAX_DIST
    nq = seq // t
    pair = n_grp * HEAD_DIM
    groups = n_heads // n_grp
    lam_spec = pl.BlockSpec((1, HALF_DIM), lambda b, h, i: (0, 0))
    return pl.pallas_call(
        functools.partial(_attn_prompt_kernel, t=t, lam_init=lam_init),
        grid=(n_seq, groups, nq),
        in_specs=[lam_spec] * 4 + [
            pl.BlockSpec((t, pair), lambda b, h, i: (b * nq + i, h)),
            pl.BlockSpec((seq, pair), lambda b, h, i: (b, groups + h)),
            pl.BlockSpec((seq, pair), lambda b, h, i: (b, 2 * groups + h)),
            pl.BlockSpec((n_grp, 3, t, t), lambda b, h, i: (h, 0, 0, 0)),
            pl.BlockSpec((1, HEAD_DIM), lambda b, h, i: (0, 0)),
        ],
        out_specs=pl.BlockSpec((t, pair), lambda b, h, i: (b * nq + i, h)),
        out_shape=jax.ShapeDtypeStruct((proj.shape[0], n_heads * HEAD_DIM), BF16),
        compiler_params=_params("parallel", "parallel", "arbitrary"),
        name="attn_prompt",
    )(*lams, proj, proj, proj, bias_tiles, gn)


def _attn_sample_kernel(pt_ref, lq1, lk1, lq2, lk2, q_ref, *refs, n_pp, lam_init):
    kc_refs, vc_refs = refs[:n_pp], refs[n_pp:2 * n_pp]
    kn_ref, vn_ref, bias_ref, gn_ref, o_ref, m_ref, s_ref, acc_ref = refs[2 * n_pp:]
    p = pl.program_id(1)
    n_steps = pl.num_programs(1)

    @pl.when(p == 0)
    def _():
        m_ref[...] = jnp.full(m_ref.shape, MASK_VALUE, F32)
        s_ref[...] = jnp.zeros(s_ref.shape, F32)
        acc_ref[...] = jnp.zeros(acc_ref.shape, F32)

    q = q_ref[0]

    def step(ks, vs, biases):
        ls = [_dot_nt(q, k.astype(BF16)) + b for k, b in zip(ks, biases)]
        m = m_ref[...]
        m_new = m
        for l in ls:
            m_new = jnp.maximum(m_new, jnp.max(l, axis=1, keepdims=True))
        a = jnp.exp2(m - m_new)
        s = a * s_ref[...]
        acc = a * acc_ref[...]
        for l, v in zip(ls, vs):
            pr = jnp.exp2(l - m_new)
            s = s + jnp.sum(pr, axis=1, keepdims=True)
            acc = acc + _dot(pr.astype(BF16), v.astype(BF16))
        s_ref[...] = s
        acc_ref[...] = acc
        m_ref[...] = m_new

    last = p == n_steps - 1
    tables = [0] * (n_pp - 1) + [jnp.where(last, 1, 0)]
    step([r[0, 0] for r in kc_refs], [r[0, 0] for r in vc_refs], [bias_ref[t] for t in tables])

    @pl.when(last)
    def _():
        step([kn_ref[0]], [vn_ref[0]], [bias_ref[2]])
        lam = _lam_value(lq1, lk1, lq2, lk2, lam_init)
        o = acc_ref[...] / s_ref[...]
        half = o.shape[0] // 2
        o_ref[0] = _rms_head(o[:half] - lam * o[half:], gn_ref[...], 1.0 - lam_init)


def _attn_sample(layer, q2, cache_k, cache_v, k_new, v_new, page_table, bias, lams, gn, lam_init):
    n_seq, n_pages = page_table.shape
    prow = cache_k.shape[2]
    rows = q2.shape[1]
    n_pp = SAMPLE_PAGES_PER_STEP if n_pages % SAMPLE_PAGES_PER_STEP == 0 else 1
    lam_spec = pl.BlockSpec((1, HALF_DIM), lambda b, p, pt: (0, 0))
    cache_specs = [pl.BlockSpec((1, 1, prow, HEAD_DIM),
                                functools.partial(lambda b, p, pt, i: (layer, pt[b, p * n_pp + i], 0, 0), i=i))
                   for i in range(n_pp)]
    new_spec = pl.BlockSpec((1, prow, HEAD_DIM), lambda b, p, pt: (b, 0, 0))
    grid_spec = pltpu.PrefetchScalarGridSpec(
        num_scalar_prefetch=1,
        grid=(n_seq, n_pages // n_pp),
        in_specs=[lam_spec] * 4 + [pl.BlockSpec((1, rows, HEAD_DIM), lambda b, p, pt: (b, 0, 0))]
        + cache_specs + cache_specs + [
            new_spec, new_spec,
            pl.BlockSpec((3, rows, prow), lambda b, p, pt: (0, 0, 0)),
            pl.BlockSpec((1, HEAD_DIM), lambda b, p, pt: (0, 0)),
        ],
        out_specs=pl.BlockSpec((1, rows // 2, HEAD_DIM), lambda b, p, pt: (b, 0, 0)),
        scratch_shapes=[pltpu.VMEM((rows, 1), F32), pltpu.VMEM((rows, 1), F32), pltpu.VMEM((rows, HEAD_DIM), F32)],
    )
    return pl.pallas_call(
        functools.partial(_attn_sample_kernel, n_pp=n_pp, lam_init=lam_init),
        grid_spec=grid_spec,
        out_shape=jax.ShapeDtypeStruct((n_seq, rows // 2, HEAD_DIM), F32),
        compiler_params=_params("parallel", "arbitrary"),
        name="attn_sample",
    )(page_table, *lams, q2, *([cache_k] * n_pp), *([cache_v] * n_pp), k_new, v_new, bias, gn)


def _gla_kernel(q_ref, k_ref, v_ref, r_ref, gp_ref, wh_ref, wl_ref, gb_ref, gn_ref, s0_ref, *rest,
                chunk, n_chunks, n_heads, valid):
    o_ref, sfin_ref, st_ref = rest[-3:]
    t = pl.program_id(1)

    @pl.when(t == 0)
    def _():
        st_ref[...] = s0_ref[0]

    half = chunk // 2
    quarter = chunk // 4
    row = lax.broadcasted_iota(jnp.int32, (chunk, chunk), 0)
    col = lax.broadcasted_iota(jnp.int32, (chunk, chunk), 1)
    tril = row >= col
    same_half = (row < half) == (col < half)
    tri_bf = jnp.where(tril, 1.0, 0.0).astype(BF16)
    lane = lax.broadcasted_iota(jnp.int32, (chunk, LANES), 1)
    lane_sq = lax.broadcasted_iota(jnp.int32, (HEAD_DIM, LANES), 1)

    for c in range(n_chunks):
        rs = slice(c * chunk, (c + 1) * chunk)
        gh, gl = _split_bf16(gp_ref[rs, :])
        z = _dot(gh, wh_ref[...]) + _dot(gl, wh_ref[...]) + _dot(gh, wl_ref[...]) + gb_ref[...]
        g = -(jnp.maximum(-z, 0.0) + jnp.log(1.0 + jnp.exp(-jnp.abs(z)))) * (1.0 / GLA_TAU)
        k = k_ref[rs, :]
        if valid < chunk:
            ok = lax.broadcasted_iota(jnp.int32, g.shape, 0) < valid
            g = jnp.where(ok, g, 0.0)
            k = jnp.where(ok, k, 0.0)
        gh, gl = _split_bf16(g)
        b = _dot(tri_bf, gh) + _dot(tri_bf, gl)
        b_last = b[chunk - 1:chunk, :]
        b_half = b[half - 1:half, :]
        in_lo = lax.broadcasted_iota(jnp.int32, b.shape, 0) < half
        b_near = jnp.where(in_lo, b[quarter - 1:quarter, :], b[half + quarter - 1:half + quarter, :])
        q = q_ref[rs, :] * (HALF_DIM ** -0.5)
        q_state = q * jnp.exp(b)
        q_near = q * jnp.exp(b - b_near)
        k_near = (k * jnp.exp(b_near - b)).astype(BF16)
        q_far = q * jnp.exp(jnp.minimum(b - b_half, 0.0))
        k_far = (k * jnp.exp(jnp.minimum(b_half - b, 0.0))).astype(BF16)
        k_end = (k * jnp.exp(b_last - b)).astype(BF16)
        decay = jnp.exp(b_last)

        for hp in range(n_heads // 2):
            ls = slice(hp * LANES, (hp + 1) * LANES)
            st = st_ref[hp]
            st_bf = st.astype(BF16)
            upd = []
            for par in range(2):
                h = 2 * hp + par
                mine = (lane < HALF_DIM) if par == 0 else (lane >= HALF_DIM)
                hs = slice(h * HEAD_DIM, (h + 1) * HEAD_DIM)
                v_h = v_ref[rs, hs].astype(BF16)
                att = jnp.where(same_half,
                                _dot_nt(jnp.where(mine, q_near[:, ls], 0.0).astype(BF16), k_near[:, ls]),
                                _dot_nt(jnp.where(mine, q_far[:, ls], 0.0).astype(BF16), k_far[:, ls]))
                att = jnp.where(tril, att, 0.0).astype(BF16)
                o = _dot(att, v_h) + _dot_nt(jnp.where(mine, q_state[:, ls], 0.0).astype(BF16), st_bf)
                r_h = r_ref[rs, hs]
                o_ref[rs, hs] = _rms_head(o, gn_ref[...], r_h * _sigmoid(r_h)).astype(o_ref.dtype)
                upd.append(_dot_tn(v_h, k_end[:, ls]))
            st_ref[hp] = st * decay[:, ls] + jnp.where(lane_sq < HALF_DIM, upd[0], upd[1])

    @pl.when(t == pl.num_programs(1) - 1)
    def _():
        sfin_ref[0] = st_ref[...]


def _into(prev):
    specs = [pl.BlockSpec(memory_space=pl.ANY) for _ in prev]
    return specs, tuple(prev)


def _gla(proj, gproj, wup_hi, wup_lo, gla_b, gn, s0, row0, n_seq, seq, n_heads, valid, prev=()):
    gw = n_heads * HEAD_DIM
    hk = n_heads * HALF_DIM
    chunk = min(GLA_CHUNK, seq)
    tt = _pick(seq, 256, chunk)
    nt = seq // tt
    r0 = row0 // tt
    assert row0 % tt == 0 and chunk % 4 == 0
    qcol = 3 * gw // hk
    row_spec = lambda width, colblk: pl.BlockSpec((tt, width), lambda b, t: (r0 + b * nt + t, colblk))
    const = lambda shape: pl.BlockSpec(shape, lambda b, t: tuple(0 for _ in shape))
    st_spec = pl.BlockSpec((1, n_heads // 2, HEAD_DIM, LANES), lambda b, t: (b, 0, 0, 0))
    prev_specs, prev_args = _into(prev)
    n_in = 10
    return pl.pallas_call(
        functools.partial(_gla_kernel, chunk=chunk, n_chunks=tt // chunk, n_heads=n_heads, valid=valid),
        grid=(n_seq, nt),
        in_specs=[row_spec(hk, qcol), row_spec(hk, qcol + 1), row_spec(gw, 4), row_spec(gw, 5),
                  pl.BlockSpec((tt, LANES), lambda b, t: (r0 + b * nt + t, 0)),
                  const((LANES, hk)), const((LANES, hk)), const((1, hk)), const((1, HEAD_DIM)), st_spec] + prev_specs,
        out_specs=[pl.BlockSpec((tt, gw), lambda b, t: (r0 + b * nt + t, 0)), st_spec],
        out_shape=[jax.ShapeDtypeStruct((proj.shape[0], gw), BF16),
                   jax.ShapeDtypeStruct((n_seq, n_heads // 2, HEAD_DIM, LANES), F32)],
        input_output_aliases={n_in + k: k for k in range(len(prev))},
        scratch_shapes=[pltpu.VMEM((n_heads // 2, HEAD_DIM, LANES), F32)],
        compiler_params=_params("parallel", "arbitrary"),
        name="gla",
    )(proj, proj, proj, proj, gproj, wup_hi, wup_lo, gla_b, gn, s0, *prev_args)


def _conv_mlp_kernel(cb_ref, cc_ref, ch_ref, du_ref, dv_ref, cw_ref, buf_ref, lg_ref, lb_ref, ws_ref, bs_ref, *rest,
                     tt, chunk, n_heads, valid):
    oc_ref, od_ref, cs_ref, vd_ref, carry_ref = rest[-5:]
    t = pl.program_id(1)

    @pl.when(t == 0)
    def _():
        carry_ref[...] = buf_ref[0]

    z = cc_ref[...] * ch_ref[...]
    row = lax.broadcasted_iota(jnp.int32, z.shape, 0)
    c0 = carry_ref[0:1, :]
    c1 = carry_ref[1:2, :]
    z1 = jnp.where(row == 0, c1, pltpu.roll(z, 1, 0))
    z2 = jnp.where(row == 0, c0, jnp.where(row == 1, c1, pltpu.roll(z, 2, 0)))
    w = cw_ref[...]
    oc_ref[...] = (cb_ref[...] * (z2 * w[0:1, :] + z1 * w[1:2, :] + z * w[2:3, :])).astype(oc_ref.dtype)
    if valid >= 2:
        carry_ref[...] = z[valid - 2:valid, :]
    else:
        carry_ref[...] = jnp.concatenate([c1, z[0:1, :]], axis=0)

    @pl.when(t == pl.num_programs(1) - 1)
    def _():
        cs_ref[0] = carry_ref[...]

    x = dv_ref[...]
    mu = jnp.mean(x, axis=1, keepdims=True)
    xc = x - mu
    var = jnp.mean(xc * xc, axis=1, keepdims=True)
    vd = xc * lax.rsqrt(var + NORM_EPS) * lg_ref[...] + lb_ref[...]
    vd_ref[...] = vd
    vd_bf = vd.astype(BF16)
    r2 = lax.broadcasted_iota(jnp.int32, (chunk, chunk), 0)
    c2 = lax.broadcasted_iota(jnp.int32, (chunk, chunk), 1)
    for g in range(n_heads):
        wc = jnp.where(r2 >= c2, ws_ref[g, :chunk, :chunk], 0.0).astype(BF16)
        bias = bs_ref[:chunk, g:g + 1]
        cols = slice(g * HEAD_DIM, (g + 1) * HEAD_DIM)
        for c in range(tt // chunk):
            rs = slice(c * chunk, (c + 1) * chunk)
            zc = _dot(wc, vd_bf[rs, cols]) + bias
            od_ref[rs, cols] = (du_ref[rs, cols] * zc).astype(od_ref.dtype)


def _conv_mlp(proj, conv_w, buf, ln_g, ln_b, ws, bs_t, row0, n_seq, seq, n_heads, valid, keep_vd, prev=()):
    gw = n_heads * HEAD_DIM
    chunk = min(CHUNK_D, seq)
    tt = _pick(seq, 256, chunk)
    nt = seq // tt
    r0 = row0 // tt
    assert row0 % tt == 0 and (valid == seq or nt == 1)
    row_spec = lambda colblk: pl.BlockSpec((tt, gw), lambda b, t: (r0 + b * nt + t, colblk))
    const = lambda shape: pl.BlockSpec(shape, lambda b, t: tuple(0 for _ in shape))
    out_rows = pl.BlockSpec((tt, gw), lambda b, t: (r0 + b * nt + t, 0))
    vd_spec = pl.BlockSpec((tt, gw), (lambda b, t: (b * nt + t, 0)) if keep_vd else (lambda b, t: (0, 0)))
    state_spec = pl.BlockSpec((1, CONV_WIDTH - 1, gw), lambda b, t: (b, 0, 0))
    prev_specs, prev_args = _into(prev)
    n_in = 11
    return pl.pallas_call(
        functools.partial(_conv_mlp_kernel, tt=tt, chunk=chunk, n_heads=n_heads, valid=min(valid, tt)),
        grid=(n_seq, nt),
        in_specs=[row_spec(6), row_spec(7), row_spec(8), row_spec(9), row_spec(10),
                  const((CONV_WIDTH, gw)), state_spec, const((1, gw)), const((1, gw)),
                  const((n_heads, CHUNK_D, CHUNK_D)), const((CHUNK_D, n_heads))] + prev_specs,
        out_specs=[out_rows, out_rows, state_spec, vd_spec],
        out_shape=[jax.ShapeDtypeStruct((proj.shape[0], gw), BF16), jax.ShapeDtypeStruct((proj.shape[0], gw), BF16),
                   jax.ShapeDtypeStruct((n_seq, CONV_WIDTH - 1, gw), F32),
                   jax.ShapeDtypeStruct((n_seq * seq if keep_vd else tt, gw), F32)],
        input_output_aliases={n_in + k: k for k in range(len(prev))},
        scratch_shapes=[pltpu.VMEM((CONV_WIDTH - 1, gw), F32)],
        compiler_params=_params("parallel", "arbitrary"),
        name="conv_mlp",
    )(proj, proj, proj, proj, proj, conv_w, buf, ln_g, ln_b, ws, bs_t, *prev_args)


def _layer_norm_rows(x, g, b):
    mu = jnp.mean(x, axis=1, keepdims=True)
    xc = x - mu
    var = jnp.mean(xc * xc, axis=1, keepdims=True)
    return xc * lax.rsqrt(var + NORM_EPS) * g + b


def _ln_router_kernel(xa_ref, xb_ref, mix_ref, g_ref, b_ref, wh_ref, wl_ref, y_ref, idx_ref, gate_ref, *, alpha, split):
    x = jnp.where(pl.program_id(0) < split, xa_ref[...], xb_ref[...])
    y = _layer_norm_rows(alpha * x + mix_ref[...], g_ref[...], b_ref[...])
    y_ref[...] = y
    yh, yl = _split_bf16(y)
    lg = _dot(yh, wh_ref[...]) + _dot(yl, wh_ref[...]) + _dot(yh, wl_ref[...])
    lane = lax.broadcasted_iota(jnp.int32, lg.shape, 1)
    neg = -jnp.inf
    glog = jnp.where(lane < N_GROUPS, lg, neg)
    gmax = jnp.max(glog, axis=1, keepdims=True)
    g_star = jnp.min(jnp.where(glog == gmax, lane, LANES), axis=1, keepdims=True)
    p_top = 1.0 / jnp.sum(jnp.exp(glog - gmax), axis=1, keepdims=True)
    lo = N_GROUPS + EXPERTS_PER_GROUP * g_star
    w1 = jnp.where((lane >= lo) & (lane < lo + EXPERTS_PER_GROUP), lg, neg)
    v1 = jnp.max(w1, axis=1, keepdims=True)
    i1 = jnp.min(jnp.where(w1 == v1, lane, LANES), axis=1, keepdims=True)
    w2 = jnp.where(lane == i1, neg, w1)
    v2 = jnp.max(w2, axis=1, keepdims=True)
    i2 = jnp.min(jnp.where(w2 == v2, lane, LANES), axis=1, keepdims=True)
    e21 = jnp.exp(v2 - v1)
    gate1 = p_top / (1.0 + e21)
    gate2 = p_top * e21 / (1.0 + e21)
    idx_ref[...] = jnp.where(lane == 0, i1 - N_GROUPS, jnp.where(lane == 1, i2 - N_GROUPS, 0))
    gate_ref[...] = jnp.where(lane == 0, gate1, jnp.where(lane == 1, gate2, 0.0))


def _ln_router(x_parts, mix, g, b, wr_hi, wr_lo, alpha):
    m, d = mix.shape
    tm = _pick(math.gcd(m, x_parts[0].shape[0]), 256, 8)
    xa, xb = x_parts if x_parts[1] is not None else (x_parts[0], x_parts[0])
    split = xa.shape[0] // tm if x_parts[1] is not None else m // tm
    rows = pl.BlockSpec((tm, d), lambda i: (i, 0))
    const = lambda shape: pl.BlockSpec(shape, lambda i: (0, 0))
    small = pl.BlockSpec((tm, LANES), lambda i: (i, 0))
    return pl.pallas_call(
        functools.partial(_ln_router_kernel, alpha=alpha, split=split),
        grid=(m // tm,),
        in_specs=[pl.BlockSpec((tm, d), lambda i: (jnp.minimum(i, split - 1), 0)),
                  pl.BlockSpec((tm, d), lambda i: (jnp.maximum(i - split, 0), 0)),
                  rows, const((1, d)), const((1, d)), const((d, LANES)), const((d, LANES))],
        out_specs=[rows, small, small],
        out_shape=[jax.ShapeDtypeStruct((m, d), F32),
                   jax.ShapeDtypeStruct((m, LANES), jnp.int32), jax.ShapeDtypeStruct((m, LANES), F32)],
        compiler_params=_params("parallel"),
        name="ln_router",
    )(xa, xb, mix, g, b, wr_hi, wr_lo)


SLAB_TILE = 8


def _token_copy(src, src_row, dst, dst_row, sem):
    rows = lambda r: pl.ds(pl.multiple_of(r * SLAB_TILE, SLAB_TILE), SLAB_TILE)
    return pltpu.make_async_copy(src.at[:, rows(src_row), :], dst.at[:, rows(dst_row), :], sem)


def _dispatch_kernel(dest_ref, x_ref, xs_hbm, stage, sem, *, tm):
    for p in range(stage.shape[0]):
        for c in range(SLAB_TILE):
            col = (p * SLAB_TILE + c) * LANES
            stage[p, pl.ds(c, tm, stride=SLAB_TILE), :] = x_ref[:, col:col + LANES]

    def copy(i, k):
        return _token_copy(stage, i, xs_hbm, dest_ref[0, 0, 2 * i + k], sem)

    def start(i, _):
        copy(i, 0).start()
        copy(i, 1).start()
        return 0

    def wait(i, _):
        copy(i, 0).wait()
        copy(i, 1).wait()
        return 0

    lax.fori_loop(0, tm, start, 0)
    lax.fori_loop(0, tm, wait, 0)


def _dispatch(x1, dest, n_rows):
    m, d = x1.shape
    planes = d // (SLAB_TILE * LANES)
    tm = _pick(m, 256, 8)
    dest3 = dest.reshape(m // tm, 1, TOP_K * tm)
    return pl.pallas_call(
        functools.partial(_dispatch_kernel, tm=tm),
        grid=(m // tm,),
        in_specs=[pl.BlockSpec((1, 1, TOP_K * tm), lambda i: (i, 0, 0), memory_space=pltpu.SMEM),
                  pl.BlockSpec((tm, d), lambda i: (i, 0))],
        out_specs=pl.BlockSpec(memory_space=pl.ANY),
        out_shape=jax.ShapeDtypeStruct((planes, n_rows * SLAB_TILE, LANES), F32),
        scratch_shapes=[pltpu.VMEM((planes, tm * SLAB_TILE, LANES), F32), pltpu.SemaphoreType.DMA(())],
        compiler_params=pltpu.CompilerParams(dimension_semantics=("arbitrary",), has_side_effects=True,
                                             vmem_limit_bytes=VMEM_LIMIT),
        name="moe_dispatch",
    )(dest3, x1)


def _expert_kernel(ie_ref, in_ref, nu_ref, x_ref, wg_ref, wu_ref, wd_ref, y_ref, xb, h_buf, *, planes, hs, sub, tiles):
    i = pl.program_id(0)
    s = pl.program_id(1)
    n_sub = (in_ref[i] + sub - 1) // sub
    live = i < nu_ref[0]
    per_blk = SLAB_TILE // tiles
    up0 = planes
    down0 = planes + hs
    kp = SLAB_TILE * LANES
    ht = h_buf.shape[2]

    def for_rows(fn):
        for k in range(1, xb.shape[1] // sub + 1):
            pl.when(n_sub == k)(functools.partial(fn, k * sub))

    @pl.when(live & (s < up0))
    def _():
        def convert(m):
            xb[s, :m, :] = jnp.concatenate([x_ref[pl.ds(c, m, stride=SLAB_TILE), :]
                                            for c in range(SLAB_TILE)], axis=1).astype(BF16)
        for_rows(convert)

    @pl.when(live & (s >= up0) & (s < down0))
    def _():
        wg = wg_ref[0].astype(BF16)
        wu = wu_ref[0].astype(BF16)

        def up(m):
            g = sum(_dot(xb[p, :m, :], wg[p * kp:(p + 1) * kp, :]) for p in range(planes))
            u = sum(_dot(xb[p, :m, :], wu[p * kp:(p + 1) * kp, :]) for p in range(planes))
            h_buf[s - up0, :m, :] = (g * _sigmoid(g) * u).astype(BF16)
        for_rows(up)

    @pl.when(live & (s >= down0))
    def _():
        wd = wd_ref[0].astype(BF16)
        c0 = ((s - down0) % per_blk) * tiles

        def down(m):
            y = sum(_dot(h_buf[j, :m, :], wd[j * ht:(j + 1) * ht, :]) for j in range(hs))
            for c in range(tiles):
                y_ref[pl.ds(c0 + c, m, stride=SLAB_TILE), :] = y[:, c * LANES:(c + 1) * LANES]
        for_rows(down)


def _experts(xs, item_e, item_n, n_used, w_gate, w_up, w_down, layer, n_items):
    d, hidden = w_gate.shape[-2:]
    n_exp = w_gate.shape[1]
    planes = xs.shape[0]
    ht = min(MOE_HIDDEN_TILE, hidden)
    hs = hidden // ht
    ns = MOE_DOWN_STEPS
    nt = d // ns
    tiles = nt // LANES
    assert SLAB_TILE % tiles == 0 and hidden % ht == 0
    per_blk = SLAB_TILE // tiles
    n_steps = planes + hs + ns
    wg3 = w_gate.reshape(-1, d, hidden)
    wu3 = w_up.reshape(-1, d, hidden)
    wd3 = w_down.reshape(-1, hidden, d)
    e0 = layer * n_exp
    rows = MOE_ITEM_ROWS

    def item(i, nu):
        return jnp.minimum(i, nu[0] - 1)

    def step(i, s, nu):
        return jnp.where(i < nu[0], s, n_steps - 1)

    def expert(i, ie, nu):
        return e0 + ie[item(i, nu)]

    def x_plane(i, s, ie, inn, nu):
        return jnp.minimum(step(i, s, nu), planes - 1), item(i, nu), 0

    def up_slice(i, s, ie, inn, nu):
        return expert(i, ie, nu), 0, jnp.clip(step(i, s, nu) - planes, 0, hs - 1)

    def down_slice(i, s, ie, inn, nu):
        return expert(i, ie, nu), 0, jnp.maximum(step(i, s, nu) - planes - hs, 0)

    def y_plane(i, s, ie, inn, nu):
        return jnp.maximum(step(i, s, nu) - planes - hs, 0) // per_blk, item(i, nu), 0

    grid_spec = pltpu.PrefetchScalarGridSpec(
        num_scalar_prefetch=3,
        grid=(n_items, n_steps),
        in_specs=[
            pl.BlockSpec((None, rows * SLAB_TILE, LANES), x_plane),
            pl.BlockSpec((1, d, ht), up_slice),
            pl.BlockSpec((1, d, ht), up_slice),
            pl.BlockSpec((1, hidden, nt), down_slice),
        ],
        out_specs=pl.BlockSpec((None, rows * SLAB_TILE, LANES), y_plane),
        scratch_shapes=[pltpu.VMEM((planes, rows, SLAB_TILE * LANES), BF16), pltpu.VMEM((hs, rows, ht), BF16)],
    )
    return pl.pallas_call(
        functools.partial(_expert_kernel, planes=planes, hs=hs, sub=MOE_SUB_ROWS, tiles=tiles),
        grid_spec=grid_spec,
        out_shape=jax.ShapeDtypeStruct(xs.shape, F32),
        compiler_params=_params("arbitrary", "arbitrary"),
        name="moe_experts",
    )(item_e, item_n, n_used, xs, wg3, wu3, wd3)


def _combine_kernel(dest_ref, next_ref, x_ref, gate_ref, g_ref, b_ref, ys_hbm, y_ref, yb_ref, stage0, stage1, sems,
                    *, tm, alpha):
    i = pl.program_id(0)
    slot = i % 2
    stages = (stage0, stage1)

    def copy(dests, t, k, sl):
        return _token_copy(ys_hbm, dests[0, 0, 2 * t + k], stages[k].at[sl], t, sems.at[sl])

    def start_tile(dests, sl):
        def start(t, _):
            copy(dests, t, 0, sl).start()
            copy(dests, t, 1, sl).start()
            return 0
        lax.fori_loop(0, tm, start, 0)

    @pl.when(i == 0)
    def _():
        start_tile(dest_ref, slot)

    @pl.when(i + 1 < pl.num_programs(0))
    def _():
        start_tile(next_ref, 1 - slot)

    def wait(t, _):
        copy(dest_ref, t, 0, slot).wait()
        copy(dest_ref, t, 1, slot).wait()
        return 0

    lax.fori_loop(0, tm, wait, 0)
    gates = gate_ref[...]
    g0 = gates[:, 0:1]
    g1 = gates[:, 1:2]
    tile = lambda stage, p, c: stage[slot, p, pl.ds(c, tm, stride=SLAB_TILE), :]
    ffn = jnp.concatenate([g0 * tile(stage0, p, c) + g1 * tile(stage1, p, c)
                           for p in range(stage0.shape[1]) for c in range(SLAB_TILE)], axis=1)
    y = _layer_norm_rows(alpha * x_ref[...] + ffn, g_ref[...], b_ref[...])
    y_ref[...] = y
    yb_ref[...] = y.astype(BF16)


def _combine(x1, gates, dest, ys, g, b, alpha):
    m, d = x1.shape
    tm = _pick(m, 256, 8)
    n_tiles = m // tm
    stage = pltpu.VMEM((2, ys.shape[0], tm * SLAB_TILE, LANES), F32)
    dest3 = dest.reshape(n_tiles, 1, TOP_K * tm)
    dest_spec = lambda off: pl.BlockSpec((1, 1, TOP_K * tm), lambda i: (jnp.minimum(i + off, n_tiles - 1), 0, 0),
                                         memory_space=pltpu.SMEM)
    rows = pl.BlockSpec((tm, d), lambda i: (i, 0))
    const = pl.BlockSpec((1, d), lambda i: (0, 0))
    return pl.pallas_call(
        functools.partial(_combine_kernel, tm=tm, alpha=alpha),
        grid=(n_tiles,),
        in_specs=[dest_spec(0), dest_spec(1), rows, pl.BlockSpec((tm, LANES), lambda i: (i, 0)), const, const,
                  pl.BlockSpec(memory_space=pl.ANY)],
        out_specs=[rows, rows],
        out_shape=[jax.ShapeDtypeStruct((m, d), F32), jax.ShapeDtypeStruct((m, d), BF16)],
        scratch_shapes=[stage, stage, pltpu.SemaphoreType.DMA((2,))],
        compiler_params=_params("arbitrary"),
        name="moe_combine",
    )(dest3, dest3, x1, gates, g, b, ys)


def _route_plan(idx, n_items):
    e_flat = idx[:, :TOP_K].reshape(-1)
    onehot = (e_flat[:, None] == jnp.arange(N_EXPERTS, dtype=jnp.int32)[None, :]).astype(jnp.int32)
    csum = jnp.cumsum(onehot, axis=0)
    rank = jnp.sum(onehot * csum, axis=1) - 1
    counts = csum[-1]
    items_e = (counts + MOE_ITEM_ROWS - 1) // MOE_ITEM_ROWS
    item_end = jnp.cumsum(items_e)
    item_start = item_end - items_e
    dest = (item_start[e_flat] * MOE_ITEM_ROWS + rank).astype(jnp.int32)
    ids = jnp.arange(n_items, dtype=jnp.int32)
    item_e = jnp.minimum(jnp.searchsorted(item_end, ids, side="right"), N_EXPERTS - 1).astype(jnp.int32)
    n_used = item_end[-1].astype(jnp.int32)
    item_n = jnp.clip(counts[item_e] - (ids - item_start[item_e]) * MOE_ITEM_ROWS, 0, MOE_ITEM_ROWS)
    item_n = jnp.where(ids < n_used, item_n, 0).astype(jnp.int32)
    return dest, item_e, item_n, n_used.reshape(1)


def kernel(x_prompt, x_sample, cache_k, cache_v, page_table, state_gla, state_conv, rel_table,
           w_in, w_out, lam_q1, lam_k1, lam_q2, lam_k2, diff_norm_g, gla_w_up, gla_b, gla_norm_g,
           conv_w, cm_ln_g, cm_ln_b, cm_ws, cm_bs, ln1_g, ln1_b, ln2_g, ln2_b,
           router_group, router_expert, w_gate, w_up, w_down):
    bp, seq, d = x_prompt.shape
    bs, ts, _ = x_sample.shape
    depth = w_in.shape[0]
    gw = w_out.shape[1] // 4
    nh = gw // HEAD_DIM
    hk = nh * HALF_DIM
    page = cache_k.shape[2]
    n_phys = cache_k.shape[1]
    tp = SAMPLE_ROWS
    n_prompt = bp * seq
    n_all = n_prompt + bs * tp
    alpha = (2.0 * depth) ** 0.25
    assert ts <= tp and nh % 2 == 0 and n_prompt % tp == 0

    x_parts = (x_prompt.reshape(n_prompt, d), jnp.pad(x_sample, ((0, 0), (0, tp - ts), (0, 0))).reshape(bs * tp, d))
    xb = jnp.concatenate([x_parts[0].astype(BF16), x_parts[1].astype(BF16)], axis=0)

    bias_tiles = _prompt_bias_tiles(rel_table, ATTN_TILE)
    t_idx = jnp.arange(ts)
    k_idx = jnp.arange(page)
    dist_far = jnp.full((ts, page), 2 * page, jnp.int32)
    dist_last = page + t_idx[:, None] - k_idx[None, :]
    dist_new = t_idx[:, None] - k_idx[None, :]
    ok = jnp.stack([dist_far > 0, dist_last > 0, (dist_new >= 0) & (k_idx[None, :] < ts)])
    b3 = _rel_bias(jnp.stack([dist_far, dist_last, dist_new]), rel_table) * LOG2E
    same_head = jnp.eye(nh, dtype=bool)
    b3 = jnp.where(ok[:, None, :, :, None] & same_head[None, :, None, None, :],
                   jnp.transpose(b3, (0, 3, 1, 2))[..., None], MASK_VALUE)
    bias_s = jnp.tile(b3.reshape(3, nh * ts, page * nh), (1, 2, 1))
    half_mask = (jnp.arange(HEAD_DIM)[None, :] < HALF_DIM) == (jnp.arange(2)[:, None] == 0)

    cache_k4 = cache_k.reshape(depth, n_phys, page * nh, HEAD_DIM)
    cache_v4 = cache_v.reshape(depth, n_phys, page * nh, HEAD_DIM)
    n_items = (TOP_K * n_all) // MOE_ITEM_ROWS + N_EXPERTS
    tm_proj = _pick(n_all, 1024, 16)

    outs = {k: [] for k in ("kp", "vp", "ks", "vs", "gp", "gs", "cp", "cs", "ds")}
    for l in range(depth):
        lam_init = 0.8 - 0.6 * math.exp(-0.3 * l)
        lams = (lam_q1[l][None], lam_k1[l][None], lam_q2[l][None], lam_k2[l][None])
        gn_a = diff_norm_g[l][None]

        w_main, w_gate_cols = _w_in_layout(w_in, l, gw)
        proj, gproj = _matmul([xb], w_main, 11 * gw, _pick(n_all, 1100, 16), _pick(11 * gw, 512, LANES), "in_proj",
                              w_side=w_gate_cols)
        proj_s = proj[n_prompt:].reshape(bs, tp, 11 * gw)[:, :ts]

        oa_p = _attn_prompt(proj, lams, bias_tiles, gn_a, bp, seq, nh, lam_init)
        q_s = proj_s[..., :gw].reshape(bs, ts, nh, HEAD_DIM) * (HALF_DIM ** -0.5 * LOG2E)
        q2 = (jnp.transpose(q_s, (0, 2, 1, 3))[:, None] * half_mask.astype(F32)[None, :, None, None, :])
        q2 = q2.reshape(bs, 2 * nh * ts, HEAD_DIM).astype(BF16)
        k_s = proj_s[..., gw:2 * gw]
        v_s = proj_s[..., 2 * gw:3 * gw]
        new_rows = lambda a: jnp.pad(a.reshape(bs, ts * nh, HEAD_DIM), ((0, 0), (0, (page - ts) * nh), (0, 0)))
        oa_s = _attn_sample(l, q2, cache_k4, cache_v4, new_rows(k_s), new_rows(v_s), page_table, bias_s, lams, gn_a,
                            lam_init)
        oa_s = jnp.transpose(oa_s.reshape(bs, nh, ts, HEAD_DIM), (0, 2, 1, 3)).reshape(bs, ts, gw)
        oa_s = jnp.pad(oa_s, ((0, 0), (0, tp - ts), (0, 0))).reshape(bs * tp, gw).astype(BF16)
        oa = lax.dynamic_update_slice(oa_p, oa_s, (n_prompt, 0))

        wup = jnp.pad(gla_w_up[l], ((0, LANES - GLA_RANK), (0, 0)))
        wup_hi, wup_lo = _split_bf16(wup)
        gb = gla_b[l][None]
        gn_b = gla_norm_g[l][None]
        s0_s = jnp.transpose(state_gla[l].reshape(bs, nh // 2, 2 * HALF_DIM, HEAD_DIM), (0, 1, 3, 2))
        s0_p = jnp.zeros((bp, nh // 2, HEAD_DIM, LANES), F32)
        ob, sf_p = _gla(proj, gproj, wup_hi, wup_lo, gb, gn_b, s0_p, 0, bp, seq, nh, seq)
        ob, sf_s = _gla(proj, gproj, wup_hi, wup_lo, gb, gn_b, s0_s, n_prompt, bs, tp, nh, ts, prev=(ob,))
        unpair = lambda s: jnp.transpose(s, (0, 1, 3, 2)).reshape(s.shape[0], nh, HALF_DIM, HEAD_DIM)

        bs_t = jnp.transpose(cm_bs[l])
        cw, lg, lb = conv_w[l], cm_ln_g[l][None], cm_ln_b[l][None]
        oc, od, cs_p, _ = _conv_mlp(proj, cw, jnp.zeros((bp, CONV_WIDTH - 1, gw), F32), lg, lb, cm_ws[l], bs_t,
                                    0, bp, seq, nh, seq, False)
        oc, od, cs_s, vd_s = _conv_mlp(proj, cw, state_conv[l], lg, lb, cm_ws[l], bs_t, n_prompt, bs, tp, nh, ts, True,
                                       prev=(oc, od))

        mix = _matmul([oa, ob, oc, od], w_out[l].astype(BF16), d, tm_proj, _pick(d, 1024, LANES), "out_proj")
        wr = jnp.pad(jnp.concatenate([router_group[l], router_expert[l]], axis=1),
                     ((0, 0), (0, LANES - N_GROUPS - N_EXPERTS)))
        wr_hi, wr_lo = _split_bf16(wr)
        x1, ridx, rgate = _ln_router(x_parts, mix, ln1_g[l][None], ln1_b[l][None], wr_hi, wr_lo, alpha)

        dest, item_e, item_n, n_used = _route_plan(ridx, n_items)
        xs = _dispatch(x1, dest, n_items * MOE_ITEM_ROWS)
        ys = _experts(xs, item_e, item_n, n_used, w_gate, w_up, w_down, l, n_items)
        x, xb = _combine(x1, rgate, dest, ys, ln2_g[l][None], ln2_b[l][None], alpha)
        x_parts = (x, None)

        heads = lambda a, n, t: a.reshape(n, t, nh, HEAD_DIM)
        outs["kp"].append(heads(proj[:n_prompt, gw:2 * gw], bp, seq))
        outs["vp"].append(heads(proj[:n_prompt, 2 * gw:3 * gw], bp, seq))
        outs["ks"].append(heads(k_s, bs, ts))
        outs["vs"].append(heads(v_s, bs, ts))
        outs["gp"].append(unpair(sf_p))
        outs["gs"].append(unpair(sf_s))
        outs["cp"].append(cs_p)
        outs["cs"].append(cs_s)
        outs["ds"].append(vd_s.reshape(bs, tp, gw)[:, :ts])

    y_prompt = x[:n_prompt].reshape(bp, seq, d)
    y_sample = x[n_prompt:].reshape(bs, tp, d)[:, :ts]
    st = lambda k: jnp.stack(outs[k])
    return (y_prompt, y_sample, st("kp"), st("vp"), st("ks"), st("vs"), st("gp"), st("gs"),
            st("cp"), st("cs"), st("ds"))
```

```python
import functools
import math

import jax
import jax.numpy as jnp
from jax import lax
from jax.experimental import pallas as pl
from jax.experimental.pallas import tpu as pltpu

F32 = jnp.float32
BF16 = jnp.bfloat16

HEAD_DIM = 128
HALF_DIM = HEAD_DIM // 2
GLA_RANK = 16
GLA_TAU = 16.0
GLA_CHUNK = 64
CONV_WIDTH = 3
CHUNK_D = 128
REL_BUCKETS = 32
REL_MAX_DIST = 128
N_GROUPS = 4
EXPERTS_PER_GROUP = 8
N_EXPERTS = N_GROUPS * EXPERTS_PER_GROUP
TOP_K = 2
NORM_EPS = 1e-5
MASK_VALUE = -1e30
LOG2E = 1.4426950408889634

LANES = 128
SAMPLE_ROWS = 16
ATTN_TILE = 256
ATTN_HEADS = 4
SAMPLE_PAGES_PER_STEP = 8
MOE_ITEM_ROWS = 768
MOE_SUB_ROWS = 128
MOE_HIDDEN_TILE = 256
MOE_DOWN_STEPS = 4
VMEM_LIMIT = 52 * 1024 * 1024


def _pick(n, target, mult):
    if n <= target:
        return n
    best = None
    for d in range(mult, target + 1, mult):
        if n % d == 0:
            best = d
    assert best is not None, (n, target, mult)
    return best


def _params(*sem):
    return pltpu.CompilerParams(dimension_semantics=sem, vmem_limit_bytes=VMEM_LIMIT)


def _dot(a, b):
    return jnp.dot(a, b, preferred_element_type=F32)


def _dot_nt(a, b):
    return lax.dot_general(a, b, (((1,), (1,)), ((), ())), preferred_element_type=F32)


def _dot_tn(a, b):
    return lax.dot_general(a, b, (((0,), (0,)), ((), ())), preferred_element_type=F32)


def _split_bf16(x):
    hi = x.astype(BF16)
    lo = (x - hi.astype(F32)).astype(BF16)
    return hi, lo


def _sigmoid(x):
    return 1.0 / (1.0 + jnp.exp(-x))


def _mm_kernel(*refs, n_in, n_side):
    w_side = refs[2 * n_in:2 * n_in + n_side]
    o_ref = refs[2 * n_in + n_side]
    o_side = refs[2 * n_in + n_side + 1:]
    acc = None
    for x_ref, w_ref in zip(refs[:n_in], refs[n_in:2 * n_in]):
        d = _dot(x_ref[...], w_ref[...])
        acc = d if acc is None else acc + d
    o_ref[...] = acc.astype(o_ref.dtype)

    if n_side:
        @pl.when(pl.program_id(1) == 0)
        def _():
            o_side[0][...] = _dot(refs[0][...], w_side[0][...])


def _matmul(xs, w, n_out, tm, tn, name, w_side=None):
    n_in = len(xs)
    n_side = 0 if w_side is None else 1
    m, kg = xs[0].shape
    grid = (m // tm, n_out // tn)
    in_specs = [pl.BlockSpec((tm, kg), lambda i, j: (i, 0)) for _ in xs]
    in_specs += [pl.BlockSpec((kg, tn), functools.partial(lambda i, j, g: (g, j), g=g)) for g in range(n_in)]
    out_specs = [pl.BlockSpec((tm, tn), lambda i, j: (i, j))]
    out_shape = [jax.ShapeDtypeStruct((m, n_out), F32)]
    side_args = ()
    if n_side:
        in_specs.append(pl.BlockSpec((kg, LANES), lambda i, j: (0, 0)))
        out_specs.append(pl.BlockSpec((tm, LANES), lambda i, j: (i, 0)))
        out_shape.append(jax.ShapeDtypeStruct((m, LANES), F32))
        side_args = (w_side,)
    out = pl.pallas_call(
        functools.partial(_mm_kernel, n_in=n_in, n_side=n_side),
        grid=grid,
        in_specs=in_specs,
        out_specs=out_specs,
        out_shape=out_shape,
        compiler_params=_params("parallel", "arbitrary"),
        name=name,
    )(*xs, *([w] * n_in), *side_args)
    return out if n_side else out[0]


def _rel_bias(dist, rel_table):
    n = jnp.maximum(dist, 0)
    max_exact = REL_BUCKETS // 2
    nf = jnp.maximum(n, 1).astype(F32)
    large = max_exact + (jnp.log(nf / max_exact) / math.log(REL_MAX_DIST / max_exact)
                         * (REL_BUCKETS - max_exact)).astype(jnp.int32)
    bucket = jnp.where(n < max_exact, n, jnp.minimum(large, REL_BUCKETS - 1))
    onehot = (bucket[..., None] == jnp.arange(REL_BUCKETS, dtype=bucket.dtype)).astype(F32)
    return jnp.einsum("...k,kh->...h", onehot, rel_table.astype(F32), precision=lax.Precision.HIGHEST)


def _prompt_bias_tiles(rel_table, t):
    r = jnp.arange(t)[:, None]
    c = jnp.arange(t)[None, :]
    tiles = []
    for delta in range(3):
        dist = delta * t + r - c
        b = _rel_bias(dist, rel_table) * LOG2E
        tiles.append(jnp.where((dist >= 0)[..., None], b, MASK_VALUE))
    return jnp.transpose(jnp.stack(tiles), (3, 0, 1, 2))


def _lam_value(lq1, lk1, lq2, lk2, lam_init):
    return (jnp.exp(jnp.sum(lq1[...] * lk1[...], axis=1, keepdims=True))
            - jnp.exp(jnp.sum(lq2[...] * lk2[...], axis=1, keepdims=True)) + lam_init)


def _rms_head(o, gain, post):
    return o * lax.rsqrt(jnp.mean(o * o, axis=1, keepdims=True) + NORM_EPS) * gain * post


def _attn_prompt_kernel(lq1, lk1, lq2, lk2, q_ref, k_ref, v_ref, bias_ref, gn_ref, o_ref, *, t, lam_init):
    qi = pl.program_id(2)
    n_grp = bias_ref.shape[0]
    lam = _lam_value(lq1, lk1, lq2, lk2, lam_init)
    lane = lax.broadcasted_iota(jnp.int32, (t, HEAD_DIM), 1)
    heads = []
    for h in range(n_grp):
        q = q_ref[:, h * HEAD_DIM:(h + 1) * HEAD_DIM] * (HALF_DIM ** -0.5 * LOG2E)
        heads.append(jnp.concatenate([jnp.where(lane < HALF_DIM, q, 0.0), jnp.where(lane >= HALF_DIM, q, 0.0)],
                                     axis=0).astype(BF16))

    def body(j, carry):
        start = pl.multiple_of(j * t, t)
        tile = jnp.minimum(qi - j, 2)
        out = []
        for h in range(n_grp):
            m, s, acc = carry[h]
            cols = slice(h * HEAD_DIM, (h + 1) * HEAD_DIM)
            kt = k_ref[pl.ds(start, t), cols].astype(BF16)
            vt = v_ref[pl.ds(start, t), cols].astype(BF16)
            bt = bias_ref[h, tile]
            l = _dot_nt(heads[h], kt) + jnp.concatenate([bt, bt], axis=0)
            m_new = jnp.maximum(m, jnp.max(l, axis=1, keepdims=True))
            a = jnp.exp2(m - m_new)
            p = jnp.exp2(l - m_new)
            out.append((m_new, a * s + jnp.sum(p, axis=1, keepdims=True), a * acc + _dot(p.astype(BF16), vt)))
        return tuple(out)

    init = (jnp.full((2 * t, 1), MASK_VALUE, F32), jnp.zeros((2 * t, 1), F32), jnp.zeros((2 * t, HEAD_DIM), F32))
    res = lax.fori_loop(0, qi + 1, body, (init,) * n_grp)
    for h in range(n_grp):
        _, s, acc = res[h]
        o = acc / s
        o = o[:t] - lam * o[t:]
        o_ref[:, h * HEAD_DIM:(h + 1) * HEAD_DIM] = _rms_head(o, gn_ref[...], 1.0 - lam_init).astype(o_ref.dtype)


def _attn_prompt(proj, lams, bias_tiles, gn, n_seq, seq, n_heads, lam_init):
    t = ATTN_TILE
    n_grp = math.gcd(ATTN_HEADS, n_heads)
    assert seq % t == 0 and t >= REL_MAX_DIST
    nq = seq // t
    pair = n_grp * HEAD_DIM
    groups = n_heads // n_grp
    lam_spec = pl.BlockSpec((1, HALF_DIM), lambda b, h, i: (0, 0))
    return pl.pallas_call(
        functools.partial(_attn_prompt_kernel, t=t, lam_init=lam_init),
        grid=(n_seq, groups, nq),
        in_specs=[lam_spec] * 4 + [
            pl.BlockSpec((t, pair), lambda b, h, i: (b * nq + i, h)),
            pl.BlockSpec((seq, pair), lambda b, h, i: (b, groups + h)),
            pl.BlockSpec((seq, pair), lambda b, h, i: (b, 2 * groups + h)),
            pl.BlockSpec((n_grp, 3, t, t), lambda b, h, i: (h, 0, 0, 0)),
            pl.BlockSpec((1, HEAD_DIM), lambda b, h, i: (0, 0)),
        ],
        out_specs=pl.BlockSpec((t, pair), lambda b, h, i: (b * nq + i, h)),
        out_shape=jax.ShapeDtypeStruct((proj.shape[0], n_heads * HEAD_DIM), BF16),
        compiler_params=_params("parallel", "parallel", "arbitrary"),
        name="attn_prompt",
    )(*lams, proj, proj, proj, bias_tiles, gn)


def _attn_sample_kernel(pt_ref, lq1, lk1, lq2, lk2, q_ref, *refs, n_pp, lam_init):
    kc_refs, vc_refs = refs[:n_pp], refs[n_pp:2 * n_pp]
    kn_ref, vn_ref, bias_ref, gn_ref, o_ref, m_ref, s_ref, acc_ref = refs[2 * n_pp:]
    p = pl.program_id(1)
    n_steps = pl.num_programs(1)

    @pl.when(p == 0)
    def _():
        m_ref[...] = jnp.full(m_ref.shape, MASK_VALUE, F32)
        s_ref[...] = jnp.zeros(s_ref.shape, F32)
        acc_ref[...] = jnp.zeros(acc_ref.shape, F32)

    q = q_ref[0]

    def step(ks, vs, biases):
        ls = [_dot_nt(q, k.astype(BF16)) + b for k, b in zip(ks, biases)]
        m = m_ref[...]
        m_new = m
        for l in ls:
            m_new = jnp.maximum(m_new, jnp.max(l, axis=1, keepdims=True))
        a = jnp.exp2(m - m_new)
        s = a * s_ref[...]
        acc = a * acc_ref[...]
        for l, v in zip(ls, vs):
            pr = jnp.exp2(l - m_new)
            s = s + jnp.sum(pr, axis=1, keepdims=True)
            acc = acc + _dot(pr.astype(BF16), v.astype(BF16))
        s_ref[...] = s
        acc_ref[...] = acc
        m_ref[...] = m_new

    last = p == n_steps - 1
    tables = [0] * (n_pp - 1) + [jnp.where(last, 1, 0)]
    step([r[0, 0] for r in kc_refs], [r[0, 0] for r in vc_refs], [bias_ref[t] for t in tables])

    @pl.when(last)
    def _():
        step([kn_ref[0]], [vn_ref[0]], [bias_ref[2]])
        lam = _lam_value(lq1, lk1, lq2, lk2, lam_init)
        o = acc_ref[...] / s_ref[...]
        half = o.shape[0] // 2
        o_ref[0] = _rms_head(o[:half] - lam * o[half:], gn_ref[...], 1.0 - lam_init)


def _attn_sample(layer, q2, cache_k, cache_v, k_new, v_new, page_table, bias, lams, gn, lam_init):
    n_seq, n_pages = page_table.shape
    prow = cache_k.shape[2]
    rows = q2.shape[1]
    n_pp = SAMPLE_PAGES_PER_STEP if n_pages % SAMPLE_PAGES_PER_STEP == 0 else 1
    lam_spec = pl.BlockSpec((1, HALF_DIM), lambda b, p, pt: (0, 0))
    cache_specs = [pl.BlockSpec((1, 1, prow, HEAD_DIM),
                                functools.partial(lambda b, p, pt, i: (layer, pt[b, p * n_pp + i], 0, 0), i=i))
                   for i in range(n_pp)]
    new_spec = pl.BlockSpec((1, prow, HEAD_DIM), lambda b, p, pt: (b, 0, 0))
    grid_spec = pltpu.PrefetchScalarGridSpec(
        num_scalar_prefetch=1,
        grid=(n_seq, n_pages // n_pp),
        in_specs=[lam_spec] * 4 + [pl.BlockSpec((1, rows, HEAD_DIM), lambda b, p, pt: (b, 0, 0))]
        + cache_specs + cache_specs + [
            new_spec, new_spec,
            pl.BlockSpec((3, rows, prow), lambda b, p, pt: (0, 0, 0)),
            pl.BlockSpec((1, HEAD_DIM), lambda b, p, pt: (0, 0)),
        ],
        out_specs=pl.BlockSpec((1, rows // 2, HEAD_DIM), lambda b, p, pt: (b, 0, 0)),
        scratch_shapes=[pltpu.VMEM((rows, 1), F32), pltpu.VMEM((rows, 1), F32), pltpu.VMEM((rows, HEAD_DIM), F32)],
    )
    return pl.pallas_call(
        functools.partial(_attn_sample_kernel, n_pp=n_pp, lam_init=lam_init),
        grid_spec=grid_spec,
        out_shape=jax.ShapeDtypeStruct((n_seq, rows // 2, HEAD_DIM), F32),
        compiler_params=_params("parallel", "arbitrary"),
        name="attn_sample",
    )(page_table, *lams, q2, *([cache_k] * n_pp), *([cache_v] * n_pp), k_new, v_new, bias, gn)


def _gla_kernel(q_ref, k_ref, v_ref, r_ref, gp_ref, wh_ref, wl_ref, gb_ref, gn_ref, s0_ref, *rest,
                chunk, n_chunks, n_heads, valid):
    o_ref, sfin_ref, st_ref = rest[-3:]
    t = pl.program_id(1)

    @pl.when(t == 0)
    def _():
        st_ref[...] = s0_ref[0]

    half = chunk // 2
    quarter = chunk // 4
    row = lax.broadcasted_iota(jnp.int32, (chunk, chunk), 0)
    col = lax.broadcasted_iota(jnp.int32, (chunk, chunk), 1)
    tril = row >= col
    same_half = (row < half) == (col < half)
    tri_bf = jnp.where(tril, 1.0, 0.0).astype(BF16)
    lane = lax.broadcasted_iota(jnp.int32, (chunk, LANES), 1)
    lane_sq = lax.broadcasted_iota(jnp.int32, (HEAD_DIM, LANES), 1)

    for c in range(n_chunks):
        rs = slice(c * chunk, (c + 1) * chunk)
        gh, gl = _split_bf16(gp_ref[rs, :])
        z = _dot(gh, wh_ref[...]) + _dot(gl, wh_ref[...]) + _dot(gh, wl_ref[...]) + gb_ref[...]
        g = -(jnp.maximum(-z, 0.0) + jnp.log(1.0 + jnp.exp(-jnp.abs(z)))) * (1.0 / GLA_TAU)
        k = k_ref[rs, :]
        if valid < chunk:
            ok = lax.broadcasted_iota(jnp.int32, g.shape, 0) < valid
            g = jnp.where(ok, g, 0.0)
            k = jnp.where(ok, k, 0.0)
        gh, gl = _split_bf16(g)
        b = _dot(tri_bf, gh) + _dot(tri_bf, gl)
        b_last = b[chunk - 1:chunk, :]
        b_half = b[half - 1:half, :]
        in_lo = lax.broadcasted_iota(jnp.int32, b.shape, 0) < half
        b_near = jnp.where(in_lo, b[quarter - 1:quarter, :], b[half + quarter - 1:half + quarter, :])
        q = q_ref[rs, :] * (HALF_DIM ** -0.5)
        q_state = q * jnp.exp(b)
        q_near = q * jnp.exp(b - b_near)
        k_near = (k * jnp.exp(b_near - b)).astype(BF16)
        q_far = q * jnp.exp(jnp.minimum(b - b_half, 0.0))
        k_far = (k * jnp.exp(jnp.minimum(b_half - b, 0.0))).astype(BF16)
        k_end = (k * jnp.exp(b_last - b)).astype(BF16)
        decay = jnp.exp(b_last)

        for hp in range(n_heads // 2):
            ls = slice(hp * LANES, (hp + 1) * LANES)
            st = st_ref[hp]
            st_bf = st.astype(BF16)
            upd = []
            for par in range(2):
                h = 2 * hp + par
                mine = (lane < HALF_DIM) if par == 0 else (lane >= HALF_DIM)
                hs = slice(h * HEAD_DIM, (h + 1) * HEAD_DIM)
                v_h = v_ref[rs, hs].astype(BF16)
                att = jnp.where(same_half,
                                _dot_nt(jnp.where(mine, q_near[:, ls], 0.0).astype(BF16), k_near[:, ls]),
                                _dot_nt(jnp.where(mine, q_far[:, ls], 0.0).astype(BF16), k_far[:, ls]))
                att = jnp.where(tril, att, 0.0).astype(BF16)
                o = _dot(att, v_h) + _dot_nt(jnp.where(mine, q_state[:, ls], 0.0).astype(BF16), st_bf)
                r_h = r_ref[rs, hs]
                o_ref[rs, hs] = _rms_head(o, gn_ref[...], r_h * _sigmoid(r_h)).astype(o_ref.dtype)
                upd.append(_dot_tn(v_h, k_end[:, ls]))
            st_ref[hp] = st * decay[:, ls] + jnp.where(lane_sq < HALF_DIM, upd[0], upd[1])

    @pl.when(t == pl.num_programs(1) - 1)
    def _():
        sfin_ref[0] = st_ref[...]


def _into(prev):
    specs = [pl.BlockSpec(memory_space=pl.ANY) for _ in prev]
    return specs, tuple(prev)


def _gla(proj, gproj, wup_hi, wup_lo, gla_b, gn, s0, row0, n_seq, seq, n_heads, valid, prev=()):
    gw = n_heads * HEAD_DIM
    hk = n_heads * HALF_DIM
    chunk = min(GLA_CHUNK, seq)
    tt = _pick(seq, 256, chunk)
    nt = seq // tt
    r0 = row0 // tt
    assert row0 % tt == 0 and chunk % 4 == 0
    qcol = 3 * gw // hk
    row_spec = lambda width, colblk: pl.BlockSpec((tt, width), lambda b, t: (r0 + b * nt + t, colblk))
    const = lambda shape: pl.BlockSpec(shape, lambda b, t: tuple(0 for _ in shape))
    st_spec = pl.BlockSpec((1, n_heads // 2, HEAD_DIM, LANES), lambda b, t: (b, 0, 0, 0))
    prev_specs, prev_args = _into(prev)
    n_in = 10
    return pl.pallas_call(
        functools.partial(_gla_kernel, chunk=chunk, n_chunks=tt // chunk, n_heads=n_heads, valid=valid),
        grid=(n_seq, nt),
        in_specs=[row_spec(hk, qcol), row_spec(hk, qcol + 1), row_spec(gw, 4), row_spec(gw, 5),
                  pl.BlockSpec((tt, LANES), lambda b, t: (r0 + b * nt + t, 0)),
                  const((LANES, hk)), const((LANES, hk)), const((1, hk)), const((1, HEAD_DIM)), st_spec] + prev_specs,
        out_specs=[pl.BlockSpec((tt, gw), lambda b, t: (r0 + b * nt + t, 0)), st_spec],
        out_shape=[jax.ShapeDtypeStruct((proj.shape[0], gw), BF16),
                   jax.ShapeDtypeStruct((n_seq, n_heads // 2, HEAD_DIM, LANES), F32)],
        input_output_aliases={n_in + k: k for k in range(len(prev))},
        scratch_shapes=[pltpu.VMEM((n_heads // 2, HEAD_DIM, LANES), F32)],
        compiler_params=_params("parallel", "arbitrary"),
        name="gla",
    )(proj, proj, proj, proj, gproj, wup_hi, wup_lo, gla_b, gn, s0, *prev_args)


def _conv_mlp_kernel(cb_ref, cc_ref, ch_ref, du_ref, dv_ref, cw_ref, buf_ref, lg_ref, lb_ref, ws_ref, bs_ref, *rest,
                     tt, chunk, n_heads, valid):
    oc_ref, od_ref, cs_ref, vd_ref, carry_ref = rest[-5:]
    t = pl.program_id(1)

    @pl.when(t == 0)
    def _():
        carry_ref[...] = buf_ref[0]

    z = cc_ref[...] * ch_ref[...]
    row = lax.broadcasted_iota(jnp.int32, z.shape, 0)
    c0 = carry_ref[0:1, :]
    c1 = carry_ref[1:2, :]
    z1 = jnp.where(row == 0, c1, pltpu.roll(z, 1, 0))
    z2 = jnp.where(row == 0, c0, jnp.where(row == 1, c1, pltpu.roll(z, 2, 0)))
    w = cw_ref[...]
    oc_ref[...] = (cb_ref[...] * (z2 * w[0:1, :] + z1 * w[1:2, :] + z * w[2:3, :])).astype(oc_ref.dtype)
    if valid >= 2:
        carry_ref[...] = z[valid - 2:valid, :]
    else:
        carry_ref[...] = jnp.concatenate([c1, z[0:1, :]], axis=0)

    @pl.when(t == pl.num_programs(1) - 1)
    def _():
        cs_ref[0] = carry_ref[...]

    x = dv_ref[...]
    mu = jnp.mean(x, axis=1, keepdims=True)
    xc = x - mu
    var = jnp.mean(xc * xc, axis=1, keepdims=True)
    vd = xc * lax.rsqrt(var + NORM_EPS) * lg_ref[...] + lb_ref[...]
    vd_ref[...] = vd
    vd_bf = vd.astype(BF16)
    r2 = lax.broadcasted_iota(jnp.int32, (chunk, chunk), 0)
    c2 = lax.broadcasted_iota(jnp.int32, (chunk, chunk), 1)
    for g in range(n_heads):
        wc = jnp.where(r2 >= c2, ws_ref[g, :chunk, :chunk], 0.0).astype(BF16)
        bias = bs_ref[:chunk, g:g + 1]
        cols = slice(g * HEAD_DIM, (g + 1) * HEAD_DIM)
        for c in range(tt // chunk):
            rs = slice(c * chunk, (c + 1) * chunk)
            zc = _dot(wc, vd_bf[rs, cols]) + bias
            od_ref[rs, cols] = (du_ref[rs, cols] * zc).astype(od_ref.dtype)


def _conv_mlp(proj, conv_w, buf, ln_g, ln_b, ws, bs_t, row0, n_seq, seq, n_heads, valid, keep_vd, prev=()):
    gw = n_heads * HEAD_DIM
    chunk = min(CHUNK_D, seq)
    tt = _pick(seq, 256, chunk)
    nt = seq // tt
    r0 = row0 // tt
    assert row0 % tt == 0 and (valid == seq or nt == 1)
    row_spec = lambda colblk: pl.BlockSpec((tt, gw), lambda b, t: (r0 + b * nt + t, colblk))
    const = lambda shape: pl.BlockSpec(shape, lambda b, t: tuple(0 for _ in shape))
    out_rows = pl.BlockSpec((tt, gw), lambda b, t: (r0 + b * nt + t, 0))
    vd_spec = pl.BlockSpec((tt, gw), (lambda b, t: (b * nt + t, 0)) if keep_vd else (lambda b, t: (0, 0)))
    state_spec = pl.BlockSpec((1, CONV_WIDTH - 1, gw), lambda b, t: (b, 0, 0))
    prev_specs, prev_args = _into(prev)
    n_in = 11
    return pl.pallas_call(
        functools.partial(_conv_mlp_kernel, tt=tt, chunk=chunk, n_heads=n_heads, valid=min(valid, tt)),
        grid=(n_seq, nt),
        in_specs=[row_spec(6), row_spec(7), row_spec(8), row_spec(9), row_spec(10),
                  const((CONV_WIDTH, gw)), state_spec, const((1, gw)), const((1, gw)),
                  const((n_heads, CHUNK_D, CHUNK_D)), const((CHUNK_D, n_heads))] + prev_specs,
        out_specs=[out_rows, out_rows, state_spec, vd_spec],
        out_shape=[jax.ShapeDtypeStruct((proj.shape[0], gw), BF16), jax.ShapeDtypeStruct((proj.shape[0], gw), BF16),
                   jax.ShapeDtypeStruct((n_seq, CONV_WIDTH - 1, gw), F32),
                   jax.ShapeDtypeStruct((n_seq * seq if keep_vd else tt, gw), F32)],
        input_output_aliases={n_in + k: k for k in range(len(prev))},
        scratch_shapes=[pltpu.VMEM((CONV_WIDTH - 1, gw), F32)],
        compiler_params=_params("parallel", "arbitrary"),
        name="conv_mlp",
    )(proj, proj, proj, proj, proj, conv_w, buf, ln_g, ln_b, ws, bs_t, *prev_args)


def _layer_norm_rows(x, g, b):
    mu = jnp.mean(x, axis=1, keepdims=True)
    xc = x - mu
    var = jnp.mean(xc * xc, axis=1, keepdims=True)
    return xc * lax.rsqrt(var + NORM_EPS) * g + b


def _ln_router_kernel(xa_ref, xb_ref, mix_ref, g_ref, b_ref, wh_ref, wl_ref, y_ref, idx_ref, gate_ref, *, alpha, split):
    x = jnp.where(pl.program_id(0) < split, xa_ref[...], xb_ref[...])
    y = _layer_norm_rows(alpha * x + mix_ref[...], g_ref[...], b_ref[...])
    y_ref[...] = y
    yh, yl = _split_bf16(y)
    lg = _dot(yh, wh_ref[...]) + _dot(yl, wh_ref[...]) + _dot(yh, wl_ref[...])
    lane = lax.broadcasted_iota(jnp.int32, lg.shape, 1)
    neg = -jnp.inf
    glog = jnp.where(lane < N_GROUPS, lg, neg)
    gmax = jnp.max(glog, axis=1, keepdims=True)
    g_star = jnp.min(jnp.where(glog == gmax, lane, LANES), axis=1, keepdims=True)
    p_top = 1.0 / jnp.sum(jnp.exp(glog - gmax), axis=1, keepdims=True)
    lo = N_GROUPS + EXPERTS_PER_GROUP * g_star
    w1 = jnp.where((lane >= lo) & (lane < lo + EXPERTS_PER_GROUP), lg, neg)
    v1 = jnp.max(w1, axis=1, keepdims=True)
    i1 = jnp.min(jnp.where(w1 == v1, lane, LANES), axis=1, keepdims=True)
    w2 = jnp.where(lane == i1, neg, w1)
    v2 = jnp.max(w2, axis=1, keepdims=True)
    i2 = jnp.min(jnp.where(w2 == v2, lane, LANES), axis=1, keepdims=True)
    e21 = jnp.exp(v2 - v1)
    gate1 = p_top / (1.0 + e21)
    gate2 = p_top * e21 / (1.0 + e21)
    idx_ref[...] = jnp.where(lane == 0, i1 - N_GROUPS, jnp.where(lane == 1, i2 - N_GROUPS, 0))
    gate_ref[...] = jnp.where(lane == 0, gate1, jnp.where(lane == 1, gate2, 0.0))


def _ln_router(x_parts, mix, g, b, wr_hi, wr_lo, alpha):
    m, d = mix.shape
    tm = _pick(math.gcd(m, x_parts[0].shape[0]), 256, 8)
    xa, xb = x_parts if x_parts[1] is not None else (x_parts[0], x_parts[0])
    split = xa.shape[0] // tm if x_parts[1] is not None else m // tm
    rows = pl.BlockSpec((tm, d), lambda i: (i, 0))
    const = lambda shape: pl.BlockSpec(shape, lambda i: (0, 0))
    small = pl.BlockSpec((tm, LANES), lambda i: (i, 0))
    return pl.pallas_call(
        functools.partial(_ln_router_kernel, alpha=alpha, split=split),
        grid=(m // tm,),
        in_specs=[pl.BlockSpec((tm, d), lambda i: (jnp.minimum(i, split - 1), 0)),
                  pl.BlockSpec((tm, d), lambda i: (jnp.maximum(i - split, 0), 0)),
                  rows, const((1, d)), const((1, d)), const((d, LANES)), const((d, LANES))],
        out_specs=[rows, small, small],
        out_shape=[jax.ShapeDtypeStruct((m, d), F32),
                   jax.ShapeDtypeStruct((m, LANES), jnp.int32), jax.ShapeDtypeStruct((m, LANES), F32)],
        compiler_params=_params("parallel"),
        name="ln_router",
    )(xa, xb, mix, g, b, wr_hi, wr_lo)


SLAB_TILE = 8


def _token_copy(src, src_row, dst, dst_row, sem):
    rows = lambda r: pl.ds(pl.multiple_of(r * SLAB_TILE, SLAB_TILE), SLAB_TILE)
    return pltpu.make_async_copy(src.at[:, rows(src_row), :], dst.at[:, rows(dst_row), :], sem)


def _dispatch_kernel(dest_ref, x_ref, xs_hbm, stage, sem, *, tm):
    for p in range(stage.shape[0]):
        for c in range(SLAB_TILE):
            col = (p * SLAB_TILE + c) * LANES
            stage[p, pl.ds(c, tm, stride=SLAB_TILE), :] = x_ref[:, col:col + LANES]

    def copy(i, k):
        return _token_copy(stage, i, xs_hbm, dest_ref[0, 0, 2 * i + k], sem)

    def start(i, _):
        copy(i, 0).start()
        copy(i, 1).start()
        return 0

    def wait(i, _):
        copy(i, 0).wait()
        copy(i, 1).wait()
        return 0

    lax.fori_loop(0, tm, start, 0)
    lax.fori_loop(0, tm, wait, 0)


def _dispatch(x1, dest, n_rows):
    m, d = x1.shape
    planes = d // (SLAB_TILE * LANES)
    tm = _pick(m, 256, 8)
    dest3 = dest.reshape(m // tm, 1, TOP_K * tm)
    return pl.pallas_call(
        functools.partial(_dispatch_kernel, tm=tm),
        grid=(m // tm,),
        in_specs=[pl.BlockSpec((1, 1, TOP_K * tm), lambda i: (i, 0, 0), memory_space=pltpu.SMEM),
                  pl.BlockSpec((tm, d), lambda i: (i, 0))],
        out_specs=pl.BlockSpec(memory_space=pl.ANY),
        out_shape=jax.ShapeDtypeStruct((planes, n_rows * SLAB_TILE, LANES), F32),
        scratch_shapes=[pltpu.VMEM((planes, tm * SLAB_TILE, LANES), F32), pltpu.SemaphoreType.DMA(())],
        compiler_params=pltpu.CompilerParams(dimension_semantics=("arbitrary",), has_side_effects=True,
                                             vmem_limit_bytes=VMEM_LIMIT),
        name="moe_dispatch",
    )(dest3, x1)


def _expert_kernel(ie_ref, in_ref, nu_ref, x_ref, wg_ref, wu_ref, wd_ref, y_ref, xb, h_buf, *, planes, hs, sub, tiles):
    i = pl.program_id(0)
    s = pl.program_id(1)
    n_sub = (in_ref[i] + sub - 1) // sub
    live = i < nu_ref[0]
    per_blk = SLAB_TILE // tiles
    up0 = planes
    down0 = planes + hs
    kp = SLAB_TILE * LANES
    ht = h_buf.shape[2]

    def for_rows(fn):
        for k in range(1, xb.shape[1] // sub + 1):
            pl.when(n_sub == k)(functools.partial(fn, k * sub))

    @pl.when(live & (s < up0))
    def _():
        def convert(m):
            xb[s, :m, :] = jnp.concatenate([x_ref[pl.ds(c, m, stride=SLAB_TILE), :]
                                            for c in range(SLAB_TILE)], axis=1).astype(BF16)
        for_rows(convert)

    @pl.when(live & (s >= up0) & (s < down0))
    def _():
        wg = wg_ref[0].astype(BF16)
        wu = wu_ref[0].astype(BF16)

        def up(m):
            g = sum(_dot(xb[p, :m, :], wg[p * kp:(p + 1) * kp, :]) for p in range(planes))
            u = sum(_dot(xb[p, :m, :], wu[p * kp:(p + 1) * kp, :]) for p in range(planes))
            h_buf[s - up0, :m, :] = (g * _sigmoid(g) * u).astype(BF16)
        for_rows(up)

    @pl.when(live & (s >= down0))
    def _():
        wd = wd_ref[0].astype(BF16)
        c0 = ((s - down0) % per_blk) * tiles

        def down(m):
            y = sum(_dot(h_buf[j, :m, :], wd[j * ht:(j + 1) * ht, :]) for j in range(hs))
            for c in range(tiles):
                y_ref[pl.ds(c0 + c, m, stride=SLAB_TILE), :] = y[:, c * LANES:(c + 1) * LANES]
        for_rows(down)


def _experts(xs, item_e, item_n, n_used, w_gate, w_up, w_down, layer, n_items):
    d, hidden = w_gate.shape[-2:]
    n_exp = w_gate.shape[1]
    planes = xs.shape[0]
    ht = min(MOE_HIDDEN_TILE, hidden)
    hs = hidden // ht
    ns = MOE_DOWN_STEPS
    nt = d // ns
    tiles = nt // LANES
    assert SLAB_TILE % tiles == 0 and hidden % ht == 0
    per_blk = SLAB_TILE // tiles
    n_steps = planes + hs + ns
    wg3 = w_gate.reshape(-1, d, hidden)
    wu3 = w_up.reshape(-1, d, hidden)
    wd3 = w_down.reshape(-1, hidden, d)
    e0 = layer * n_exp
    rows = MOE_ITEM_ROWS

    def item(i, nu):
        return jnp.minimum(i, nu[0] - 1)

    def step(i, s, nu):
        return jnp.where(i < nu[0], s, n_steps - 1)

    def expert(i, ie, nu):
        return e0 + ie[item(i, nu)]

    def x_plane(i, s, ie, inn, nu):
        return jnp.minimum(step(i, s, nu), planes - 1), item(i, nu), 0

    def up_slice(i, s, ie, inn, nu):
        return expert(i, ie, nu), 0, jnp.clip(step(i, s, nu) - planes, 0, hs - 1)

    def down_slice(i, s, ie, inn, nu):
        return expert(i, ie, nu), 0, jnp.maximum(step(i, s, nu) - planes - hs, 0)

    def y_plane(i, s, ie, inn, nu):
        return jnp.maximum(step(i, s, nu) - planes - hs, 0) // per_blk, item(i, nu), 0

    grid_spec = pltpu.PrefetchScalarGridSpec(
        num_scalar_prefetch=3,
        grid=(n_items, n_steps),
        in_specs=[
            pl.BlockSpec((None, rows * SLAB_TILE, LANES), x_plane),
            pl.BlockSpec((1, d, ht), up_slice),
            pl.BlockSpec((1, d, ht), up_slice),
            pl.BlockSpec((1, hidden, nt), down_slice),
        ],
        out_specs=pl.BlockSpec((None, rows * SLAB_TILE, LANES), y_plane),
        scratch_shapes=[pltpu.VMEM((planes, rows, SLAB_TILE * LANES), BF16), pltpu.VMEM((hs, rows, ht), BF16)],
    )
    return pl.pallas_call(
        functools.partial(_expert_kernel, planes=planes, hs=hs, sub=MOE_SUB_ROWS, tiles=tiles),
        grid_spec=grid_spec,
        out_shape=jax.ShapeDtypeStruct(xs.shape, F32),
        compiler_params=_params("arbitrary", "arbitrary"),
        name="moe_experts",
    )(item_e, item_n, n_used, xs, wg3, wu3, wd3)


def _combine_kernel(dest_ref, next_ref, x_ref, gate_ref, g_ref, b_ref, ys_hbm, y_ref, yb_ref, stage0, stage1, sems,
                    *, tm, alpha):
    i = pl.program_id(0)
    slot = i % 2
    stages = (stage0, stage1)

    def copy(dests, t, k, sl):
        return _token_copy(ys_hbm, dests[0, 0, 2 * t + k], stages[k].at[sl], t, sems.at[sl])

    def start_tile(dests, sl):
        def start(t, _):
            copy(dests, t, 0, sl).start()
            copy(dests, t, 1, sl).start()
            return 0
        lax.fori_loop(0, tm, start, 0)

    @pl.when(i == 0)
    def _():
        start_tile(dest_ref, slot)

    @pl.when(i + 1 < pl.num_programs(0))
    def _():
        start_tile(next_ref, 1 - slot)

    def wait(t, _):
        copy(dest_ref, t, 0, slot).wait()
        copy(dest_ref, t, 1, slot).wait()
        return 0

    lax.fori_loop(0, tm, wait, 0)
    gates = gate_ref[...]
    g0 = gates[:, 0:1]
    g1 = gates[:, 1:2]
    tile = lambda stage, p, c: stage[slot, p, pl.ds(c, tm, stride=SLAB_TILE), :]
    ffn = jnp.concatenate([g0 * tile(stage0, p, c) + g1 * tile(stage1, p, c)
                           for p in range(stage0.shape[1]) for c in range(SLAB_TILE)], axis=1)
    y = _layer_norm_rows(alpha * x_ref[...] + ffn, g_ref[...], b_ref[...])
    y_ref[...] = y
    yb_ref[...] = y.astype(BF16)


def _combine(x1, gates, dest, ys, g, b, alpha):
    m, d = x1.shape
    tm = _pick(m, 256, 8)
    n_tiles = m // tm
    stage = pltpu.VMEM((2, ys.shape[0], tm * SLAB_TILE, LANES), F32)
    dest3 = dest.reshape(n_tiles, 1, TOP_K * tm)
    dest_spec = lambda off: pl.BlockSpec((1, 1, TOP_K * tm), lambda i: (jnp.minimum(i + off, n_tiles - 1), 0, 0),
                                         memory_space=pltpu.SMEM)
    rows = pl.BlockSpec((tm, d), lambda i: (i, 0))
    const = pl.BlockSpec((1, d), lambda i: (0, 0))
    return pl.pallas_call(
        functools.partial(_combine_kernel, tm=tm, alpha=alpha),
        grid=(n_tiles,),
        in_specs=[dest_spec(0), dest_spec(1), rows, pl.BlockSpec((tm, LANES), lambda i: (i, 0)), const, const,
                  pl.BlockSpec(memory_space=pl.ANY)],
        out_specs=[rows, rows],
        out_shape=[jax.ShapeDtypeStruct((m, d), F32), jax.ShapeDtypeStruct((m, d), BF16)],
        scratch_shapes=[stage, stage, pltpu.SemaphoreType.DMA((2,))],
        compiler_params=_params("arbitrary"),
        name="moe_combine",
    )(dest3, dest3, x1, gates, g, b, ys)


def _route_plan(idx, n_items):
    e_flat = idx[:, :TOP_K].reshape(-1)
    onehot = (e_flat[:, None] == jnp.arange(N_EXPERTS, dtype=jnp.int32)[None, :]).astype(jnp.int32)
    csum = jnp.cumsum(onehot, axis=0)
    rank = jnp.sum(onehot * csum, axis=1) - 1
    counts = csum[-1]
    items_e = (counts + MOE_ITEM_ROWS - 1) // MOE_ITEM_ROWS
    item_end = jnp.cumsum(items_e)
    item_start = item_end - items_e
    dest = (item_start[e_flat] * MOE_ITEM_ROWS + rank).astype(jnp.int32)
    ids = jnp.arange(n_items, dtype=jnp.int32)
    item_e = jnp.minimum(jnp.searchsorted(item_end, ids, side="right"), N_EXPERTS - 1).astype(jnp.int32)
    n_used = item_end[-1].astype(jnp.int32)
    item_n = jnp.clip(counts[item_e] - (ids - item_start[item_e]) * MOE_ITEM_ROWS, 0, MOE_ITEM_ROWS)
    item_n = jnp.where(ids < n_used, item_n, 0).astype(jnp.int32)
    return dest, item_e, item_n, n_used.reshape(1)


def kernel(x_prompt, x_sample, cache_k, cache_v, page_table, state_gla, state_conv, rel_table,
           w_in, w_out, lam_q1, lam_k1, lam_q2, lam_k2, diff_norm_g, gla_w_up, gla_b, gla_norm_g,
           conv_w, cm_ln_g, cm_ln_b, cm_ws, cm_bs, ln1_g, ln1_b, ln2_g, ln2_b,
           router_group, router_expert, w_gate, w_up, w_down):
    bp, seq, d = x_prompt.shape
    bs, ts, _ = x_sample.shape
    depth = w_in.shape[0]
    gw = w_out.shape[1] // 4
    nh = gw // HEAD_DIM
    hk = nh * HALF_DIM
    page = cache_k.shape[2]
    n_phys = cache_k.shape[1]
    tp = SAMPLE_ROWS
    n_prompt = bp * seq
    n_all = n_prompt + bs * tp
    alpha = (2.0 * depth) ** 0.25
    assert ts <= tp and nh % 2 == 0 and n_prompt % tp == 0

    x_parts = (x_prompt.reshape(n_prompt, d), jnp.pad(x_sample, ((0, 0), (0, tp - ts), (0, 0))).reshape(bs * tp, d))
    xb = jnp.concatenate([x_parts[0].astype(BF16), x_parts[1].astype(BF16)], axis=0)

    bias_tiles = _prompt_bias_tiles(rel_table, ATTN_TILE)
    t_idx = jnp.arange(ts)
    k_idx = jnp.arange(page)
    dist_far = jnp.full((ts, page), 2 * page, jnp.int32)
    dist_last = page + t_idx[:, None] - k_idx[None, :]
    dist_new = t_idx[:, None] - k_idx[None, :]
    ok = jnp.stack([dist_far > 0, dist_last > 0, (dist_new >= 0) & (k_idx[None, :] < ts)])
    b3 = _rel_bias(jnp.stack([dist_far, dist_last, dist_new]), rel_table) * LOG2E
    same_head = jnp.eye(nh, dtype=bool)
    b3 = jnp.where(ok[:, None, :, :, None] & same_head[None, :, None, None, :],
                   jnp.transpose(b3, (0, 3, 1, 2))[..., None], MASK_VALUE)
    bias_s = jnp.tile(b3.reshape(3, nh * ts, page * nh), (1, 2, 1))
    half_mask = (jnp.arange(HEAD_DIM)[None, :] < HALF_DIM) == (jnp.arange(2)[:, None] == 0)

    cache_k4 = cache_k.reshape(depth, n_phys, page * nh, HEAD_DIM)
    cache_v4 = cache_v.reshape(depth, n_phys, page * nh, HEAD_DIM)
    n_items = (TOP_K * n_all) // MOE_ITEM_ROWS + N_EXPERTS
    tm_proj = _pick(n_all, 1024, 16)

    outs = {k: [] for k in ("kp", "vp", "ks", "vs", "gp", "gs", "cp", "cs", "ds")}
    for l in range(depth):
        lam_init = 0.8 - 0.6 * math.exp(-0.3 * l)
        lams = (lam_q1[l][None], lam_k1[l][None], lam_q2[l][None], lam_k2[l][None])
        gn_a = diff_norm_g[l][None]

        w_main = jnp.concatenate([w_in[l, :, :5 * gw], w_in[l, :, 5 * gw + GLA_RANK:]], axis=1).astype(BF16)
        w_gate_cols = jnp.pad(w_in[l, :, 5 * gw:5 * gw + GLA_RANK], ((0, 0), (0, LANES - GLA_RANK))).astype(BF16)
        proj, gproj = _matmul([xb], w_main, 11 * gw, _pick(n_all, 1100, 16), _pick(11 * gw, 512, LANES), "in_proj",
                              w_side=w_gate_cols)
        proj_s = proj[n_prompt:].reshape(bs, tp, 11 * gw)[:, :ts]

        oa_p = _attn_prompt(proj, lams, bias_tiles, gn_a, bp, seq, nh, lam_init)
        q_s = proj_s[..., :gw].reshape(bs, ts, nh, HEAD_DIM) * (HALF_DIM ** -0.5 * LOG2E)
        q2 = (jnp.transpose(q_s, (0, 2, 1, 3))[:, None] * half_mask.astype(F32)[None, :, None, None, :])
        q2 = q2.reshape(bs, 2 * nh * ts, HEAD_DIM).astype(BF16)
        k_s = proj_s[..., gw:2 * gw]
        v_s = proj_s[..., 2 * gw:3 * gw]
        new_rows = lambda a: jnp.pad(a.reshape(bs, ts * nh, HEAD_DIM), ((0, 0), (0, (page - ts) * nh), (0, 0)))
        oa_s = _attn_sample(l, q2, cache_k4, cache_v4, new_rows(k_s), new_rows(v_s), page_table, bias_s, lams, gn_a,
                            lam_init)
        oa_s = jnp.transpose(oa_s.reshape(bs, nh, ts, HEAD_DIM), (0, 2, 1, 3)).reshape(bs, ts, gw)
        oa_s = jnp.pad(oa_s, ((0, 0), (0, tp - ts), (0, 0))).reshape(bs * tp, gw).astype(BF16)
        oa = lax.dynamic_update_slice(oa_p, oa_s, (n_prompt, 0))

        wup = jnp.pad(gla_w_up[l], ((0, LANES - GLA_RANK), (0, 0)))
        wup_hi, wup_lo = _split_bf16(wup)
        gb = gla_b[l][None]
        gn_b = gla_norm_g[l][None]
        s0_s = jnp.transpose(state_gla[l].reshape(bs, nh // 2, 2 * HALF_DIM, HEAD_DIM), (0, 1, 3, 2))
        s0_p = jnp.zeros((bp, nh // 2, HEAD_DIM, LANES), F32)
        ob, sf_p = _gla(proj, gproj, wup_hi, wup_lo, gb, gn_b, s0_p, 0, bp, seq, nh, seq)
        ob, sf_s = _gla(proj, gproj, wup_hi, wup_lo, gb, gn_b, s0_s, n_prompt, bs, tp, nh, ts, prev=(ob,))
        unpair = lambda s: jnp.transpose(s, (0, 1, 3, 2)).reshape(s.shape[0], nh, HALF_DIM, HEAD_DIM)

        bs_t = jnp.transpose(cm_bs[l])
        cw, lg, lb = conv_w[l], cm_ln_g[l][None], cm_ln_b[l][None]
        oc, od, cs_p, _ = _conv_mlp(proj, cw, jnp.zeros((bp, CONV_WIDTH - 1, gw), F32), lg, lb, cm_ws[l], bs_t,
                                    0, bp, seq, nh, seq, False)
        oc, od, cs_s, vd_s = _conv_mlp(proj, cw, state_conv[l], lg, lb, cm_ws[l], bs_t, n_prompt, bs, tp, nh, ts, True,
                                       prev=(oc, od))

        mix = _matmul([oa, ob, oc, od], w_out[l].astype(BF16), d, tm_proj, _pick(d, 1024, LANES), "out_proj")
        wr = jnp.pad(jnp.concatenate([router_group[l], router_expert[l]], axis=1),
                     ((0, 0), (0, LANES - N_GROUPS - N_EXPERTS)))
        wr_hi, wr_lo = _split_bf16(wr)
        x1, ridx, rgate = _ln_router(x_parts, mix, ln1_g[l][None], ln1_b[l][None], wr_hi, wr_lo, alpha)

        dest, item_e, item_n, n_used = _route_plan(ridx, n_items)
        xs = _dispatch(x1, dest, n_items * MOE_ITEM_ROWS)
        ys = _experts(xs, item_e, item_n, n_used, w_gate, w_up, w_down, l, n_items)
        x, xb = _combine(x1, rgate, dest, ys, ln2_g[l][None], ln2_b[l][None], alpha)
        x_parts = (x, None)

        heads = lambda a, n, t: a.reshape(n, t, nh, HEAD_DIM)
        outs["kp"].append(heads(proj[:n_prompt, gw:2 * gw], bp, seq))
        outs["vp"].append(heads(proj[:n_prompt, 2 * gw:3 * gw], bp, seq))
        outs["ks"].append(heads(k_s, bs, ts))
        outs["vs"].append(heads(v_s, bs, ts))
        outs["gp"].append(unpair(sf_p))
        outs["gs"].append(unpair(sf_s))
        outs["cp"].append(cs_p)
        outs["cs"].append(cs_s)
        outs["ds"].append(vd_s.reshape(bs, tp, gw)[:, :ts])

    y_prompt = x[:n_prompt].reshape(bp, seq, d)
    y_sample = x[n_prompt:].reshape(bs, tp, d)[:, :ts]
    st = lambda k: jnp.stack(outs[k])
    return (y_prompt, y_sample, st("kp"), st("vp"), st("ks"), st("vs"), st("gp"), st("gs"),
            st("cp"), st("cs"), st("ds"))
```
